```python
import jax
import jax.numpy as jnp
from jax import lax
import numpy as np

D_MODEL = 2048
BATCH = 2
SEQ = 16384
DEPTH = 2

GRID_W = 64
CTX_LEN = 256
N_BRANCH = 4
W_BR = D_MODEL // 4
W_A = W_BR
W_C = W_BR
HEAD_DIM = 128
N_Q = W_BR // HEAD_DIM
N_KV = N_Q // 2
GROUP = N_Q // N_KV
Q_W = N_Q * HEAD_DIM
KV_W = N_KV * HEAD_DIM
CONV_A = 3
CONV_C = 31
BLOCK = 128
WINDOW = 128
ROT_AXIS = HEAD_DIM // 2
ROPE_THETA = 10000.0
FFN_HIDDEN = -(-8 * D_MODEL // (3 * 256)) * 256
EPS = 1e-6
ATTN_SCALE = HEAD_DIM ** -0.5

OFF_A = 0
OFF_BQ = OFF_A + 3 * W_A
OFF_BK = OFF_BQ + Q_W
OFF_BV = OFF_BK + KV_W
OFF_C = OFF_BV + KV_W
OFF_DQ = OFF_C + 2 * W_C
OFF_DK = OFF_DQ + Q_W
OFF_DV = OFF_DK + KV_W
OFF_G = OFF_DV + KV_W
P_IN = OFF_G + N_BRANCH * D_MODEL

kernel_name = 'hybrid_parallel_gated_dit_block'


def rms_norm(x, g):
    xf = x.astype(jnp.float32)
    y = xf * lax.rsqrt(jnp.mean(xf * xf, axis=-1, keepdims=True) + EPS)
    return (y * g.astype(jnp.float32)).astype(x.dtype)


def layer_norm(x, g, b):
    xf = x.astype(jnp.float32)
    mu = jnp.mean(xf, axis=-1, keepdims=True)
    var = jnp.mean(jnp.square(xf - mu), axis=-1, keepdims=True)
    y = (xf - mu) * lax.rsqrt(var + EPS) * g.astype(jnp.float32) + b.astype(jnp.float32)
    return y.astype(x.dtype)


def depthwise_conv(h, w, b=None):
    k, ch = w.shape
    y = lax.conv_general_dilated(h, w[:, None, :].astype(h.dtype), window_strides=(1,),
                                 padding=[(k // 2, k // 2)],
                                 dimension_numbers=('NWC', 'WIO', 'NWC'),
                                 feature_group_count=ch)
    return y if b is None else y + b


def heads(p, off, n):
    return p[..., off:off + n * HEAD_DIM].reshape(p.shape[:2] + (n, HEAD_DIM))


def axial_rope_tables(row, col):
    inv = ROPE_THETA ** (-jnp.arange(0, ROT_AXIS, 2, dtype=jnp.float32) / ROT_AXIS)
    ar = row.astype(jnp.float32)[:, None] * inv
    ac = col.astype(jnp.float32)[:, None] * inv
    ang = jnp.concatenate([ar, ar, ac, ac], axis=-1)
    return jnp.cos(ang), jnp.sin(ang)


def apply_rope(x, cos, sin):
    xr = x.reshape(x.shape[:-1] + (2, 2, ROT_AXIS // 2))
    rot = jnp.stack([-xr[..., 1, :], xr[..., 0, :]], axis=-2).reshape(x.shape)
    return x * cos + rot * sin


def short_conv_mixer(pa, w):
    bg, cg, h = jnp.split(pa, 3, axis=-1)
    return bg * depthwise_conv(cg * h, w)


def conformer_conv(pc, w, b, g, beta):
    val, gate = jnp.split(pc, 2, axis=-1)
    h = depthwise_conv(val * jax.nn.sigmoid(gate), w, b)
    return jax.nn.silu(layer_norm(h, g, beta))


def window_attention(q, k, v, kc, vc, sink):
    bsz, n_tok = q.shape[:2]
    nb = n_tok // BLOCK
    qb = q.reshape(bsz, nb, BLOCK, N_KV, GROUP, HEAD_DIM)
    pad = ((0, 0), (1, 1), (0, 0), (0, 0), (0, 0))
    kp = jnp.pad(k.reshape(bsz, nb, BLOCK, N_KV, HEAD_DIM), pad)
    vp = jnp.pad(v.reshape(bsz, nb, BLOCK, N_KV, HEAD_DIM), pad)
    kw = jnp.concatenate([kp[:, :-2], kp[:, 1:-1], kp[:, 2:]], axis=2)
    vw = jnp.concatenate([vp[:, :-2], vp[:, 1:-1], vp[:, 2:]], axis=2)
    s_loc = jnp.einsum('bnqhgd,bnkhd->bnhgqk', qb, kw).astype(jnp.float32)
    s_ctx = jnp.einsum('bnqhgd,bchd->bnhgqc', qb, kc).astype(jnp.float32)
    k_off = jnp.arange(3 * BLOCK) - BLOCK
    rel = k_off[None, :] - jnp.arange(BLOCK)[:, None]
    k_abs = jnp.arange(nb)[:, None, None] * BLOCK + k_off[None, None, :]
    valid = (jnp.abs(rel) <= WINDOW)[None] & (k_abs >= 0) & (k_abs < n_tok)
    s_loc = jnp.where(valid[None, :, None, None], s_loc, -jnp.inf)
    s_sink = jnp.broadcast_to(sink.astype(jnp.float32).reshape(N_KV, GROUP, 1, 1), s_loc.shape[:-1] + (1,))
    p = jax.nn.softmax(jnp.concatenate([s_loc, s_ctx, s_sink], axis=-1), axis=-1).astype(v.dtype)
    n_loc = 3 * BLOCK
    n_ctx = kc.shape[1]
    o = (jnp.einsum('bnhgqk,bnkhd->bnqhgd', p[..., :n_loc], vw)
         + jnp.einsum('bnhgqc,bchd->bnqhgd', p[..., n_loc:n_loc + n_ctx], vc))
    return o.reshape(bsz, n_tok, Q_W)


def global_attention(q, k, v, kc, vc):
    bsz, n_tok = q.shape[:2]
    nb = n_tok // BLOCK
    qb = jnp.moveaxis(q.reshape(bsz, nb, BLOCK, N_KV, GROUP, HEAD_DIM), 1, 0)
    k_all = jnp.concatenate([k, kc], axis=1)
    v_all = jnp.concatenate([v, vc], axis=1)

    def one_block(q_blk):
        s = jnp.einsum('bqhgd,bkhd->bhgqk', q_blk, k_all).astype(jnp.float32)
        p = jax.nn.softmax(s, axis=-1).astype(v_all.dtype)
        return jnp.einsum('bhgqk,bkhd->bqhgd', p, v_all)

    o = lax.map(one_block, qb)
    return jnp.moveaxis(o, 0, 1).reshape(bsz, n_tok, Q_W)


def context_attention(qc, kc, vc, sink=None):
    bsz, n = qc.shape[:2]
    qg = qc.reshape(bsz, n, N_KV, GROUP, HEAD_DIM)
    s = jnp.einsum('bqhgd,bkhd->bhgqk', qg, kc).astype(jnp.float32)
    n_k = kc.shape[1]
    if sink is not None:
        s_sink = jnp.broadcast_to(sink.astype(jnp.float32).reshape(N_KV, GROUP, 1, 1), s.shape[:-1] + (1,))
        s = jnp.concatenate([s, s_sink], axis=-1)
    p = jax.nn.softmax(s, axis=-1)[..., :n_k].astype(vc.dtype)
    return jnp.einsum('bhgqk,bkhd->bqhgd', p, vc).reshape(bsz, n, Q_W)


def merge_branches(ys, gate_logits, b_gate, w_branch, w_out):
    g = jax.nn.sigmoid(gate_logits.reshape(gate_logits.shape[:2] + (N_BRANCH, D_MODEL)) + b_gate)
    m = g[:, :, 0] * (ys[0] @ w_branch[0])
    for i in range(1, N_BRANCH):
        m = m + g[:, :, i] * (ys[i] @ w_branch[i])
    return m @ w_out


def swiglu(h, w_in, w_out):
    a, b = jnp.split(h @ w_in, 2, axis=-1)
    return (jax.nn.silu(a) * b) @ w_out


def setup_inputs(seed: int = 0) -> dict:
    key = jax.random.key(seed)
    ks = jax.random.split(key, 24)
    f32 = jnp.float32
    D = D_MODEL

    def nrm(k, shape, scale):
        return jax.random.normal(k, shape, f32) * scale

    return {
        'x': nrm(ks[0], (BATCH, SEQ, D), 1.0),
        'c': nrm(ks[1], (BATCH, D), 1.0),
        'ctx': nrm(ks[2], (BATCH, CTX_LEN, D), 1.0),
        'c_ctx': nrm(ks[3], (D,), 1.0),
        'w_ada': nrm(ks[4], (DEPTH, D, 6 * D), 0.5 * D ** -0.5),
        'b_ada': nrm(ks[5], (DEPTH, 6 * D), 0.02),
        'norm_mix': 1.0 + nrm(ks[6], (DEPTH, D), 0.02),
        'norm_ffn': 1.0 + nrm(ks[7], (DEPTH, D), 0.02),
        'w_in': nrm(ks[8], (DEPTH, D, P_IN), D ** -0.5),
        'b_gate': nrm(ks[9], (DEPTH, N_BRANCH, D), 0.02),
        'conv_a_w': nrm(ks[10], (DEPTH, CONV_A, W_A), CONV_A ** -0.5),
        'sink_b': nrm(ks[11], (DEPTH, N_Q), 0.5),
        'qk_norm_q': 1.0 + nrm(ks[12], (DEPTH, HEAD_DIM), 0.02),
        'qk_norm_k': 1.0 + nrm(ks[13], (DEPTH, HEAD_DIM), 0.02),
        'conv_c_w': nrm(ks[14], (DEPTH, CONV_C, W_C), CONV_C ** -0.5),
        'conv_c_b': nrm(ks[15], (DEPTH, W_C), 0.02),
        'ln_c_g': 1.0 + nrm(ks[16], (DEPTH, W_C), 0.02),
        'ln_c_b': nrm(ks[17], (DEPTH, W_C), 0.02),
        'w_branch': nrm(ks[18], (DEPTH, N_BRANCH, W_BR, D), W_BR ** -0.5),
        'w_out': nrm(ks[19], (DEPTH, D, D), D ** -0.5),
        'w_ffn_in': nrm(ks[20], (DEPTH, D, 2 * FFN_HIDDEN), D ** -0.5),
        'w_ffn_out': nrm(ks[21], (DEPTH, FFN_HIDDEN, D), FFN_HIDDEN ** -0.5),
        'norm_final': 1.0 + nrm(ks[22], (D,), 0.02),
    }


def reference(x, c, ctx, c_ctx, w_ada, b_ada, norm_mix, norm_ffn, w_in, b_gate, conv_a_w, sink_b,
              qk_norm_q, qk_norm_k, conv_c_w, conv_c_b, ln_c_g, ln_c_b, w_branch, w_out,
              w_ffn_in, w_ffn_out, norm_final):
    n_tok = x.shape[1]
    rows = n_tok // GRID_W
    row = jnp.repeat(jnp.arange(rows), GRID_W)
    col = jnp.tile(jnp.arange(GRID_W), rows)
    cos, sin = axial_rope_tables(row, col)
    cos = cos.astype(x.dtype)[:, None, :]
    sin = sin.astype(x.dtype)[:, None, :]

    xc = ctx
    s_c = jax.nn.silu(c)
    s_cc = jax.nn.silu(c_ctx)
    for l in range(DEPTH):
        last = l == DEPTH - 1
        mod = (s_c @ w_ada[l] + b_ada[l])[:, None, :]
        sh_m, sc_m, g_m, sh_f, sc_f, g_f = jnp.split(mod, 6, axis=-1)
        n_c = 2 if last else 6
        mod_c = s_cc @ w_ada[l][:, :n_c * D_MODEL] + b_ada[l][:n_c * D_MODEL]
        mc = jnp.split(mod_c, n_c)

        h = rms_norm(x, norm_mix[l]) * (1.0 + sc_m) + sh_m
        hc = rms_norm(xc, norm_mix[l]) * (1.0 + mc[1]) + mc[0]

        if last:
            kvb_c = hc @ w_in[l][:, OFF_BK:OFF_BK + 2 * KV_W]
            kvd_c = hc @ w_in[l][:, OFF_DK:OFF_DK + 2 * KV_W]
        else:
            pc = hc @ w_in[l]
            kvb_c = pc[..., OFF_BK:OFF_BK + 2 * KV_W]
            kvd_c = pc[..., OFF_DK:OFF_DK + 2 * KV_W]
        kc_b = heads(kvb_c, 0, N_KV)
        vc_b = heads(kvb_c, KV_W, N_KV)
        kc_d = rms_norm(heads(kvd_c, 0, N_KV), qk_norm_k[l])
        vc_d = heads(kvd_c, KV_W, N_KV)

        p = h @ w_in[l]
        y_a = short_conv_mixer(p[..., OFF_A:OFF_A + 3 * W_A], conv_a_w[l])
        q_b = apply_rope(heads(p, OFF_BQ, N_Q), cos, sin) * ATTN_SCALE
        k_b = apply_rope(heads(p, OFF_BK, N_KV), cos, sin)
        y_b = window_attention(q_b, k_b, heads(p, OFF_BV, N_KV), kc_b, vc_b, sink_b[l])
        y_c = conformer_conv(p[..., OFF_C:OFF_C + 2 * W_C], conv_c_w[l], conv_c_b[l], ln_c_g[l], ln_c_b[l])
        q_d = apply_rope(rms_norm(heads(p, OFF_DQ, N_Q), qk_norm_q[l]), cos, sin) * ATTN_SCALE
        k_d = apply_rope(rms_norm(heads(p, OFF_DK, N_KV), qk_norm_k[l]), cos, sin)
        y_d = global_attention(q_d, k_d, heads(p, OFF_DV, N_KV), kc_d, vc_d)
        mix = merge_branches([y_a, y_b, y_c, y_d], p[..., OFF_G:], b_gate[l], w_branch[l], w_out[l])
        x = x + g_m * mix

        if not last:
            yc_a = short_conv_mixer(pc[..., OFF_A:OFF_A + 3 * W_A], conv_a_w[l])
            yc_b = context_attention(heads(pc, OFF_BQ, N_Q) * ATTN_SCALE, kc_b, vc_b, sink_b[l])
            yc_c = conformer_conv(pc[..., OFF_C:OFF_C + 2 * W_C], conv_c_w[l], conv_c_b[l], ln_c_g[l], ln_c_b[l])
            qc_d = rms_norm(heads(pc, OFF_DQ, N_Q), qk_norm_q[l]) * ATTN_SCALE
            yc_d = context_attention(qc_d, kc_d, vc_d)
            mix_c = merge_branches([yc_a, yc_b, yc_c, yc_d], pc[..., OFF_G:], b_gate[l], w_branch[l], w_out[l])
            xc = xc + mc[2] * mix_c
            hc2 = rms_norm(xc, norm_ffn[l]) * (1.0 + mc[4]) + mc[3]
            xc = xc + mc[5] * swiglu(hc2, w_ffn_in[l], w_ffn_out[l])

        h2 = rms_norm(x, norm_ffn[l]) * (1.0 + sc_f) + sh_f
        x = x + g_f * swiglu(h2, w_ffn_in[l], w_ffn_out[l])

    return rms_norm(x, norm_final)
```

```python
import functools

import jax
import jax.numpy as jnp
from jax import lax
from jax.experimental import pallas as pl
from jax.experimental.pallas import tpu as pltpu

F32 = jnp.float32
BF16 = jnp.bfloat16

D_MODEL = 2048
N_BRANCH = 4
W_BR = D_MODEL // 4
HEAD_DIM = 128
N_Q = W_BR // HEAD_DIM
N_KV = N_Q // 2
GROUP = N_Q // N_KV
Q_W = N_Q * HEAD_DIM
KV_W = N_KV * HEAD_DIM
CONV_A = 3
CONV_C = 31
WINDOW = 128
GRID_W = 64
ROT_AXIS = HEAD_DIM // 2
ROPE_THETA = 10000.0
FFN_HIDDEN = 5632
EPS = 1e-6
ATTN_SCALE = HEAD_DIM ** -0.5
NEG_BIG = -1e30

OFF_A = 0
OFF_BQ = OFF_A + 3 * W_BR
OFF_BK = OFF_BQ + Q_W
OFF_BV = OFF_BK + KV_W
OFF_C = OFF_BV + KV_W
OFF_DQ = OFF_C + 2 * W_BR
OFF_DK = OFF_DQ + Q_W
OFF_DV = OFF_DK + KV_W
OFF_G = OFF_DV + KV_W

PF_W = 3 * W_BR + 2 * W_BR
PQ_W = 2 * (Q_W + 2 * KV_W)
TN = 512
N_PF_TILES = PF_W // TN
N_PROJ_TILES = (PF_W + PQ_W) // TN

VMEM_LIMIT = 56 * 1024 * 1024


def _cparams(sem):
    return pltpu.CompilerParams(dimension_semantics=sem, vmem_limit_bytes=VMEM_LIMIT)


def _sigmoid(z):
    return 1.0 / (1.0 + jnp.exp(-z))


def _ada_kernel(c_ref, w_ref, b_ref, o_ref):
    c = c_ref[...]
    s = (c * _sigmoid(c)).astype(BF16)
    o_ref[...] = jnp.dot(s, w_ref[...].astype(BF16), preferred_element_type=F32) + b_ref[...]


ADA_ROWS = 16


def _ada(cvec, w, b):
    n = w.shape[1]
    tn = 1024
    return pl.pallas_call(
        _ada_kernel,
        out_shape=jax.ShapeDtypeStruct((ADA_ROWS, n), F32),
        grid=(n // tn,),
        in_specs=[pl.BlockSpec((ADA_ROWS, D_MODEL), lambda j: (0, 0)),
                  pl.BlockSpec((D_MODEL, tn), lambda j: (0, j)),
                  pl.BlockSpec((1, tn), lambda j: (0, j))],
        out_specs=pl.BlockSpec((ADA_ROWS, tn), lambda j: (0, j)),
        compiler_params=_cparams(("arbitrary",)),
        name="ada",
    )(cvec, w, b)


def _rope(xh, cos, sn, sp):
    return xh * cos + pltpu.roll(xh, HEAD_DIM - ROT_AXIS // 2, 1) * sn + pltpu.roll(xh, ROT_AXIS // 2, 1) * sp


def _head_norm(xh, g):
    return xh * lax.rsqrt(jnp.mean(xh * xh, axis=-1, keepdims=True) + EPS) * g


def _inproj_kernel(x_ref, g_ref, sc_ref, sh_ref, w_ref, cos_ref, sn_ref, sp_ref, gq_ref, gk_ref,
                   pf_ref, pq_ref, h_ref):
    j = pl.program_id(1)

    @pl.when(j == 0)
    def _():
        x = x_ref[...]
        y = x * lax.rsqrt(jnp.mean(x * x, axis=-1, keepdims=True) + EPS) * g_ref[...]
        h_ref[...] = (y * (1.0 + sc_ref[0]) + sh_ref[0]).astype(BF16)

    acc = jnp.dot(h_ref[...], w_ref[...], preferred_element_type=F32)

    @pl.when(j < N_PF_TILES)
    def _():
        pf_ref[...] = acc

    def rope(xh):
        return _rope(xh, cos_ref[...], sn_ref[...], sp_ref[...])

    def head(k):
        return acc[:, k * HEAD_DIM:(k + 1) * HEAD_DIM]

    def put(k, v):
        pq_ref[:, k * HEAD_DIM:(k + 1) * HEAD_DIM] = v.astype(BF16)

    @pl.when(j == N_PF_TILES)
    def _():
        for k in range(N_Q):
            put(k, rope(head(k)) * ATTN_SCALE)

    @pl.when(j == N_PF_TILES + 1)
    def _():
        for k in range(N_KV):
            put(k, rope(head(k)))
        pq_ref[:, KV_W:] = acc[:, KV_W:].astype(BF16)

    @pl.when(j == N_PF_TILES + 2)
    def _():
        for k in range(N_Q):
            put(k, rope(_head_norm(head(k), gq_ref[...])) * ATTN_SCALE)

    @pl.when(j == N_PF_TILES + 3)
    def _():
        for k in range(N_KV):
            put(k, rope(_head_norm(head(k), gk_ref[...])))
        pq_ref[:, KV_W:] = acc[:, KV_W:].astype(BF16)


def _inproj(x, g, sc, sh, w, cos, sn, sp, gq, gk, *, seq, tm):
    t = x.shape[0]
    nbt = seq // tm
    return pl.pallas_call(
        _inproj_kernel,
        out_shape=(jax.ShapeDtypeStruct((t, PF_W), F32),
                   jax.ShapeDtypeStruct((t, PQ_W), BF16),
                   jax.ShapeDtypeStruct((t, D_MODEL), BF16)),
        grid=(t // tm, N_PROJ_TILES),
        in_specs=[pl.BlockSpec((tm, D_MODEL), lambda i, j: (i, 0)),
                  pl.BlockSpec((1, D_MODEL), lambda i, j: (0, 0)),
                  pl.BlockSpec((1, 1, D_MODEL), lambda i, j: (i // nbt, 0, 0)),
                  pl.BlockSpec((1, 1, D_MODEL), lambda i, j: (i // nbt, 0, 0)),
                  pl.BlockSpec((D_MODEL, TN), lambda i, j: (0, j)),
                  pl.BlockSpec((tm, HEAD_DIM), lambda i, j: (i % nbt, 0)),
                  pl.BlockSpec((tm, HEAD_DIM), lambda i, j: (i % nbt, 0)),
                  pl.BlockSpec((tm, HEAD_DIM), lambda i, j: (i % nbt, 0)),
                  pl.BlockSpec((1, HEAD_DIM), lambda i, j: (0, 0)),
                  pl.BlockSpec((1, HEAD_DIM), lambda i, j: (0, 0))],
        out_specs=(pl.BlockSpec((tm, TN), lambda i, j: (i, jnp.minimum(j, N_PF_TILES - 1))),
                   pl.BlockSpec((tm, TN), lambda i, j: (i, jnp.clip(j - N_PF_TILES, 0, PQ_W // TN - 1))),
                   pl.BlockSpec((tm, D_MODEL), lambda i, j: (i, 0))),
        compiler_params=_cparams(("parallel", "arbitrary")),
        name="inproj",
    )(x, g, sc, sh, w, cos, sn, sp, gq, gk)


HALO_A = 8


def _conv_a_kernel(bg_ref, cg_ref, h_ref, cgp_ref, hp_ref, cgn_ref, hn_ref, w_ref, o_ref, u_ref, *, nbt):
    it = pl.program_id(0) % nbt
    tm = bg_ref.shape[0]
    u_ref[HALO_A:HALO_A + tm, :] = cg_ref[...] * h_ref[...]
    u_ref[0:HALO_A, :] = jnp.where(it > 0, cgp_ref[...] * hp_ref[...], 0.0)
    u_ref[HALO_A + tm:, :] = jnp.where(it < nbt - 1, cgn_ref[...] * hn_ref[...], 0.0)
    y = (w_ref[0:1, :] * u_ref[HALO_A - 1:HALO_A - 1 + tm, :]
         + w_ref[1:2, :] * u_ref[HALO_A:HALO_A + tm, :]
         + w_ref[2:3, :] * u_ref[HALO_A + 1:HALO_A + 1 + tm, :])
    o_ref[...] = (bg_ref[...] * y).astype(BF16)


def _conv_a(pf, w, *, seq, tm):
    t = pf.shape[0]
    nbt = seq // tm
    r = tm // HALO_A
    last = t // HALO_A - 1

    def cur(c):
        return pl.BlockSpec((tm, W_BR), lambda i: (i, c))

    def prev(c):
        return pl.BlockSpec((HALO_A, W_BR), lambda i: (jnp.maximum(i * r - 1, 0), c))

    def nxt(c):
        return pl.BlockSpec((HALO_A, W_BR), lambda i: (jnp.minimum((i + 1) * r, last), c))

    return pl.pallas_call(
        functools.partial(_conv_a_kernel, nbt=nbt),
        out_shape=jax.ShapeDtypeStruct((t, W_BR), BF16),
        grid=(t // tm,),
        in_specs=[cur(0), cur(1), cur(2), prev(1), prev(2), nxt(1), nxt(2),
                  pl.BlockSpec((CONV_A, W_BR), lambda i: (0, 0))],
        out_specs=pl.BlockSpec((tm, W_BR), lambda i: (i, 0)),
        scratch_shapes=[pltpu.VMEM((tm + 2 * HALO_A, W_BR), F32)],
        compiler_params=_cparams(("parallel",)),
        name="conv_a",
    )(pf, pf, pf, pf, pf, pf, pf, w)


HALO_C = 16
ROWS_C = 64


def _conv_c_kernel(v_ref, g_ref, vp_ref, gp_ref, vn_ref, gn_ref, w_ref, b_ref, lg_ref, lb_ref,
                   o_ref, u_ref, *, nbt):
    it = pl.program_id(0) % nbt
    tm = v_ref.shape[0]
    u_ref[HALO_C:HALO_C + tm, :] = v_ref[...] * _sigmoid(g_ref[...])
    u_ref[0:HALO_C, :] = jnp.where(it > 0, vp_ref[...] * _sigmoid(gp_ref[...]), 0.0)
    u_ref[HALO_C + tm:, :] = jnp.where(it < nbt - 1, vn_ref[...] * _sigmoid(gn_ref[...]), 0.0)
    pad = CONV_C // 2
    for c in range(tm // ROWS_C):
        base = HALO_C + c * ROWS_C - pad
        acc = w_ref[0:1, :] * u_ref[base:base + ROWS_C, :]
        for k in range(1, CONV_C):
            acc = acc + w_ref[k:k + 1, :] * u_ref[base + k:base + k + ROWS_C, :]
        acc = acc + b_ref[...]
        mu = jnp.mean(acc, axis=-1, keepdims=True)
        xc = acc - mu
        var = jnp.mean(xc * xc, axis=-1, keepdims=True)
        y = xc * lax.rsqrt(var + EPS) * lg_ref[...] + lb_ref[...]
        o_ref[c * ROWS_C:(c + 1) * ROWS_C, :] = (y * _sigmoid(y)).astype(BF16)


def _conv_c(pf, w, b, lg, lb, *, seq, tm):
    t = pf.shape[0]
    nbt = seq // tm
    r = tm // HALO_C
    last = t // HALO_C - 1
    c0 = 3

    def cur(c):
        return pl.BlockSpec((tm, W_BR), lambda i: (i, c))

    def prev(c):
        return pl.BlockSpec((HALO_C, W_BR), lambda i: (jnp.maximum(i * r - 1, 0), c))

    def nxt(c):
        return pl.BlockSpec((HALO_C, W_BR), lambda i: (jnp.minimum((i + 1) * r, last), c))

    def vec():
        return pl.BlockSpec((1, W_BR), lambda i: (0, 0))

    return pl.pallas_call(
        functools.partial(_conv_c_kernel, nbt=nbt),
        out_shape=jax.ShapeDtypeStruct((t, W_BR), BF16),
        grid=(t // tm,),
        in_specs=[cur(c0), cur(c0 + 1), prev(c0), prev(c0 + 1), nxt(c0), nxt(c0 + 1),
                  pl.BlockSpec((CONV_C, W_BR), lambda i: (0, 0)), vec(), vec(), vec()],
        out_specs=pl.BlockSpec((tm, W_BR), lambda i: (i, 0)),
        scratch_shapes=[pltpu.VMEM((tm + 2 * HALO_C, W_BR), F32)],
        compiler_params=_cparams(("parallel",)),
        name="conv_c",
    )(pf, pf, pf, pf, pf, pf, w, b, lg, lb)


def _dot_nt(a, b):
    return lax.dot_general(a, b, (((1,), (1,)), ((), ())), preferred_element_type=F32)


def _win_kernel(sink_ref, q_ref, kp_ref, kc_ref, kn_ref, kx_ref, vp_ref, vc_ref, vn_ref, vx_ref, o_ref, *, nq):
    hk = pl.program_id(1)
    i = pl.program_id(2)
    tq = q_ref.shape[0]
    n_loc = tq + 2 * WINDOW
    kk = jnp.concatenate([kp_ref[...], kc_ref[...], kn_ref[...], kx_ref[...]], axis=0)
    vv = jnp.concatenate([vp_ref[...], vc_ref[...], vn_ref[...], vx_ref[...]], axis=0)
    n_all = kk.shape[0]
    row = lax.broadcasted_iota(jnp.int32, (tq, n_all), 0)
    col = lax.broadcasted_iota(jnp.int32, (tq, n_all), 1)
    rel = col - WINDOW - row
    lo = jnp.where(i > 0, 0, WINDOW)
    hi = jnp.where(i < nq - 1, n_loc, tq + WINDOW)
    valid = (col >= n_loc) | ((jnp.abs(rel) <= WINDOW) & (col >= lo) & (col < hi))
    for g in range(GROUP):
        q = q_ref[:, g * HEAD_DIM:(g + 1) * HEAD_DIM]
        s = jnp.where(valid, _dot_nt(q, kk), NEG_BIG)
        sink = sink_ref[hk * GROUP + g]
        m = jnp.maximum(jnp.max(s, axis=-1, keepdims=True), sink)
        p = jnp.exp(s - m)
        l = jnp.sum(p, axis=-1, keepdims=True) + jnp.exp(sink - m)
        o = jnp.dot(p.astype(BF16), vv, preferred_element_type=F32) / l
        o_ref[:, g * HEAD_DIM:(g + 1) * HEAD_DIM] = o.astype(BF16)


def _win_attn(pq, pq_c, sink, *, batch, seq, ctx, tq):
    nq = seq // tq
    r = tq // WINDOW
    nblk = seq // WINDOW
    kcol = Q_W // HEAD_DIM
    vcol = (Q_W + KV_W) // HEAD_DIM

    def cur(c0):
        return pl.BlockSpec((tq, HEAD_DIM), lambda b, h, i: (b * nq + i, c0 + h))

    def prev(c0):
        return pl.BlockSpec((WINDOW, HEAD_DIM), lambda b, h, i: (b * nblk + jnp.maximum(i * r - 1, 0), c0 + h))

    def nxt(c0):
        return pl.BlockSpec((WINDOW, HEAD_DIM),
                            lambda b, h, i: (b * nblk + jnp.minimum((i + 1) * r, nblk - 1), c0 + h))

    def cx(c0):
        return pl.BlockSpec((ctx, HEAD_DIM), lambda b, h, i: (b, c0 + h))

    return pl.pallas_call(
        functools.partial(_win_kernel, nq=nq),
        out_shape=jax.ShapeDtypeStruct((batch * seq, Q_W), BF16),
        grid=(batch, N_KV, nq),
        in_specs=[pl.BlockSpec(memory_space=pltpu.SMEM),
                  pl.BlockSpec((tq, GROUP * HEAD_DIM), lambda b, h, i: (b * nq + i, h)),
                  prev(kcol), cur(kcol), nxt(kcol), cx(kcol),
                  prev(vcol), cur(vcol), nxt(vcol), cx(vcol)],
        out_specs=pl.BlockSpec((tq, GROUP * HEAD_DIM), lambda b, h, i: (b * nq + i, h)),
        compiler_params=_cparams(("parallel", "parallel", "parallel")),
        name="win_attn",
    )(sink, pq, pq, pq, pq, pq_c, pq, pq, pq, pq_c)


def _flash_kernel(sink_ref, q_ref, k_ref, v_ref, *rest, n_main, has_extra, use_sink):
    if has_extra:
        kx_ref, vx_ref, o_ref, q2_ref, m_ref, l_ref, acc_ref = rest
    else:
        o_ref, q2_ref, m_ref, l_ref, acc_ref = rest
    hk = pl.program_id(1)
    j = pl.program_id(3)
    tq = q_ref.shape[0]

    @pl.when(j == 0)
    def _():
        for g in range(GROUP):
            q2_ref[g * tq:(g + 1) * tq, :] = q_ref[:, g * HEAD_DIM:(g + 1) * HEAD_DIM]
            if use_sink:
                m_ref[g * tq:(g + 1) * tq, :] = jnp.full((tq, HEAD_DIM), sink_ref[hk * GROUP + g], F32)
        if use_sink:
            l_ref[...] = jnp.ones(l_ref.shape, F32)
        else:
            m_ref[...] = jnp.full(m_ref.shape, NEG_BIG, F32)
            l_ref[...] = jnp.zeros(l_ref.shape, F32)
        acc_ref[...] = jnp.zeros(acc_ref.shape, F32)

    def step(k, v):
        s = _dot_nt(q2_ref[...], k)
        m_prev = m_ref[...]
        m_next = jnp.maximum(m_prev, jnp.max(s, axis=-1, keepdims=True))
        alpha = jnp.exp(m_prev - m_next)
        p = jnp.exp(s - m_next[:, 0:1])
        l_ref[...] = alpha * l_ref[...] + jnp.sum(p, axis=-1, keepdims=True)
        acc_ref[...] = alpha * acc_ref[...] + jnp.dot(p.astype(BF16), v, preferred_element_type=F32)
        m_ref[...] = m_next

    @pl.when(j < n_main)
    def _():
        step(k_ref[...], v_ref[...])

    if has_extra:
        @pl.when(j == n_main)
        def _():
            step(kx_ref[...], vx_ref[...])

    @pl.when(j == n_main + (1 if has_extra else 0) - 1)
    def _():
        o = acc_ref[...] / l_ref[...]
        for g in range(GROUP):
            o_ref[:, g * HEAD_DIM:(g + 1) * HEAD_DIM] = o[g * tq:(g + 1) * tq, :].astype(BF16)


def _flash(q_arr, kv_arr, kx_arr, sink, *, batch, seq_q, seq_k, seq_x, tq, tk, q_off, k_off, v_off, use_sink):
    nq = seq_q // tq
    n_main = seq_k // tk
    has_extra = kx_arr is not None
    n_steps = n_main + (1 if has_extra else 0)
    qc = q_off // (GROUP * HEAD_DIM)
    kc = k_off // HEAD_DIM
    vc = v_off // HEAD_DIM

    def kv(c0):
        return pl.BlockSpec((tk, HEAD_DIM), lambda b, h, i, j: (b * n_main + jnp.minimum(j, n_main - 1), c0 + h))

    def kx(c0):
        return pl.BlockSpec((seq_x, HEAD_DIM), lambda b, h, i, j: (b, c0 + h))

    in_specs = [pl.BlockSpec(memory_space=pltpu.SMEM),
                pl.BlockSpec((tq, GROUP * HEAD_DIM), lambda b, h, i, j: (b * nq + i, qc + h)),
                kv(kc), kv(vc)]
    args = [sink, q_arr, kv_arr, kv_arr]
    if has_extra:
        in_specs += [kx(kc), kx(vc)]
        args += [kx_arr, kx_arr]
    rows = GROUP * tq
    return pl.pallas_call(
        functools.partial(_flash_kernel, n_main=n_main, has_extra=has_extra, use_sink=use_sink),
        out_shape=jax.ShapeDtypeStruct((batch * seq_q, Q_W), BF16),
        grid=(batch, N_KV, nq, n_steps),
        in_specs=in_specs,
        out_specs=pl.BlockSpec((tq, GROUP * HEAD_DIM), lambda b, h, i, j: (b * nq + i, h)),
        scratch_shapes=[pltpu.VMEM((rows, HEAD_DIM), BF16),
                        pltpu.VMEM((rows, HEAD_DIM), F32),
                        pltpu.VMEM((rows, HEAD_DIM), F32),
                        pltpu.VMEM((rows, HEAD_DIM), F32)],
        compiler_params=_cparams(("parallel", "parallel", "parallel", "arbitrary")),
        name="flash",
    )(*args)


def _merge_kernel(h_ref, ya_ref, yb_ref, yc_ref, yd_ref, wga_ref, wgb_ref, wgc_ref, wgd_ref,
                  wb_ref, bg_ref, o_ref):
    h = h_ref[...]
    m = None
    for k, (y_ref, wg_ref) in enumerate(((ya_ref, wga_ref), (yb_ref, wgb_ref), (yc_ref, wgc_ref), (yd_ref, wgd_ref))):
        logits = jnp.dot(h, wg_ref[...], preferred_element_type=F32) + bg_ref[k:k + 1, :]
        t = _sigmoid(logits) * jnp.dot(y_ref[...], wb_ref[k], preferred_element_type=F32)
        m = t if m is None else m + t
    o_ref[...] = m.astype(BF16)


def _merge(h, ys, wg, wb, bg, *, tm, tn):
    t = h.shape[0]
    nct = D_MODEL // tn

    def y():
        return pl.BlockSpec((tm, W_BR), lambda i, j: (i, 0))

    def g(k):
        return pl.BlockSpec((D_MODEL, tn), lambda i, j: (0, k * nct + j))

    return pl.pallas_call(
        _merge_kernel,
        out_shape=jax.ShapeDtypeStruct((t, D_MODEL), BF16),
        grid=(t // tm, nct),
        in_specs=[pl.BlockSpec((tm, D_MODEL), lambda i, j: (i, 0)), y(), y(), y(), y(),
                  g(0), g(1), g(2), g(3),
                  pl.BlockSpec((N_BRANCH, W_BR, tn), lambda i, j: (0, 0, j)),
                  pl.BlockSpec((N_BRANCH, tn), lambda i, j: (0, j))],
        out_specs=pl.BlockSpec((tm, tn), lambda i, j: (i, j)),
        compiler_params=_cparams(("parallel", "arbitrary")),
        name="merge",
    )(h, *ys, wg, wg, wg, wg, wb, bg)


def _resid_kernel(a_ref, w_ref, x_ref, gate_ref, o_ref):
    o_ref[...] = x_ref[...] + gate_ref[0] * jnp.dot(a_ref[...], w_ref[...], preferred_element_type=F32)


def _resid(a, w, x, gate, *, seq, tm, tn):
    t, k = a.shape
    nbt = seq // tm
    return pl.pallas_call(
        _resid_kernel,
        out_shape=jax.ShapeDtypeStruct((t, D_MODEL), F32),
        grid=(t // tm, D_MODEL // tn),
        in_specs=[pl.BlockSpec((tm, k), lambda i, j: (i, 0)),
                  pl.BlockSpec((k, tn), lambda i, j: (0, j)),
                  pl.BlockSpec((tm, tn), lambda i, j: (i, j)),
                  pl.BlockSpec((1, 1, tn), lambda i, j: (i // nbt, 0, j))],
        out_specs=pl.BlockSpec((tm, tn), lambda i, j: (i, j)),
        compiler_params=_cparams(("parallel", "arbitrary")),
        name="resid",
    )(a, w, x, gate)


def _ffn_in_kernel(x_ref, g_ref, sc_ref, sh_ref, wa_ref, wb_ref, o_ref, h_ref):
    @pl.when(pl.program_id(1) == 0)
    def _():
        x = x_ref[...]
        y = x * lax.rsqrt(jnp.mean(x * x, axis=-1, keepdims=True) + EPS) * g_ref[...]
        h_ref[...] = (y * (1.0 + sc_ref[0]) + sh_ref[0]).astype(BF16)

    h = h_ref[...]
    a = jnp.dot(h, wa_ref[...], preferred_element_type=F32)
    b = jnp.dot(h, wb_ref[...], preferred_element_type=F32)
    o_ref[...] = (a * _sigmoid(a) * b).astype(BF16)


def _ffn_in(x, g, sc, sh, w, *, seq, tm, tn):
    t = x.shape[0]
    nbt = seq // tm
    nct = FFN_HIDDEN // tn
    return pl.pallas_call(
        _ffn_in_kernel,
        out_shape=jax.ShapeDtypeStruct((t, FFN_HIDDEN), BF16),
        grid=(t // tm, nct),
        in_specs=[pl.BlockSpec((tm, D_MODEL), lambda i, j: (i, 0)),
                  pl.BlockSpec((1, D_MODEL), lambda i, j: (0, 0)),
                  pl.BlockSpec((1, 1, D_MODEL), lambda i, j: (i // nbt, 0, 0)),
                  pl.BlockSpec((1, 1, D_MODEL), lambda i, j: (i // nbt, 0, 0)),
                  pl.BlockSpec((D_MODEL, tn), lambda i, j: (0, j)),
                  pl.BlockSpec((D_MODEL, tn), lambda i, j: (0, nct + j))],
        out_specs=pl.BlockSpec((tm, tn), lambda i, j: (i, j)),
        scratch_shapes=[pltpu.VMEM((tm, D_MODEL), BF16)],
        compiler_params=_cparams(("parallel", "arbitrary")),
        name="ffn_in",
    )(x, g, sc, sh, w, w)


def _final_kernel(x_ref, g_ref, o_ref):
    x = x_ref[...]
    o_ref[...] = x * lax.rsqrt(jnp.mean(x * x, axis=-1, keepdims=True) + EPS) * g_ref[...]


def _final_norm(x, g, *, tm):
    t = x.shape[0]
    return pl.pallas_call(
        _final_kernel,
        out_shape=jax.ShapeDtypeStruct((t, D_MODEL), F32),
        grid=(t // tm,),
        in_specs=[pl.BlockSpec((tm, D_MODEL), lambda i: (i, 0)),
                  pl.BlockSpec((1, D_MODEL), lambda i: (0, 0))],
        out_specs=pl.BlockSpec((tm, D_MODEL), lambda i: (i, 0)),
        compiler_params=_cparams(("parallel",)),
        name="final_norm",
    )(x, g)


def _rope_tables(seq):
    pos = jnp.arange(seq)
    row = (pos // GRID_W).astype(F32)
    col = (pos % GRID_W).astype(F32)
    inv = ROPE_THETA ** (-jnp.arange(0, ROT_AXIS, 2, dtype=F32) / ROT_AXIS)
    ar = row[:, None] * inv
    ac = col[:, None] * inv
    ang = jnp.concatenate([ar, ar, ac, ac], axis=-1)
    cos, sin = jnp.cos(ang), jnp.sin(ang)
    first = (jnp.arange(HEAD_DIM) % ROT_AXIS) < ROT_AXIS // 2
    return cos, jnp.where(first, -sin, 0.0), jnp.where(first, 0.0, sin)


def _tile(n, pref):
    return pref if n % pref == 0 else n


def _forward(x, c, ctx, c_ctx, w_ada, b_ada, norm_mix, norm_ffn, w_in, b_gate, conv_a_w, sink_b,
             qk_norm_q, qk_norm_k, conv_c_w, conv_c_b, ln_c_g, ln_c_b, w_branch, w_out,
             w_ffn_in, w_ffn_out, norm_final):
    batch, seq, _ = x.shape
    n_ctx = ctx.shape[1]
    depth = w_in.shape[0]
    xs = x.reshape(batch * seq, D_MODEL)
    xc = ctx.reshape(batch * n_ctx, D_MODEL)

    tm = _tile(seq, 1024)
    tm_conv_a = _tile(seq, 512)
    tm_conv_c = _tile(seq, 256)
    tq_win = _tile(seq, 512)
    tq = _tile(seq, 512)
    tk = _tile(seq, 1024)

    rope_x = _rope_tables(seq)
    rope_c = (jnp.ones((n_ctx, HEAD_DIM), F32), jnp.zeros((n_ctx, HEAD_DIM), F32),
              jnp.zeros((n_ctx, HEAD_DIM), F32))
    cvec = jnp.zeros((ADA_ROWS, D_MODEL), F32).at[:batch].set(c).at[batch].set(c_ctx)

    def row(v):
        return v.reshape(1, -1)

    def mixers(pf, pq, pq_c, l, *, s, tma, tmc, is_ctx):
        y_a = _conv_a(pf, conv_a_w[l], seq=s, tm=tma)
        y_c = _conv_c(pf, conv_c_w[l], row(conv_c_b[l]), row(ln_c_g[l]), row(ln_c_b[l]), seq=s, tm=tmc)
        if is_ctx:
            y_b = _flash(pq, pq, None, sink_b[l], batch=batch, seq_q=s, seq_k=s, seq_x=0, tq=s, tk=s,
                         q_off=0, k_off=Q_W, v_off=Q_W + KV_W, use_sink=True)
            y_d = _flash(pq, pq, None, sink_b[l], batch=batch, seq_q=s, seq_k=s, seq_x=0, tq=s, tk=s,
                         q_off=Q_W + 2 * KV_W, k_off=2 * Q_W + 2 * KV_W, v_off=2 * Q_W + 3 * KV_W,
                         use_sink=False)
        else:
            y_b = _win_attn(pq, pq_c, sink_b[l], batch=batch, seq=s, ctx=n_ctx, tq=tq_win)
            y_d = _flash(pq, pq, pq_c, sink_b[l], batch=batch, seq_q=s, seq_k=s, seq_x=n_ctx, tq=tq, tk=tk,
                         q_off=Q_W + 2 * KV_W, k_off=2 * Q_W + 2 * KV_W, v_off=2 * Q_W + 3 * KV_W,
                         use_sink=False)
        return [y_a, y_b, y_c, y_d]

    for l in range(depth):
        last = l == depth - 1
        wl = w_in[l]
        w_proj = jnp.concatenate([wl[:, OFF_A:OFF_BQ], wl[:, OFF_C:OFF_DQ],
                                  wl[:, OFF_BQ:OFF_C], wl[:, OFF_DQ:OFF_G]], axis=1).astype(BF16)
        w_gate = wl[:, OFF_G:].astype(BF16)
        w_br = w_branch[l].astype(BF16)
        w_o = w_out[l].astype(BF16)
        w_f1 = w_ffn_in[l].astype(BF16)
        w_f2 = w_ffn_out[l].astype(BF16)

        mod = _ada(cvec, w_ada[l], row(b_ada[l]))
        mx = mod[:batch].reshape(batch, 1, 6, D_MODEL)
        mcx = jnp.broadcast_to(mod[batch].reshape(1, 1, 6, D_MODEL), (batch, 1, 6, D_MODEL))
        sh_m, sc_m, g_m, sh_f, sc_f, g_f = (mx[:, :, k] for k in range(6))
        csh_m, csc_m, cg_m, csh_f, csc_f, cg_f = (mcx[:, :, k] for k in range(6))
        gq, gk = row(qk_norm_q[l]), row(qk_norm_k[l])

        pf_c, pq_c, h_c = _inproj(xc, row(norm_mix[l]), csc_m, csh_m, w_proj, *rope_c, gq, gk,
                                  seq=n_ctx, tm=n_ctx)
        pf, pq, h = _inproj(xs, row(norm_mix[l]), sc_m, sh_m, w_proj, *rope_x, gq, gk, seq=seq, tm=tm)

        ys = mixers(pf, pq, pq_c, l, s=seq, tma=tm_conv_a, tmc=tm_conv_c, is_ctx=False)
        m = _merge(h, ys, w_gate, w_br, b_gate[l], tm=tm, tn=512)
        xs = _resid(m, w_o, xs, g_m, seq=seq, tm=tm, tn=512)

        if not last:
            ys_c = mixers(pf_c, pq_c, None, l, s=n_ctx, tma=n_ctx, tmc=n_ctx, is_ctx=True)
            m_c = _merge(h_c, ys_c, w_gate, w_br, b_gate[l], tm=n_ctx, tn=512)
            xc = _resid(m_c, w_o, xc, cg_m, seq=n_ctx, tm=n_ctx, tn=512)
            hid_c = _ffn_in(xc, row(norm_ffn[l]), csc_f, csh_f, w_f1, seq=n_ctx, tm=n_ctx, tn=512)
            xc = _resid(hid_c, w_f2, xc, cg_f, seq=n_ctx, tm=n_ctx, tn=256)

        hid = _ffn_in(xs, row(norm_ffn[l]), sc_f, sh_f, w_f1, seq=seq, tm=tm, tn=512)
        xs = _resid(hid, w_f2, xs, g_f, seq=seq, tm=tm, tn=256)

    out = _final_norm(xs, row(norm_final), tm=_tile(batch * seq, 512))
    return out.reshape(batch, seq, D_MODEL)


def kernel(x, c, ctx, c_ctx, w_ada, b_ada, norm_mix, norm_ffn, w_in, b_gate, conv_a_w, sink_b, qk_norm_q,
           qk_norm_k, conv_c_w, conv_c_b, ln_c_g, ln_c_b, w_branch, w_out, w_ffn_in, w_ffn_out, norm_final):
    return _forward(x, c, ctx, c_ctx, w_ada, b_ada, norm_mix, norm_ffn, w_in, b_gate, conv_a_w, sink_b,
                    qk_norm_q, qk_norm_k, conv_c_w, conv_c_b, ln_c_g, ln_c_b, w_branch, w_out,
                    w_ffn_in, w_ffn_out, norm_final)
```

```python
import functools

import jax
import jax.numpy as jnp
from jax import lax
from jax.experimental import pallas as pl
from jax.experimental.pallas import tpu as pltpu

F32 = jnp.float32
BF16 = jnp.bfloat16

D_MODEL = 2048
N_BRANCH = 4
W_BR = D_MODEL // 4
HEAD_DIM = 128
N_Q = W_BR // HEAD_DIM
N_KV = N_Q // 2
GROUP = N_Q // N_KV
Q_W = N_Q * HEAD_DIM
KV_W = N_KV * HEAD_DIM
CONV_A = 3
CONV_C = 31
WINDOW = 128
GRID_W = 64
ROT_AXIS = HEAD_DIM // 2
ROPE_THETA = 10000.0
FFN_HIDDEN = 5632
EPS = 1e-6
ATTN_SCALE = HEAD_DIM ** -0.5
LOG2E = 1.4426950408889634
Q_SCALE = ATTN_SCALE * LOG2E
NEG_BIG = -1e30

OFF_A = 0
OFF_BQ = OFF_A + 3 * W_BR
OFF_BK = OFF_BQ + Q_W
OFF_BV = OFF_BK + KV_W
OFF_C = OFF_BV + KV_W
OFF_DQ = OFF_C + 2 * W_BR
OFF_DK = OFF_DQ + Q_W
OFF_DV = OFF_DK + KV_W
OFF_G = OFF_DV + KV_W

PF_W = 3 * W_BR + 2 * W_BR
PQ_W = 2 * (Q_W + 2 * KV_W)
TN = 512
N_PF_TILES = PF_W // TN
N_PROJ_TILES = (PF_W + PQ_W) // TN

VMEM_LIMIT = 56 * 1024 * 1024


def _cparams(sem):
    return pltpu.CompilerParams(dimension_semantics=sem, vmem_limit_bytes=VMEM_LIMIT)


def _sigmoid(z):
    return 1.0 / (1.0 + jnp.exp(-z))


def _ada_kernel(c_ref, w_ref, b_ref, o_ref):
    c = c_ref[...]
    s = (c * _sigmoid(c)).astype(BF16)
    o_ref[...] = jnp.dot(s, w_ref[...].astype(BF16), preferred_element_type=F32) + b_ref[...]


ADA_ROWS = 16


def _ada(cvec, w, b):
    n = w.shape[1]
    tn = 1024
    return pl.pallas_call(
        _ada_kernel,
        out_shape=jax.ShapeDtypeStruct((ADA_ROWS, n), F32),
        grid=(n // tn,),
        in_specs=[pl.BlockSpec((ADA_ROWS, D_MODEL), lambda j: (0, 0)),
                  pl.BlockSpec((D_MODEL, tn), lambda j: (0, j)),
                  pl.BlockSpec((1, tn), lambda j: (0, j))],
        out_specs=pl.BlockSpec((ADA_ROWS, tn), lambda j: (0, j)),
        compiler_params=_cparams(("arbitrary",)),
        name="ada",
    )(cvec, w, b)


def _rope(xh, cos, sn, sp):
    return xh * cos + pltpu.roll(xh, HEAD_DIM - ROT_AXIS // 2, 1) * sn + pltpu.roll(xh, ROT_AXIS // 2, 1) * sp


def _head_norm(xh, g):
    return xh * lax.rsqrt(jnp.mean(xh * xh, axis=-1, keepdims=True) + EPS) * g


def _inproj_kernel(x_ref, g_ref, sc_ref, sh_ref, w_ref, cos_ref, sn_ref, sp_ref, gq_ref, gk_ref,
                   pf_ref, pq_ref, h_ref, vt_ref):
    j = pl.program_id(1)

    @pl.when(j == 0)
    def _():
        x = x_ref[...]
        y = x * lax.rsqrt(jnp.mean(x * x, axis=-1, keepdims=True) + EPS) * g_ref[...]
        h_ref[...] = (y * (1.0 + sc_ref[0]) + sh_ref[0]).astype(BF16)

    acc = jnp.dot(h_ref[...], w_ref[...], preferred_element_type=F32)

    @pl.when(j < N_PF_TILES)
    def _():
        pf_ref[...] = acc

    def rope(xh):
        return _rope(xh, cos_ref[...], sn_ref[...], sp_ref[...])

    def head(k):
        return acc[:, k * HEAD_DIM:(k + 1) * HEAD_DIM]

    def put(k, v):
        pq_ref[:, k * HEAD_DIM:(k + 1) * HEAD_DIM] = v.astype(BF16)

    @pl.when(j == N_PF_TILES)
    def _():
        for k in range(N_Q):
            put(k, rope(head(k)) * Q_SCALE)

    @pl.when(j == N_PF_TILES + 1)
    def _():
        for k in range(N_KV):
            put(k, rope(head(k)))
        pq_ref[:, KV_W:] = acc[:, KV_W:].astype(BF16)
        vt_ref[0:KV_W, :] = acc[:, KV_W:].T.astype(BF16)

    @pl.when(j == N_PF_TILES + 2)
    def _():
        for k in range(N_Q):
            put(k, rope(_head_norm(head(k), gq_ref[...])) * Q_SCALE)

    @pl.when(j == N_PF_TILES + 3)
    def _():
        for k in range(N_KV):
            put(k, rope(_head_norm(head(k), gk_ref[...])))
        pq_ref[:, KV_W:] = acc[:, KV_W:].astype(BF16)
        vt_ref[KV_W:, :] = acc[:, KV_W:].T.astype(BF16)


def _proj_block(j):
    n_a, n_b, n_c = 3 * W_BR // TN, (Q_W + 2 * KV_W) // TN, 2 * W_BR // TN
    return jnp.where(j < n_a, j, jnp.where(j < n_a + n_c, j + n_b, jnp.where(j < n_a + n_c + n_b, j - n_c, j)))


def _inproj(x, g, sc, sh, w, cos, sn, sp, gq, gk, *, seq, tm):
    t = x.shape[0]
    nbt = seq // tm
    return pl.pallas_call(
        _inproj_kernel,
        out_shape=(jax.ShapeDtypeStruct((t, PF_W), F32),
                   jax.ShapeDtypeStruct((t, PQ_W), BF16),
                   jax.ShapeDtypeStruct((t, D_MODEL), BF16),
                   jax.ShapeDtypeStruct((2 * KV_W, t), BF16)),
        grid=(t // tm, N_PROJ_TILES),
        in_specs=[pl.BlockSpec((tm, D_MODEL), lambda i, j: (i, 0)),
                  pl.BlockSpec((1, D_MODEL), lambda i, j: (0, 0)),
                  pl.BlockSpec((1, 1, D_MODEL), lambda i, j: (i // nbt, 0, 0)),
                  pl.BlockSpec((1, 1, D_MODEL), lambda i, j: (i // nbt, 0, 0)),
                  pl.BlockSpec((D_MODEL, TN), lambda i, j: (0, _proj_block(j))),
                  pl.BlockSpec((tm, HEAD_DIM), lambda i, j: (i % nbt, 0)),
                  pl.BlockSpec((tm, HEAD_DIM), lambda i, j: (i % nbt, 0)),
                  pl.BlockSpec((tm, HEAD_DIM), lambda i, j: (i % nbt, 0)),
                  pl.BlockSpec((1, HEAD_DIM), lambda i, j: (0, 0)),
                  pl.BlockSpec((1, HEAD_DIM), lambda i, j: (0, 0))],
        out_specs=(pl.BlockSpec((tm, TN), lambda i, j: (i, jnp.minimum(j, N_PF_TILES - 1))),
                   pl.BlockSpec((tm, TN), lambda i, j: (i, jnp.clip(j - N_PF_TILES, 0, PQ_W // TN - 1))),
                   pl.BlockSpec((tm, D_MODEL), lambda i, j: (i, 0)),
                   pl.BlockSpec((2 * KV_W, tm), lambda i, j: (0, i))),
        compiler_params=_cparams(("parallel", "arbitrary")),
        name="inproj",
    )(x, g, sc, sh, w, cos, sn, sp, gq, gk)


HALO_A = 8


def _conv_a_kernel(bg_ref, cg_ref, h_ref, cgp_ref, hp_ref, cgn_ref, hn_ref, w_ref, o_ref, u_ref, *, nbt):
    it = pl.program_id(0) % nbt
    tm = bg_ref.shape[0]
    u_ref[HALO_A:HALO_A + tm, :] = cg_ref[...] * h_ref[...]
    u_ref[0:HALO_A, :] = jnp.where(it > 0, cgp_ref[...] * hp_ref[...], 0.0)
    u_ref[HALO_A + tm:, :] = jnp.where(it < nbt - 1, cgn_ref[...] * hn_ref[...], 0.0)
    y = (w_ref[0:1, :] * u_ref[HALO_A - 1:HALO_A - 1 + tm, :]
         + w_ref[1:2, :] * u_ref[HALO_A:HALO_A + tm, :]
         + w_ref[2:3, :] * u_ref[HALO_A + 1:HALO_A + 1 + tm, :])
    o_ref[...] = (bg_ref[...] * y).astype(BF16)


def _conv_a(pf, w, *, seq, tm):
    t = pf.shape[0]
    nbt = seq // tm
    r = tm // HALO_A
    last = t // HALO_A - 1

    def cur(c):
        return pl.BlockSpec((tm, W_BR), lambda i: (i, c))

    def prev(c):
        return pl.BlockSpec((HALO_A, W_BR), lambda i: (jnp.maximum(i * r - 1, 0), c))

    def nxt(c):
        return pl.BlockSpec((HALO_A, W_BR), lambda i: (jnp.minimum((i + 1) * r, last), c))

    return pl.pallas_call(
        functools.partial(_conv_a_kernel, nbt=nbt),
        out_shape=jax.ShapeDtypeStruct((t, W_BR), BF16),
        grid=(t // tm,),
        in_specs=[cur(0), cur(1), cur(2), prev(1), prev(2), nxt(1), nxt(2),
                  pl.BlockSpec((CONV_A, W_BR), lambda i: (0, 0))],
        out_specs=pl.BlockSpec((tm, W_BR), lambda i: (i, 0)),
        scratch_shapes=[pltpu.VMEM((tm + 2 * HALO_A, W_BR), F32)],
        compiler_params=_cparams(("parallel",)),
        name="conv_a",
    )(pf, pf, pf, pf, pf, pf, pf, w)


HALO_C = 16
ROWS_C = 64
SUBLANES = 8


def _conv_c_kernel(v_ref, g_ref, vp_ref, gp_ref, vn_ref, gn_ref, w_ref, b_ref, lg_ref, lb_ref,
                   o_ref, u_ref, us_ref, *, nbt):
    it = pl.program_id(0) % nbt
    tm = v_ref.shape[0]
    u_ref[HALO_C:HALO_C + tm, :] = v_ref[...] * _sigmoid(g_ref[...])
    u_ref[0:HALO_C, :] = jnp.where(it > 0, vp_ref[...] * _sigmoid(gp_ref[...]), 0.0)
    u_ref[HALO_C + tm:, :] = jnp.where(it < nbt - 1, vn_ref[...] * _sigmoid(gn_ref[...]), 0.0)
    n_sh = us_ref.shape[1]
    for r in range(1, SUBLANES):
        us_ref[r - 1] = u_ref[r:r + n_sh, :]
    pad = CONV_C // 2

    def tap(row):
        a, r = divmod(row, SUBLANES)
        if r == 0:
            return u_ref[a * SUBLANES:a * SUBLANES + ROWS_C, :]
        return us_ref[r - 1, a * SUBLANES:a * SUBLANES + ROWS_C, :]

    for c in range(tm // ROWS_C):
        base = HALO_C + c * ROWS_C - pad
        acc = w_ref[0:1, :] * tap(base)
        for k in range(1, CONV_C):
            acc = acc + w_ref[k:k + 1, :] * tap(base + k)
        acc = acc + b_ref[...]
        mu = jnp.mean(acc, axis=-1, keepdims=True)
        xc = acc - mu
        var = jnp.mean(xc * xc, axis=-1, keepdims=True)
        y = xc * lax.rsqrt(var + EPS) * lg_ref[...] + lb_ref[...]
        o_ref[c * ROWS_C:(c + 1) * ROWS_C, :] = (y * _sigmoid(y)).astype(BF16)


def _conv_c(pf, w, b, lg, lb, *, seq, tm):
    t = pf.shape[0]
    nbt = seq // tm
    r = tm // HALO_C
    last = t // HALO_C - 1
    c0 = 3

    def cur(c):
        return pl.BlockSpec((tm, W_BR), lambda i: (i, c))

    def prev(c):
        return pl.BlockSpec((HALO_C, W_BR), lambda i: (jnp.maximum(i * r - 1, 0), c))

    def nxt(c):
        return pl.BlockSpec((HALO_C, W_BR), lambda i: (jnp.minimum((i + 1) * r, last), c))

    def vec():
        return pl.BlockSpec((1, W_BR), lambda i: (0, 0))

    return pl.pallas_call(
        functools.partial(_conv_c_kernel, nbt=nbt),
        out_shape=jax.ShapeDtypeStruct((t, W_BR), BF16),
        grid=(t // tm,),
        in_specs=[cur(c0), cur(c0 + 1), prev(c0), prev(c0 + 1), nxt(c0), nxt(c0 + 1),
                  pl.BlockSpec((CONV_C, W_BR), lambda i: (0, 0)), vec(), vec(), vec()],
        out_specs=pl.BlockSpec((tm, W_BR), lambda i: (i, 0)),
        scratch_shapes=[pltpu.VMEM((tm + 2 * HALO_C, W_BR), F32),
                        pltpu.VMEM((SUBLANES - 1, tm + 2 * HALO_C - SUBLANES, W_BR), F32)],
        compiler_params=_cparams(("parallel",)),
        name="conv_c",
    )(pf, pf, pf, pf, pf, pf, w, b, lg, lb)


def _dot_nt(a, b):
    return lax.dot_general(a, b, (((1,), (1,)), ((), ())), preferred_element_type=F32)


def _win_kernel(sink_ref, q_ref, kp_ref, kc_ref, kn_ref, kx_ref, vp_ref, vc_ref, vn_ref, vx_ref, o_ref, *, nq):
    hk = pl.program_id(1)
    i = pl.program_id(2)
    tq = q_ref.shape[0]
    n_loc = tq + 2 * WINDOW
    kk = jnp.concatenate([kp_ref[...], kc_ref[...], kn_ref[...], kx_ref[...]], axis=0)
    vv = jnp.concatenate([vp_ref[...], vc_ref[...], vn_ref[...], vx_ref[...]], axis=0)
    n_all = kk.shape[0]
    row = lax.broadcasted_iota(jnp.int32, (tq, n_all), 0)
    col = lax.broadcasted_iota(jnp.int32, (tq, n_all), 1)
    rel = col - WINDOW - row
    lo = jnp.where(i > 0, 0, WINDOW)
    hi = jnp.where(i < nq - 1, n_loc, tq + WINDOW)
    valid = (col >= n_loc) | ((jnp.abs(rel) <= WINDOW) & (col >= lo) & (col < hi))
    for g in range(GROUP):
        q = q_ref[:, g * HEAD_DIM:(g + 1) * HEAD_DIM]
        s = jnp.where(valid, _dot_nt(q, kk), NEG_BIG)
        sink = sink_ref[hk * GROUP + g] * LOG2E
        m = jnp.maximum(jnp.max(s, axis=-1, keepdims=True), sink)
        p = jnp.exp2(s - m)
        l = jnp.sum(p, axis=-1, keepdims=True) + jnp.exp2(sink - m)
        o = jnp.dot(p.astype(BF16), vv, preferred_element_type=F32) / l
        o_ref[:, g * HEAD_DIM:(g + 1) * HEAD_DIM] = o.astype(BF16)


def _win_attn(pq, pq_c, sink, *, batch, seq, ctx, tq):
    nq = seq // tq
    r = tq // WINDOW
    nblk = seq // WINDOW
    kcol = Q_W // HEAD_DIM
    vcol = (Q_W + KV_W) // HEAD_DIM

    def cur(c0):
        return pl.BlockSpec((tq, HEAD_DIM), lambda b, h, i: (b * nq + i, c0 + h))

    def prev(c0):
        return pl.BlockSpec((WINDOW, HEAD_DIM), lambda b, h, i: (b * nblk + jnp.maximum(i * r - 1, 0), c0 + h))

    def nxt(c0):
        return pl.BlockSpec((WINDOW, HEAD_DIM),
                            lambda b, h, i: (b * nblk + jnp.minimum((i + 1) * r, nblk - 1), c0 + h))

    def cx(c0):
        return pl.BlockSpec((ctx, HEAD_DIM), lambda b, h, i: (b, c0 + h))

    return pl.pallas_call(
        functools.partial(_win_kernel, nq=nq),
        out_shape=jax.ShapeDtypeStruct((batch * seq, Q_W), BF16),
        grid=(batch, N_KV, nq),
        in_specs=[pl.BlockSpec(memory_space=pltpu.SMEM),
                  pl.BlockSpec((tq, GROUP * HEAD_DIM), lambda b, h, i: (b * nq + i, h)),
                  prev(kcol), cur(kcol), nxt(kcol), cx(kcol),
                  prev(vcol), cur(vcol), nxt(vcol), cx(vcol)],
        out_specs=pl.BlockSpec((tq, GROUP * HEAD_DIM), lambda b, h, i: (b * nq + i, h)),
        compiler_params=_cparams(("parallel", "parallel", "parallel")),
        name="win_attn",
    )(sink, pq, pq, pq, pq, pq_c, pq, pq, pq, pq_c)


ONES_ROWS = 16
Q_CHUNK = 512
K_SUB = 1024
QK_AHEAD = 2


def _flash_kernel(sink_ref, q_ref, k_ref, vt_ref, *rest, n_main, has_extra, use_sink):
    if has_extra:
        kx_ref, vtx_ref, o_ref, q2_ref, m_ref, acc_ref, vext_ref, vextx_ref = rest
    else:
        o_ref, q2_ref, m_ref, acc_ref, vext_ref = rest
    hk = pl.program_id(1)
    j = pl.program_id(3)
    tq = q_ref.shape[0]

    @pl.when(j == 0)
    def _():
        for g in range(GROUP):
            q2_ref[g * tq:(g + 1) * tq, :] = q_ref[:, g * HEAD_DIM:(g + 1) * HEAD_DIM]
            if use_sink:
                m_ref[:, g * tq:(g + 1) * tq] = jnp.full((1, tq), sink_ref[hk * GROUP + g] * LOG2E, F32)
        if not use_sink:
            m_ref[...] = jnp.full(m_ref.shape, NEG_BIG, F32)
        acc_ref[0:HEAD_DIM, :] = jnp.zeros((HEAD_DIM, GROUP * tq), F32)
        acc_ref[HEAD_DIM:, :] = jnp.full((ONES_ROWS, GROUP * tq), 1.0 if use_sink else 0.0, F32)
        vext_ref[HEAD_DIM:, :] = jnp.ones((ONES_ROWS, vext_ref.shape[1]), BF16)
        if has_extra:
            vextx_ref[HEAD_DIM:, :] = jnp.ones((ONES_ROWS, vextx_ref.shape[1]), BF16)

    n_chunks = GROUP * tq // Q_CHUNK

    def step(kk_ref, vv_ref, vext):
        tkk = kk_ref.shape[0]
        kb = min(tkk, K_SUB)
        units = [(b, c) for b in range(tkk // kb) for c in range(n_chunks)]
        vext[0:HEAD_DIM, :] = vv_ref[...]

        def scores(u):
            b, c = u
            return _dot_nt(kk_ref[b * kb:(b + 1) * kb, :], q2_ref[c * Q_CHUNK:(c + 1) * Q_CHUNK, :])

        m = [m_ref[:, c * Q_CHUNK:(c + 1) * Q_CHUNK] for c in range(n_chunks)]
        acc = [acc_ref[:, c * Q_CHUNK:(c + 1) * Q_CHUNK] for c in range(n_chunks)]
        ahead = [scores(u) for u in units[:QK_AHEAD]]
        for idx, (b, c) in enumerate(units):
            s = ahead.pop(0)
            if idx + QK_AHEAD < len(units):
                ahead.append(scores(units[idx + QK_AHEAD]))
            m_next = jnp.maximum(m[c], jnp.max(s, axis=0, keepdims=True))
            alpha = jnp.exp2(m[c] - m_next)
            p = jnp.exp2(s - m_next).astype(BF16)
            acc[c] = alpha * acc[c] + jnp.dot(vext[:, b * kb:(b + 1) * kb], p, preferred_element_type=F32)
            m[c] = m_next
        acc_ref[...] = jnp.concatenate(acc, axis=1)
        m_ref[...] = jnp.concatenate(m, axis=1)

    @pl.when(j < n_main)
    def _():
        step(k_ref, vt_ref, vext_ref)

    if has_extra:
        @pl.when(j == n_main)
        def _():
            step(kx_ref, vtx_ref, vextx_ref)

    @pl.when(j == n_main + (1 if has_extra else 0) - 1)
    def _():
        o = (acc_ref[0:HEAD_DIM, :] / acc_ref[HEAD_DIM:HEAD_DIM + 1, :]).T
        for g in range(GROUP):
            o_ref[:, g * HEAD_DIM:(g + 1) * HEAD_DIM] = o[g * tq:(g + 1) * tq, :].astype(BF16)


def _flash(q_arr, k_arr, vt_arr, kx_arr, vtx_arr, sink, *, batch, seq_q, seq_k, seq_x, tq, tk, q_off, k_off,
           vt_row, use_sink):
    nq = seq_q // tq
    n_main = seq_k // tk
    has_extra = kx_arr is not None
    n_steps = n_main + (1 if has_extra else 0)
    qc = q_off // (GROUP * HEAD_DIM)
    kc = k_off // HEAD_DIM

    def jj(j):
        return jnp.minimum(j, n_main - 1)

    in_specs = [pl.BlockSpec(memory_space=pltpu.SMEM),
                pl.BlockSpec((tq, GROUP * HEAD_DIM), lambda b, h, i, j: (b * nq + i, qc + h)),
                pl.BlockSpec((tk, HEAD_DIM), lambda b, h, i, j: (b * n_main + jj(j), kc + h)),
                pl.BlockSpec((HEAD_DIM, tk), lambda b, h, i, j: (vt_row + h, b * n_main + jj(j)))]
    args = [sink, q_arr, k_arr, vt_arr]
    rows = GROUP * tq
    scratch = [pltpu.VMEM((rows, HEAD_DIM), BF16),
               pltpu.VMEM((1, rows), F32),
               pltpu.VMEM((HEAD_DIM + ONES_ROWS, rows), F32),
               pltpu.VMEM((HEAD_DIM + ONES_ROWS, tk), BF16)]
    if has_extra:
        in_specs += [pl.BlockSpec((seq_x, HEAD_DIM), lambda b, h, i, j: (b, kc + h)),
                     pl.BlockSpec((HEAD_DIM, seq_x), lambda b, h, i, j: (vt_row + h, b))]
        args += [kx_arr, vtx_arr]
        scratch += [pltpu.VMEM((HEAD_DIM + ONES_ROWS, seq_x), BF16)]
    return pl.pallas_call(
        functools.partial(_flash_kernel, n_main=n_main, has_extra=has_extra, use_sink=use_sink),
        out_shape=jax.ShapeDtypeStruct((batch * seq_q, Q_W), BF16),
        grid=(batch, N_KV, nq, n_steps),
        in_specs=in_specs,
        out_specs=pl.BlockSpec((tq, GROUP * HEAD_DIM), lambda b, h, i, j: (b * nq + i, h)),
        scratch_shapes=scratch,
        compiler_params=_cparams(("parallel", "parallel", "parallel", "arbitrary")),
        name="flash",
    )(*args)


def _merge_kernel(h_ref, ya_ref, yb_ref, yc_ref, yd_ref, wga_ref, wgb_ref, wgc_ref, wgd_ref,
                  wb_ref, bg_ref, o_ref):
    h = h_ref[...]
    m = None
    for k, (y_ref, wg_ref) in enumerate(((ya_ref, wga_ref), (yb_ref, wgb_ref), (yc_ref, wgc_ref), (yd_ref, wgd_ref))):
        logits = jnp.dot(h, wg_ref[...], preferred_element_type=F32) + bg_ref[k:k + 1, :]
        t = _sigmoid(logits) * jnp.dot(y_ref[...], wb_ref[k], preferred_element_type=F32)
        m = t if m is None else m + t
    o_ref[...] = m.astype(BF16)


def _merge(h, ys, wg, wb, bg, *, tm, tn):
    t = h.shape[0]
    nct = D_MODEL // tn

    def y():
        return pl.BlockSpec((tm, W_BR), lambda i, j: (i, 0))

    def g(k):
        return pl.BlockSpec((D_MODEL, tn), lambda i, j: (0, OFF_G // tn + k * nct + j))

    return pl.pallas_call(
        _merge_kernel,
        out_shape=jax.ShapeDtypeStruct((t, D_MODEL), BF16),
        grid=(t // tm, nct),
        in_specs=[pl.BlockSpec((tm, D_MODEL), lambda i, j: (i, 0)), y(), y(), y(), y(),
                  g(0), g(1), g(2), g(3),
                  pl.BlockSpec((N_BRANCH, W_BR, tn), lambda i, j: (0, 0, j)),
                  pl.BlockSpec((N_BRANCH, tn), lambda i, j: (0, j))],
        out_specs=pl.BlockSpec((tm, tn), lambda i, j: (i, j)),
        compiler_params=_cparams(("parallel", "arbitrary")),
        name="merge",
    )(h, *ys, wg, wg, wg, wg, wb, bg)


def _resid_kernel(a_ref, w_ref, x_ref, gate_ref, o_ref):
    o_ref[...] = x_ref[...] + gate_ref[0] * jnp.dot(a_ref[...], w_ref[...], preferred_element_type=F32)


def _resid(a, w, x, gate, *, seq, tm, tn):
    t, k = a.shape
    nbt = seq // tm
    return pl.pallas_call(
        _resid_kernel,
        out_shape=jax.ShapeDtypeStruct((t, D_MODEL), F32),
        grid=(t // tm, D_MODEL // tn),
        in_specs=[pl.BlockSpec((tm, k), lambda i, j: (i, 0)),
                  pl.BlockSpec((k, tn), lambda i, j: (0, j)),
                  pl.BlockSpec((tm, tn), lambda i, j: (i, j)),
                  pl.BlockSpec((1, 1, tn), lambda i, j: (i // nbt, 0, j))],
        out_specs=pl.BlockSpec((tm, tn), lambda i, j: (i, j)),
        compiler_params=_cparams(("parallel", "arbitrary")),
        name="resid",
    )(a, w, x, gate)


def _ffn_in_kernel(x_ref, g_ref, sc_ref, sh_ref, wa_ref, wb_ref, o_ref, h_ref):
    @pl.when(pl.program_id(1) == 0)
    def _():
        x = x_ref[...]
        y = x * lax.rsqrt(jnp.mean(x * x, axis=-1, keepdims=True) + EPS) * g_ref[...]
        h_ref[...] = (y * (1.0 + sc_ref[0]) + sh_ref[0]).astype(BF16)

    h = h_ref[...]
    a = jnp.dot(h, wa_ref[...], preferred_element_type=F32)
    b = jnp.dot(h, wb_ref[...], preferred_element_type=F32)
    o_ref[...] = (a * _sigmoid(a) * b).astype(BF16)


def _ffn_in(x, g, sc, sh, w, *, seq, tm, tn):
    t = x.shape[0]
    nbt = seq // tm
    nct = FFN_HIDDEN // tn
    return pl.pallas_call(
        _ffn_in_kernel,
        out_shape=jax.ShapeDtypeStruct((t, FFN_HIDDEN), BF16),
        grid=(t // tm, nct),
        in_specs=[pl.BlockSpec((tm, D_MODEL), lambda i, j: (i, 0)),
                  pl.BlockSpec((1, D_MODEL), lambda i, j: (0, 0)),
                  pl.BlockSpec((1, 1, D_MODEL), lambda i, j: (i // nbt, 0, 0)),
                  pl.BlockSpec((1, 1, D_MODEL), lambda i, j: (i // nbt, 0, 0)),
                  pl.BlockSpec((D_MODEL, tn), lambda i, j: (0, j)),
                  pl.BlockSpec((D_MODEL, tn), lambda i, j: (0, nct + j))],
        out_specs=pl.BlockSpec((tm, tn), lambda i, j: (i, j)),
        scratch_shapes=[pltpu.VMEM((tm, D_MODEL), BF16)],
        compiler_params=_cparams(("parallel", "arbitrary")),
        name="ffn_in",
    )(x, g, sc, sh, w, w)


def _final_kernel(x_ref, g_ref, o_ref):
    x = x_ref[...]
    o_ref[...] = x * lax.rsqrt(jnp.mean(x * x, axis=-1, keepdims=True) + EPS) * g_ref[...]


def _final_norm(x, g, *, tm):
    t = x.shape[0]
    return pl.pallas_call(
        _final_kernel,
        out_shape=jax.ShapeDtypeStruct((t, D_MODEL), F32),
        grid=(t // tm,),
        in_specs=[pl.BlockSpec((tm, D_MODEL), lambda i: (i, 0)),
                  pl.BlockSpec((1, D_MODEL), lambda i: (0, 0))],
        out_specs=pl.BlockSpec((tm, D_MODEL), lambda i: (i, 0)),
        compiler_params=_cparams(("parallel",)),
        name="final_norm",
    )(x, g)


def _rope_tables(seq):
    pos = jnp.arange(seq)
    row = (pos // GRID_W).astype(F32)
    col = (pos % GRID_W).astype(F32)
    inv = ROPE_THETA ** (-jnp.arange(0, ROT_AXIS, 2, dtype=F32) / ROT_AXIS)
    ar = row[:, None] * inv
    ac = col[:, None] * inv
    ang = jnp.concatenate([ar, ar, ac, ac], axis=-1)
    cos, sin = jnp.cos(ang), jnp.sin(ang)
    first = (jnp.arange(HEAD_DIM) % ROT_AXIS) < ROT_AXIS // 2
    return cos, jnp.where(first, -sin, 0.0), jnp.where(first, 0.0, sin)


def _tile(n, pref):
    return pref if n % pref == 0 else n


def _forward(x, c, ctx, c_ctx, w_ada, b_ada, norm_mix, norm_ffn, w_in, b_gate, conv_a_w, sink_b,
             qk_norm_q, qk_norm_k, conv_c_w, conv_c_b, ln_c_g, ln_c_b, w_branch, w_out,
             w_ffn_in, w_ffn_out, norm_final):
    batch, seq, _ = x.shape
    n_ctx = ctx.shape[1]
    depth = w_in.shape[0]
    xs = x.reshape(batch * seq, D_MODEL)
    xc = ctx.reshape(batch * n_ctx, D_MODEL)

    tm = _tile(seq, 1024)
    tm_conv_a = _tile(seq, 512)
    tm_conv_c = _tile(seq, 256)
    tq_win = _tile(seq, 512)
    tq = _tile(seq, 1024)
    tk = _tile(seq, 2048)

    rope_x = _rope_tables(seq)
    rope_c = (jnp.ones((n_ctx, HEAD_DIM), F32), jnp.zeros((n_ctx, HEAD_DIM), F32),
              jnp.zeros((n_ctx, HEAD_DIM), F32))
    cvec = jnp.zeros((ADA_ROWS, D_MODEL), F32).at[:batch].set(c).at[batch].set(c_ctx)

    def row(v):
        return v.reshape(1, -1)

    def mixers(pf, pq, vt, pq_c, vt_c, l, *, s, tma, tmc, is_ctx):
        y_a = _conv_a(pf, conv_a_w[l], seq=s, tm=tma)
        y_c = _conv_c(pf, conv_c_w[l], row(conv_c_b[l]), row(ln_c_g[l]), row(ln_c_b[l]), seq=s, tm=tmc)
        off_d = Q_W + 2 * KV_W
        if is_ctx:
            y_b = _flash(pq, pq, vt, None, None, sink_b[l], batch=batch, seq_q=s, seq_k=s, seq_x=0, tq=s, tk=s,
                         q_off=0, k_off=Q_W, vt_row=0, use_sink=True)
            y_d = _flash(pq, pq, vt, None, None, sink_b[l], batch=batch, seq_q=s, seq_k=s, seq_x=0, tq=s, tk=s,
                         q_off=off_d, k_off=off_d + Q_W, vt_row=N_KV, use_sink=False)
        else:
            y_b = _win_attn(pq, pq_c, sink_b[l], batch=batch, seq=s, ctx=n_ctx, tq=tq_win)
            y_d = _flash(pq, pq, vt, pq_c, vt_c, sink_b[l], batch=batch, seq_q=s, seq_k=s, seq_x=n_ctx,
                         tq=tq, tk=tk, q_off=off_d, k_off=off_d + Q_W, vt_row=N_KV, use_sink=False)
        return [y_a, y_b, y_c, y_d]

    for l in range(depth):
        last = l == depth - 1
        w_proj = w_gate = w_in[l].astype(BF16)
        w_br = w_branch[l].astype(BF16)
        w_o = w_out[l].astype(BF16)
        w_f1 = w_ffn_in[l].astype(BF16)
        w_f2 = w_ffn_out[l].astype(BF16)

        mod = _ada(cvec, w_ada[l], row(b_ada[l]))
        mx = mod[:batch].reshape(batch, 1, 6, D_MODEL)
        mcx = jnp.broadcast_to(mod[batch].reshape(1, 1, 6, D_MODEL), (batch, 1, 6, D_MODEL))
        sh_m, sc_m, g_m, sh_f, sc_f, g_f = (mx[:, :, k] for k in range(6))
        csh_m, csc_m, cg_m, csh_f, csc_f, cg_f = (mcx[:, :, k] for k in range(6))
        gq, gk = row(qk_norm_q[l]), row(qk_norm_k[l])

        pf_c, pq_c, h_c, vt_c = _inproj(xc, row(norm_mix[l]), csc_m, csh_m, w_proj, *rope_c, gq, gk,
                                  seq=n_ctx, tm=n_ctx)
        pf, pq, h, vt = _inproj(xs, row(norm_mix[l]), sc_m, sh_m, w_proj, *rope_x, gq, gk, seq=seq, tm=tm)

        ys = mixers(pf, pq, vt, pq_c, vt_c, l, s=seq, tma=tm_conv_a, tmc=tm_conv_c, is_ctx=False)
        m = _merge(h, ys, w_gate, w_br, b_gate[l], tm=tm, tn=512)
        xs = _resid(m, w_o, xs, g_m, seq=seq, tm=tm, tn=512)

        if not last:
            ys_c = mixers(pf_c, pq_c, vt_c, None, None, l, s=n_ctx, tma=n_ctx, tmc=n_ctx, is_ctx=True)
            m_c = _merge(h_c, ys_c, w_gate, w_br, b_gate[l], tm=n_ctx, tn=512)
            xc = _resid(m_c, w_o, xc, cg_m, seq=n_ctx, tm=n_ctx, tn=512)
            hid_c = _ffn_in(xc, row(norm_ffn[l]), csc_f, csh_f, w_f1, seq=n_ctx, tm=n_ctx, tn=512)
            xc = _resid(hid_c, w_f2, xc, cg_f, seq=n_ctx, tm=n_ctx, tn=256)

        hid = _ffn_in(xs, row(norm_ffn[l]), sc_f, sh_f, w_f1, seq=seq, tm=tm, tn=512)
        xs = _resid(hid, w_f2, xs, g_f, seq=seq, tm=tm, tn=256)

    out = _final_norm(xs, row(norm_final), tm=_tile(batch * seq, 512))
    return out.reshape(batch, seq, D_MODEL)


def kernel(x, c, ctx, c_ctx, w_ada, b_ada, norm_mix, norm_ffn, w_in, b_gate, conv_a_w, sink_b, qk_norm_q,
           qk_norm_k, conv_c_w, conv_c_b, ln_c_g, ln_c_b, w_branch, w_out, w_ffn_in, w_ffn_out, norm_final):
    return _forward(x, c, ctx, c_ctx, w_ada, b_ada, norm_mix, norm_ffn, w_in, b_gate, conv_a_w, sink_b,
                    qk_norm_q, qk_norm_k, conv_c_w, conv_c_b, ln_c_g, ln_c_b, w_branch, w_out,
                    w_ffn_in, w_ffn_out, norm_final)
```

```python
import functools

import jax
import jax.numpy as jnp
from jax import lax
from jax.experimental import pallas as pl
from jax.experimental.pallas import tpu as pltpu

F32 = jnp.float32
BF16 = jnp.bfloat16

D_MODEL = 2048
N_BRANCH = 4
W_BR = D_MODEL // 4
HEAD_DIM = 128
N_Q = W_BR // HEAD_DIM
N_KV = N_Q // 2
GROUP = N_Q // N_KV
Q_W = N_Q * HEAD_DIM
KV_W = N_KV * HEAD_DIM
CONV_A = 3
CONV_C = 31
WINDOW = 128
GRID_W = 64
ROT_AXIS = HEAD_DIM // 2
ROPE_THETA = 10000.0
FFN_HIDDEN = 5632
EPS = 1e-6
ATTN_SCALE = HEAD_DIM ** -0.5
LOG2E = 1.4426950408889634
Q_SCALE = ATTN_SCALE * LOG2E
NEG_BIG = -1e30

OFF_A = 0
OFF_BQ = OFF_A + 3 * W_BR
OFF_BK = OFF_BQ + Q_W
OFF_BV = OFF_BK + KV_W
OFF_C = OFF_BV + KV_W
OFF_DQ = OFF_C + 2 * W_BR
OFF_DK = OFF_DQ + Q_W
OFF_DV = OFF_DK + KV_W
OFF_G = OFF_DV + KV_W

PF_W = 3 * W_BR + 2 * W_BR
PQ_W = 2 * (Q_W + 2 * KV_W)
TN = 512
N_PF_TILES = PF_W // TN
N_PROJ_TILES = (PF_W + PQ_W) // TN

VMEM_LIMIT = 56 * 1024 * 1024


def _cparams(sem):
    return pltpu.CompilerParams(dimension_semantics=sem, vmem_limit_bytes=VMEM_LIMIT)


def _sigmoid(z):
    return 1.0 / (1.0 + jnp.exp(-z))


def _ada_kernel(c_ref, w_ref, b_ref, o_ref):
    c = c_ref[...]
    s = (c * _sigmoid(c)).astype(BF16)
    o_ref[...] = jnp.dot(s, w_ref[...].astype(BF16), preferred_element_type=F32) + b_ref[...]


ADA_ROWS = 16


def _ada(cvec, w, b, *, layer):
    n = w.shape[2]
    tn = 1024
    return pl.pallas_call(
        _ada_kernel,
        out_shape=jax.ShapeDtypeStruct((ADA_ROWS, n), F32),
        grid=(n // tn,),
        in_specs=[pl.BlockSpec((ADA_ROWS, D_MODEL), lambda j: (0, 0)),
                  pl.BlockSpec((None, D_MODEL, tn), lambda j: (layer, 0, j)),
                  pl.BlockSpec((1, tn), lambda j: (0, j))],
        out_specs=pl.BlockSpec((ADA_ROWS, tn), lambda j: (0, j)),
        compiler_params=_cparams(("arbitrary",)),
        name="ada",
    )(cvec, w, b)


def _rope(xh, cos, sn, sp):
    return xh * cos + pltpu.roll(xh, HEAD_DIM - ROT_AXIS // 2, 1) * sn + pltpu.roll(xh, ROT_AXIS // 2, 1) * sp


def _head_norm(xh, g):
    return xh * lax.rsqrt(jnp.mean(xh * xh, axis=-1, keepdims=True) + EPS) * g


NORM_CHUNK = 16
NORM_UNROLL = 8


def _norm_mod_rows(x_ref, g_ref, sc_ref, sh_ref, h_ref):
    gs = g_ref[...] * (1.0 + sc_ref[0])
    sh = sh_ref[0]

    def body(i, carry):
        rows = pl.ds(pl.multiple_of(i * NORM_CHUNK, NORM_CHUNK), NORM_CHUNK)
        x = x_ref[rows, :]
        r = lax.rsqrt(jnp.mean(x * x, axis=-1, keepdims=True) + EPS)
        h_ref[rows, :] = (x * r * gs + sh).astype(BF16)
        return carry

    lax.fori_loop(0, x_ref.shape[0] // NORM_CHUNK, body, 0, unroll=NORM_UNROLL)


def _inproj_kernel(x_ref, g_ref, sc_ref, sh_ref, w_ref, cos_ref, sn_ref, sp_ref, gq_ref, gk_ref,
                   pf_ref, pq_ref, h_ref, vt_ref):
    j = pl.program_id(1)

    @pl.when(j == 0)
    def _():
        _norm_mod_rows(x_ref, g_ref, sc_ref, sh_ref, h_ref)

    acc = jnp.dot(h_ref[...], w_ref[...], preferred_element_type=F32)

    @pl.when(j < N_PF_TILES)
    def _():
        pf_ref[...] = acc

    def rope(xh):
        return _rope(xh, cos_ref[...], sn_ref[...], sp_ref[...])

    def head(k):
        return acc[:, k * HEAD_DIM:(k + 1) * HEAD_DIM]

    def put(k, v):
        pq_ref[:, k * HEAD_DIM:(k + 1) * HEAD_DIM] = v.astype(BF16)

    @pl.when(j == N_PF_TILES)
    def _():
        for k in range(N_Q):
            put(k, rope(head(k)) * Q_SCALE)

    @pl.when(j == N_PF_TILES + 1)
    def _():
        for k in range(N_KV):
            put(k, rope(head(k)))
        pq_ref[:, KV_W:] = acc[:, KV_W:].astype(BF16)
        vt_ref[0:KV_W, :] = acc[:, KV_W:].T.astype(BF16)

    @pl.when(j == N_PF_TILES + 2)
    def _():
        for k in range(N_Q):
            put(k, rope(_head_norm(head(k), gq_ref[...])) * Q_SCALE)

    @pl.when(j == N_PF_TILES + 3)
    def _():
        for k in range(N_KV):
            put(k, rope(_head_norm(head(k), gk_ref[...])))
        pq_ref[:, KV_W:] = acc[:, KV_W:].astype(BF16)
        vt_ref[KV_W:, :] = acc[:, KV_W:].T.astype(BF16)


def _proj_block(j):
    n_a, n_b, n_c = 3 * W_BR // TN, (Q_W + 2 * KV_W) // TN, 2 * W_BR // TN
    return jnp.where(j < n_a, j, jnp.where(j < n_a + n_c, j + n_b, jnp.where(j < n_a + n_c + n_b, j - n_c, j)))


def _inproj(x, g, sc, sh, w, cos, sn, sp, gq, gk, *, layer, seq, tm):
    t = x.shape[0]
    nbt = seq // tm
    return pl.pallas_call(
        _inproj_kernel,
        out_shape=(jax.ShapeDtypeStruct((t, PF_W), F32),
                   jax.ShapeDtypeStruct((t, PQ_W), BF16),
                   jax.ShapeDtypeStruct((t, D_MODEL), BF16),
                   jax.ShapeDtypeStruct((2 * KV_W, t), BF16)),
        grid=(t // tm, N_PROJ_TILES),
        in_specs=[pl.BlockSpec((tm, D_MODEL), lambda i, j: (i, 0)),
                  pl.BlockSpec((1, D_MODEL), lambda i, j: (0, 0)),
                  pl.BlockSpec((1, 1, D_MODEL), lambda i, j: (i // nbt, 0, 0)),
                  pl.BlockSpec((1, 1, D_MODEL), lambda i, j: (i // nbt, 0, 0)),
                  pl.BlockSpec((None, D_MODEL, TN), lambda i, j: (layer, 0, _proj_block(j))),
                  pl.BlockSpec((tm, HEAD_DIM), lambda i, j: (i % nbt, 0)),
                  pl.BlockSpec((tm, HEAD_DIM), lambda i, j: (i % nbt, 0)),
                  pl.BlockSpec((tm, HEAD_DIM), lambda i, j: (i % nbt, 0)),
                  pl.BlockSpec((1, HEAD_DIM), lambda i, j: (0, 0)),
                  pl.BlockSpec((1, HEAD_DIM), lambda i, j: (0, 0))],
        out_specs=(pl.BlockSpec((tm, TN), lambda i, j: (i, jnp.minimum(j, N_PF_TILES - 1))),
                   pl.BlockSpec((tm, TN), lambda i, j: (i, jnp.clip(j - N_PF_TILES, 0, PQ_W // TN - 1))),
                   pl.BlockSpec((tm, D_MODEL), lambda i, j: (i, 0)),
                   pl.BlockSpec((2 * KV_W, tm), lambda i, j: (0, i))),
        compiler_params=_cparams(("parallel", "arbitrary")),
        name="inproj",
    )(x, g, sc, sh, w, cos, sn, sp, gq, gk)


HALO_A = 8


def _conv_a_kernel(bg_ref, cg_ref, h_ref, cgp_ref, hp_ref, cgn_ref, hn_ref, w_ref, o_ref, u_ref, *, nbt):
    it = pl.program_id(0) % nbt
    tm = bg_ref.shape[0]
    u_ref[HALO_A:HALO_A + tm, :] = cg_ref[...] * h_ref[...]
    u_ref[0:HALO_A, :] = jnp.where(it > 0, cgp_ref[...] * hp_ref[...], 0.0)
    u_ref[HALO_A + tm:, :] = jnp.where(it < nbt - 1, cgn_ref[...] * hn_ref[...], 0.0)
    y = (w_ref[0:1, :] * u_ref[HALO_A - 1:HALO_A - 1 + tm, :]
         + w_ref[1:2, :] * u_ref[HALO_A:HALO_A + tm, :]
         + w_ref[2:3, :] * u_ref[HALO_A + 1:HALO_A + 1 + tm, :])
    o_ref[...] = (bg_ref[...] * y).astype(BF16)


def _conv_a(pf, w, *, seq, tm):
    t = pf.shape[0]
    nbt = seq // tm
    r = tm // HALO_A
    last = t // HALO_A - 1

    def cur(c):
        return pl.BlockSpec((tm, W_BR), lambda i: (i, c))

    def prev(c):
        return pl.BlockSpec((HALO_A, W_BR), lambda i: (jnp.maximum(i * r - 1, 0), c))

    def nxt(c):
        return pl.BlockSpec((HALO_A, W_BR), lambda i: (jnp.minimum((i + 1) * r, last), c))

    return pl.pallas_call(
        functools.partial(_conv_a_kernel, nbt=nbt),
        out_shape=jax.ShapeDtypeStruct((t, W_BR), BF16),
        grid=(t // tm,),
        in_specs=[cur(0), cur(1), cur(2), prev(1), prev(2), nxt(1), nxt(2),
                  pl.BlockSpec((CONV_A, W_BR), lambda i: (0, 0))],
        out_specs=pl.BlockSpec((tm, W_BR), lambda i: (i, 0)),
        scratch_shapes=[pltpu.VMEM((tm + 2 * HALO_A, W_BR), F32)],
        compiler_params=_cparams(("parallel",)),
        name="conv_a",
    )(pf, pf, pf, pf, pf, pf, pf, w)


HALO_C = 16
ROWS_C = 64
SUBLANES = 8


def _conv_c_kernel(v_ref, g_ref, vp_ref, gp_ref, vn_ref, gn_ref, w_ref, b_ref, lg_ref, lb_ref,
                   o_ref, u_ref, us_ref, *, nbt):
    it = pl.program_id(0) % nbt
    tm = v_ref.shape[0]
    u_ref[HALO_C:HALO_C + tm, :] = v_ref[...] * _sigmoid(g_ref[...])
    u_ref[0:HALO_C, :] = jnp.where(it > 0, vp_ref[...] * _sigmoid(gp_ref[...]), 0.0)
    u_ref[HALO_C + tm:, :] = jnp.where(it < nbt - 1, vn_ref[...] * _sigmoid(gn_ref[...]), 0.0)
    n_sh = us_ref.shape[1]
    for r in range(1, SUBLANES):
        us_ref[r - 1] = u_ref[r:r + n_sh, :]
    pad = CONV_C // 2

    def tap(row):
        a, r = divmod(row, SUBLANES)
        if r == 0:
            return u_ref[a * SUBLANES:a * SUBLANES + ROWS_C, :]
        return us_ref[r - 1, a * SUBLANES:a * SUBLANES + ROWS_C, :]

    for c in range(tm // ROWS_C):
        base = HALO_C + c * ROWS_C - pad
        acc = w_ref[0:1, :] * tap(base)
        for k in range(1, CONV_C):
            acc = acc + w_ref[k:k + 1, :] * tap(base + k)
        acc = acc + b_ref[...]
        mu = jnp.mean(acc, axis=-1, keepdims=True)
        xc = acc - mu
        var = jnp.mean(xc * xc, axis=-1, keepdims=True)
        y = xc * lax.rsqrt(var + EPS) * lg_ref[...] + lb_ref[...]
        o_ref[c * ROWS_C:(c + 1) * ROWS_C, :] = (y * _sigmoid(y)).astype(BF16)


def _conv_c(pf, w, b, lg, lb, *, seq, tm):
    t = pf.shape[0]
    nbt = seq // tm
    r = tm // HALO_C
    last = t // HALO_C - 1
    c0 = 3

    def cur(c):
        return pl.BlockSpec((tm, W_BR), lambda i: (i, c))

    def prev(c):
        return pl.BlockSpec((HALO_C, W_BR), lambda i: (jnp.maximum(i * r - 1, 0), c))

    def nxt(c):
        return pl.BlockSpec((HALO_C, W_BR), lambda i: (jnp.minimum((i + 1) * r, last), c))

    def vec():
        return pl.BlockSpec((1, W_BR), lambda i: (0, 0))

    return pl.pallas_call(
        functools.partial(_conv_c_kernel, nbt=nbt),
        out_shape=jax.ShapeDtypeStruct((t, W_BR), BF16),
        grid=(t // tm,),
        in_specs=[cur(c0), cur(c0 + 1), prev(c0), prev(c0 + 1), nxt(c0), nxt(c0 + 1),
                  pl.BlockSpec((CONV_C, W_BR), lambda i: (0, 0)), vec(), vec(), vec()],
        out_specs=pl.BlockSpec((tm, W_BR), lambda i: (i, 0)),
        scratch_shapes=[pltpu.VMEM((tm + 2 * HALO_C, W_BR), F32),
                        pltpu.VMEM((SUBLANES - 1, tm + 2 * HALO_C - SUBLANES, W_BR), F32)],
        compiler_params=_cparams(("parallel",)),
        name="conv_c",
    )(pf, pf, pf, pf, pf, pf, w, b, lg, lb)


def _dot_nt(a, b):
    return lax.dot_general(a, b, (((1,), (1,)), ((), ())), preferred_element_type=F32)


def _win_kernel(sink_ref, q_ref, kp_ref, kc_ref, kn_ref, kx_ref, vp_ref, vc_ref, vn_ref, vx_ref, o_ref, *, nq):
    hk = pl.program_id(1)
    i = pl.program_id(2)
    tq = q_ref.shape[0]
    n_loc = tq + 2 * WINDOW
    kk = jnp.concatenate([kp_ref[...], kc_ref[...], kn_ref[...], kx_ref[...]], axis=0)
    vv = jnp.concatenate([vp_ref[...], vc_ref[...], vn_ref[...], vx_ref[...]], axis=0)
    n_all = kk.shape[0]
    row = lax.broadcasted_iota(jnp.int32, (tq, n_all), 0)
    col = lax.broadcasted_iota(jnp.int32, (tq, n_all), 1)
    rel = col - WINDOW - row
    lo = jnp.where(i > 0, 0, WINDOW)
    hi = jnp.where(i < nq - 1, n_loc, tq + WINDOW)
    valid = (col >= n_loc) | ((jnp.abs(rel) <= WINDOW) & (col >= lo) & (col < hi))
    for g in range(GROUP):
        q = q_ref[:, g * HEAD_DIM:(g + 1) * HEAD_DIM]
        s = jnp.where(valid, _dot_nt(q, kk), NEG_BIG)
        sink = sink_ref[hk * GROUP + g] * LOG2E
        m = jnp.maximum(jnp.max(s, axis=-1, keepdims=True), sink)
        p = jnp.exp2(s - m)
        l = jnp.sum(p, axis=-1, keepdims=True) + jnp.exp2(sink - m)
        o = jnp.dot(p.astype(BF16), vv, preferred_element_type=F32) / l
        o_ref[:, g * HEAD_DIM:(g + 1) * HEAD_DIM] = o.astype(BF16)


def _win_attn(pq, pq_c, sink, *, batch, seq, ctx, tq):
    nq = seq // tq
    r = tq // WINDOW
    nblk = seq // WINDOW
    kcol = Q_W // HEAD_DIM
    vcol = (Q_W + KV_W) // HEAD_DIM

    def cur(c0):
        return pl.BlockSpec((tq, HEAD_DIM), lambda b, h, i: (b * nq + i, c0 + h))

    def prev(c0):
        return pl.BlockSpec((WINDOW, HEAD_DIM), lambda b, h, i: (b * nblk + jnp.maximum(i * r - 1, 0), c0 + h))

    def nxt(c0):
        return pl.BlockSpec((WINDOW, HEAD_DIM),
                            lambda b, h, i: (b * nblk + jnp.minimum((i + 1) * r, nblk - 1), c0 + h))

    def cx(c0):
        return pl.BlockSpec((ctx, HEAD_DIM), lambda b, h, i: (b, c0 + h))

    return pl.pallas_call(
        functools.partial(_win_kernel, nq=nq),
        out_shape=jax.ShapeDtypeStruct((batch * seq, Q_W), BF16),
        grid=(batch, N_KV, nq),
        in_specs=[pl.BlockSpec(memory_space=pltpu.SMEM),
                  pl.BlockSpec((tq, GROUP * HEAD_DIM), lambda b, h, i: (b * nq + i, h)),
                  prev(kcol), cur(kcol), nxt(kcol), cx(kcol),
                  prev(vcol), cur(vcol), nxt(vcol), cx(vcol)],
        out_specs=pl.BlockSpec((tq, GROUP * HEAD_DIM), lambda b, h, i: (b * nq + i, h)),
        compiler_params=_cparams(("parallel", "parallel", "parallel")),
        name="win_attn",
    )(sink, pq, pq, pq, pq, pq_c, pq, pq, pq, pq_c)


ONES_ROWS = 16
Q_CHUNK = 512
K_SUB = 1024
QK_AHEAD = 2


def _flash_kernel(sink_ref, q_ref, k_ref, vt_ref, *rest, n_main, has_extra, use_sink):
    if has_extra:
        kx_ref, vtx_ref, o_ref, q2_ref, m_ref, acc_ref, vext_ref, vextx_ref = rest
    else:
        o_ref, q2_ref, m_ref, acc_ref, vext_ref = rest
    hk = pl.program_id(1)
    j = pl.program_id(3)
    tq = q_ref.shape[0]

    @pl.when(j == 0)
    def _():
        for g in range(GROUP):
            q2_ref[g * tq:(g + 1) * tq, :] = q_ref[:, g * HEAD_DIM:(g + 1) * HEAD_DIM]
            if use_sink:
                m_ref[:, g * tq:(g + 1) * tq] = jnp.full((1, tq), sink_ref[hk * GROUP + g] * LOG2E, F32)
        if not use_sink:
            m_ref[...] = jnp.full(m_ref.shape, NEG_BIG, F32)
        acc_ref[0:HEAD_DIM, :] = jnp.zeros((HEAD_DIM, GROUP * tq), F32)
        acc_ref[HEAD_DIM:, :] = jnp.full((ONES_ROWS, GROUP * tq), 1.0 if use_sink else 0.0, F32)
        vext_ref[HEAD_DIM:, :] = jnp.ones((ONES_ROWS, vext_ref.shape[1]), BF16)
        if has_extra:
            vextx_ref[HEAD_DIM:, :] = jnp.ones((ONES_ROWS, vextx_ref.shape[1]), BF16)

    n_chunks = GROUP * tq // Q_CHUNK

    def step(kk_ref, vv_ref, vext):
        tkk = kk_ref.shape[0]
        kb = min(tkk, K_SUB)
        units = [(b, c) for b in range(tkk // kb) for c in range(n_chunks)]
        vext[0:HEAD_DIM, :] = vv_ref[...]

        def scores(u):
            b, c = u
            return _dot_nt(kk_ref[b * kb:(b + 1) * kb, :], q2_ref[c * Q_CHUNK:(c + 1) * Q_CHUNK, :])

        m = [m_ref[:, c * Q_CHUNK:(c + 1) * Q_CHUNK] for c in range(n_chunks)]
        acc = [acc_ref[:, c * Q_CHUNK:(c + 1) * Q_CHUNK] for c in range(n_chunks)]
        ahead = [scores(u) for u in units[:QK_AHEAD]]
        for idx, (b, c) in enumerate(units):
            s = ahead.pop(0)
            if idx + QK_AHEAD < len(units):
                ahead.append(scores(units[idx + QK_AHEAD]))
            m_next = jnp.maximum(m[c], jnp.max(s, axis=0, keepdims=True))
            alpha = jnp.exp2(m[c] - m_next)
            p = jnp.exp2(s - m_next).astype(BF16)
            acc[c] = alpha * acc[c] + jnp.dot(vext[:, b * kb:(b + 1) * kb], p, preferred_element_type=F32)
            m[c] = m_next
        acc_ref[...] = jnp.concatenate(acc, axis=1)
        m_ref[...] = jnp.concatenate(m, axis=1)

    @pl.when(j < n_main)
    def _():
        step(k_ref, vt_ref, vext_ref)

    if has_extra:
        @pl.when(j == n_main)
        def _():
            step(kx_ref, vtx_ref, vextx_ref)

    @pl.when(j == n_main + (1 if has_extra else 0) - 1)
    def _():
        o = (acc_ref[0:HEAD_DIM, :] / acc_ref[HEAD_DIM:HEAD_DIM + 1, :]).T
        for g in range(GROUP):
            o_ref[:, g * HEAD_DIM:(g + 1) * HEAD_DIM] = o[g * tq:(g + 1) * tq, :].astype(BF16)


def _flash(q_arr, k_arr, vt_arr, kx_arr, vtx_arr, sink, *, batch, seq_q, seq_k, seq_x, tq, tk, q_off, k_off,
           vt_row, use_sink):
    nq = seq_q // tq
    n_main = seq_k // tk
    has_extra = kx_arr is not None
    n_steps = n_main + (1 if has_extra else 0)
    qc = q_off // (GROUP * HEAD_DIM)
    kc = k_off // HEAD_DIM

    def jj(j):
        return jnp.minimum(j, n_main - 1)

    in_specs = [pl.BlockSpec(memory_space=pltpu.SMEM),
                pl.BlockSpec((tq, GROUP * HEAD_DIM), lambda b, h, i, j: (b * nq + i, qc + h)),
                pl.BlockSpec((tk, HEAD_DIM), lambda b, h, i, j: (b * n_main + jj(j), kc + h)),
                pl.BlockSpec((HEAD_DIM, tk), lambda b, h, i, j: (vt_row + h, b * n_main + jj(j)))]
    args = [sink, q_arr, k_arr, vt_arr]
    rows = GROUP * tq
    scratch = [pltpu.VMEM((rows, HEAD_DIM), BF16),
               pltpu.VMEM((1, rows), F32),
               pltpu.VMEM((HEAD_DIM + ONES_ROWS, rows), F32),
               pltpu.VMEM((HEAD_DIM + ONES_ROWS, tk), BF16)]
    if has_extra:
        in_specs += [pl.BlockSpec((seq_x, HEAD_DIM), lambda b, h, i, j: (b, kc + h)),
                     pl.BlockSpec((HEAD_DIM, seq_x), lambda b, h, i, j: (vt_row + h, b))]
        args += [kx_arr, vtx_arr]
        scratch += [pltpu.VMEM((HEAD_DIM + ONES_ROWS, seq_x), BF16)]
    return pl.pallas_call(
        functools.partial(_flash_kernel, n_main=n_main, has_extra=has_extra, use_sink=use_sink),
        out_shape=jax.ShapeDtypeStruct((batch * seq_q, Q_W), BF16),
        grid=(batch, N_KV, nq, n_steps),
        in_specs=in_specs,
        out_specs=pl.BlockSpec((tq, GROUP * HEAD_DIM), lambda b, h, i, j: (b * nq + i, h)),
        scratch_shapes=scratch,
        compiler_params=_cparams(("parallel", "parallel", "parallel", "arbitrary")),
        name="flash",
    )(*args)


def _merge_kernel(h_ref, ya_ref, yb_ref, yc_ref, yd_ref, wga_ref, wgb_ref, wgc_ref, wgd_ref,
                  wb_ref, bg_ref, o_ref):
    h = h_ref[...]
    m = None
    for k, (y_ref, wg_ref) in enumerate(((ya_ref, wga_ref), (yb_ref, wgb_ref), (yc_ref, wgc_ref), (yd_ref, wgd_ref))):
        logits = jnp.dot(h, wg_ref[...], preferred_element_type=F32) + bg_ref[k:k + 1, :]
        t = _sigmoid(logits) * jnp.dot(y_ref[...], wb_ref[k], preferred_element_type=F32)
        m = t if m is None else m + t
    o_ref[...] = m.astype(BF16)


def _merge(h, ys, wg, wb, bg, *, layer, tm, tn):
    t = h.shape[0]
    nct = D_MODEL // tn

    def y():
        return pl.BlockSpec((tm, W_BR), lambda i, j: (i, 0))

    def g(k):
        return pl.BlockSpec((None, D_MODEL, tn), lambda i, j: (layer, 0, OFF_G // tn + k * nct + j))

    return pl.pallas_call(
        _merge_kernel,
        out_shape=jax.ShapeDtypeStruct((t, D_MODEL), BF16),
        grid=(t // tm, nct),
        in_specs=[pl.BlockSpec((tm, D_MODEL), lambda i, j: (i, 0)), y(), y(), y(), y(),
                  g(0), g(1), g(2), g(3),
                  pl.BlockSpec((None, N_BRANCH, W_BR, tn), lambda i, j: (layer, 0, 0, j)),
                  pl.BlockSpec((N_BRANCH, tn), lambda i, j: (0, j))],
        out_specs=pl.BlockSpec((tm, tn), lambda i, j: (i, j)),
        compiler_params=_cparams(("parallel", "arbitrary")),
        name="merge",
    )(h, *ys, wg, wg, wg, wg, wb, bg)


def _resid_kernel(a_ref, w_ref, x_ref, gate_ref, o_ref):
    o_ref[...] = x_ref[...] + gate_ref[0] * jnp.dot(a_ref[...], w_ref[...], preferred_element_type=F32)


def _resid(a, w, x, gate, *, layer, seq, tm, tn):
    t, k = a.shape
    nbt = seq // tm
    return pl.pallas_call(
        _resid_kernel,
        out_shape=jax.ShapeDtypeStruct((t, D_MODEL), F32),
        grid=(t // tm, D_MODEL // tn),
        in_specs=[pl.BlockSpec((tm, k), lambda i, j: (i, 0)),
                  pl.BlockSpec((None, k, tn), lambda i, j: (layer, 0, j)),
                  pl.BlockSpec((tm, tn), lambda i, j: (i, j)),
                  pl.BlockSpec((1, 1, tn), lambda i, j: (i // nbt, 0, j))],
        out_specs=pl.BlockSpec((tm, tn), lambda i, j: (i, j)),
        compiler_params=_cparams(("parallel", "arbitrary")),
        name="resid",
    )(a, w, x, gate)


def _resid_norm_kernel(a_ref, w_ref, x_ref, gate_ref, gn_ref, o_ref, acc_ref, *, nk):
    k = pl.program_id(1)
    part = jnp.dot(a_ref[...], w_ref[...], preferred_element_type=F32)

    @pl.when(k == 0)
    def _():
        acc_ref[...] = part

    @pl.when(k > 0)
    def _():
        acc_ref[...] += part

    @pl.when(k == nk - 1)
    def _():
        gate = gate_ref[0]
        gn = gn_ref[...]

        def body(i, carry):
            rows = pl.ds(pl.multiple_of(i * NORM_CHUNK, NORM_CHUNK), NORM_CHUNK)
            y = x_ref[rows, :] + gate * acc_ref[rows, :]
            o_ref[rows, :] = y * lax.rsqrt(jnp.mean(y * y, axis=-1, keepdims=True) + EPS) * gn
            return carry

        lax.fori_loop(0, o_ref.shape[0] // NORM_CHUNK, body, 0, unroll=NORM_UNROLL)


def _resid_norm(a, w, x, gate, gn, *, layer, seq, tm, tk):
    t, k = a.shape
    nbt = seq // tm
    nk = k // tk
    return pl.pallas_call(
        functools.partial(_resid_norm_kernel, nk=nk),
        out_shape=jax.ShapeDtypeStruct((t, D_MODEL), F32),
        grid=(t // tm, nk),
        in_specs=[pl.BlockSpec((tm, tk), lambda i, j: (i, j)),
                  pl.BlockSpec((None, tk, D_MODEL), lambda i, j: (layer, j, 0)),
                  pl.BlockSpec((tm, D_MODEL), lambda i, j: (i, 0)),
                  pl.BlockSpec((1, 1, D_MODEL), lambda i, j: (i // nbt, 0, 0)),
                  pl.BlockSpec((1, D_MODEL), lambda i, j: (0, 0))],
        out_specs=pl.BlockSpec((tm, D_MODEL), lambda i, j: (i, 0)),
        scratch_shapes=[pltpu.VMEM((tm, D_MODEL), F32)],
        compiler_params=_cparams(("parallel", "arbitrary")),
        name="resid_norm",
    )(a, w, x, gate, gn)


def _ffn_in_kernel(x_ref, g_ref, sc_ref, sh_ref, wa_ref, wb_ref, o_ref, h_ref):
    @pl.when(pl.program_id(1) == 0)
    def _():
        _norm_mod_rows(x_ref, g_ref, sc_ref, sh_ref, h_ref)

    h = h_ref[...]
    a = jnp.dot(h, wa_ref[...], preferred_element_type=F32)
    b = jnp.dot(h, wb_ref[...], preferred_element_type=F32)
    o_ref[...] = (a * _sigmoid(a) * b).astype(BF16)


def _ffn_in(x, g, sc, sh, w, *, layer, seq, tm, tn):
    t = x.shape[0]
    nbt = seq // tm
    nct = FFN_HIDDEN // tn
    return pl.pallas_call(
        _ffn_in_kernel,
        out_shape=jax.ShapeDtypeStruct((t, FFN_HIDDEN), BF16),
        grid=(t // tm, nct),
        in_specs=[pl.BlockSpec((tm, D_MODEL), lambda i, j: (i, 0)),
                  pl.BlockSpec((1, D_MODEL), lambda i, j: (0, 0)),
                  pl.BlockSpec((1, 1, D_MODEL), lambda i, j: (i // nbt, 0, 0)),
                  pl.BlockSpec((1, 1, D_MODEL), lambda i, j: (i // nbt, 0, 0)),
                  pl.BlockSpec((None, D_MODEL, tn), lambda i, j: (layer, 0, j)),
                  pl.BlockSpec((None, D_MODEL, tn), lambda i, j: (layer, 0, nct + j))],
        out_specs=pl.BlockSpec((tm, tn), lambda i, j: (i, j)),
        scratch_shapes=[pltpu.VMEM((tm, D_MODEL), BF16)],
        compiler_params=_cparams(("parallel", "arbitrary")),
        name="ffn_in",
    )(x, g, sc, sh, w, w)


def _rope_tables(seq):
    pos = jnp.arange(seq)
    row = (pos // GRID_W).astype(F32)
    col = (pos % GRID_W).astype(F32)
    inv = ROPE_THETA ** (-jnp.arange(0, ROT_AXIS, 2, dtype=F32) / ROT_AXIS)
    ar = row[:, None] * inv
    ac = col[:, None] * inv
    ang = jnp.concatenate([ar, ar, ac, ac], axis=-1)
    cos, sin = jnp.cos(ang), jnp.sin(ang)
    first = (jnp.arange(HEAD_DIM) % ROT_AXIS) < ROT_AXIS // 2
    return cos, jnp.where(first, -sin, 0.0), jnp.where(first, 0.0, sin)


def _tile(n, pref):
    return pref if n % pref == 0 else n


def _forward(x, c, ctx, c_ctx, w_ada, b_ada, norm_mix, norm_ffn, w_in, b_gate, conv_a_w, sink_b,
             qk_norm_q, qk_norm_k, conv_c_w, conv_c_b, ln_c_g, ln_c_b, w_branch, w_out,
             w_ffn_in, w_ffn_out, norm_final):
    batch, seq, _ = x.shape
    n_ctx = ctx.shape[1]
    depth = w_in.shape[0]
    xs = x.reshape(batch * seq, D_MODEL)
    xc = ctx.reshape(batch * n_ctx, D_MODEL)

    tm = _tile(seq, 1024)
    tm_conv_a = _tile(seq, 512)
    tm_conv_c = _tile(seq, 256)
    tq_win = _tile(seq, 512)
    tq = _tile(seq, 1024)
    tk = _tile(seq, 4096)

    rope_x = _rope_tables(seq)
    rope_c = (jnp.ones((n_ctx, HEAD_DIM), F32), jnp.zeros((n_ctx, HEAD_DIM), F32),
              jnp.zeros((n_ctx, HEAD_DIM), F32))
    cvec = jnp.zeros((ADA_ROWS, D_MODEL), F32).at[:batch].set(c).at[batch].set(c_ctx)

    def row(v):
        return v.reshape(1, -1)

    w_proj = w_gate = w_in.astype(BF16)
    w_br = w_branch.astype(BF16)
    w_o = w_out.astype(BF16)
    w_f1 = w_ffn_in.astype(BF16)
    w_f2 = w_ffn_out.astype(BF16)

    def mixers(pf, pq, vt, pq_c, vt_c, l, *, s, tma, tmc, is_ctx):
        y_a = _conv_a(pf, conv_a_w[l], seq=s, tm=tma)
        y_c = _conv_c(pf, conv_c_w[l], row(conv_c_b[l]), row(ln_c_g[l]), row(ln_c_b[l]), seq=s, tm=tmc)
        off_d = Q_W + 2 * KV_W
        if is_ctx:
            y_b = _flash(pq, pq, vt, None, None, sink_b[l], batch=batch, seq_q=s, seq_k=s, seq_x=0, tq=s, tk=s,
                         q_off=0, k_off=Q_W, vt_row=0, use_sink=True)
            y_d = _flash(pq, pq, vt, None, None, sink_b[l], batch=batch, seq_q=s, seq_k=s, seq_x=0, tq=s, tk=s,
                         q_off=off_d, k_off=off_d + Q_W, vt_row=N_KV, use_sink=False)
        else:
            y_b = _win_attn(pq, pq_c, sink_b[l], batch=batch, seq=s, ctx=n_ctx, tq=tq_win)
            y_d = _flash(pq, pq, vt, pq_c, vt_c, sink_b[l], batch=batch, seq_q=s, seq_k=s, seq_x=n_ctx,
                         tq=tq, tk=tk, q_off=off_d, k_off=off_d + Q_W, vt_row=N_KV, use_sink=False)
        return [y_a, y_b, y_c, y_d]

    for l in range(depth):
        last = l == depth - 1
        mod = _ada(cvec, w_ada, row(b_ada[l]), layer=l)
        mx = mod[:batch].reshape(batch, 1, 6, D_MODEL)
        mcx = jnp.broadcast_to(mod[batch].reshape(1, 1, 6, D_MODEL), (batch, 1, 6, D_MODEL))
        sh_m, sc_m, g_m, sh_f, sc_f, g_f = (mx[:, :, k] for k in range(6))
        csh_m, csc_m, cg_m, csh_f, csc_f, cg_f = (mcx[:, :, k] for k in range(6))
        gq, gk = row(qk_norm_q[l]), row(qk_norm_k[l])

        pf_c, pq_c, h_c, vt_c = _inproj(xc, row(norm_mix[l]), csc_m, csh_m, w_proj, *rope_c, gq, gk,
                                         layer=l, seq=n_ctx, tm=n_ctx)
        pf, pq, h, vt = _inproj(xs, row(norm_mix[l]), sc_m, sh_m, w_proj, *rope_x, gq, gk,
                                 layer=l, seq=seq, tm=tm)

        ys = mixers(pf, pq, vt, pq_c, vt_c, l, s=seq, tma=tm_conv_a, tmc=tm_conv_c, is_ctx=False)
        m = _merge(h, ys, w_gate, w_br, b_gate[l], layer=l, tm=tm, tn=512)
        xs = _resid(m, w_o, xs, g_m, layer=l, seq=seq, tm=tm, tn=512)

        if not last:
            ys_c = mixers(pf_c, pq_c, vt_c, None, None, l, s=n_ctx, tma=n_ctx, tmc=n_ctx, is_ctx=True)
            m_c = _merge(h_c, ys_c, w_gate, w_br, b_gate[l], layer=l, tm=n_ctx, tn=512)
            xc = _resid(m_c, w_o, xc, cg_m, layer=l, seq=n_ctx, tm=n_ctx, tn=512)
            hid_c = _ffn_in(xc, row(norm_ffn[l]), csc_f, csh_f, w_f1, layer=l, seq=n_ctx, tm=n_ctx, tn=512)
            xc = _resid(hid_c, w_f2, xc, cg_f, layer=l, seq=n_ctx, tm=n_ctx, tn=256)

        hid = _ffn_in(xs, row(norm_ffn[l]), sc_f, sh_f, w_f1, layer=l, seq=seq, tm=tm, tn=512)
        if last:
            xs = _resid_norm(hid, w_f2, xs, g_f, row(norm_final), layer=l, seq=seq, tm=_tile(seq, 512),
                             tk=FFN_HIDDEN // 2)
        else:
            xs = _resid(hid, w_f2, xs, g_f, layer=l, seq=seq, tm=tm, tn=256)

    return xs.reshape(batch, seq, D_MODEL)


def kernel(x, c, ctx, c_ctx, w_ada, b_ada, norm_mix, norm_ffn, w_in, b_gate, conv_a_w, sink_b, qk_norm_q,
           qk_norm_k, conv_c_w, conv_c_b, ln_c_g, ln_c_b, w_branch, w_out, w_ffn_in, w_ffn_out, norm_final):
    return _forward(x, c, ctx, c_ctx, w_ada, b_ada, norm_mix, norm_ffn, w_in, b_gate, conv_a_w, sink_b,
                    qk_norm_q, qk_norm_k, conv_c_w, conv_c_b, ln_c_g, ln_c_b, w_branch, w_out,
                    w_ffn_in, w_ffn_out, norm_final)
```

```python
import functools

import jax
import jax.numpy as jnp
from jax import lax
from jax.experimental import pallas as pl
from jax.experimental.pallas import tpu as pltpu

F32 = jnp.float32
BF16 = jnp.bfloat16

D_MODEL = 2048
N_BRANCH = 4
W_BR = D_MODEL // 4
HEAD_DIM = 128
N_Q = W_BR // HEAD_DIM
N_KV = N_Q // 2
GROUP = N_Q // N_KV
Q_W = N_Q * HEAD_DIM
KV_W = N_KV * HEAD_DIM
CONV_A = 3
CONV_C = 31
WINDOW = 128
GRID_W = 64
ROT_AXIS = HEAD_DIM // 2
ROPE_THETA = 10000.0
FFN_HIDDEN = 5632
EPS = 1e-6
ATTN_SCALE = HEAD_DIM ** -0.5
LOG2E = 1.4426950408889634
Q_SCALE = ATTN_SCALE * LOG2E
NEG_BIG = -1e30

OFF_A = 0
OFF_BQ = OFF_A + 3 * W_BR
OFF_BK = OFF_BQ + Q_W
OFF_BV = OFF_BK + KV_W
OFF_C = OFF_BV + KV_W
OFF_DQ = OFF_C + 2 * W_BR
OFF_DK = OFF_DQ + Q_W
OFF_DV = OFF_DK + KV_W
OFF_G = OFF_DV + KV_W

PF_W = 3 * W_BR + 2 * W_BR
PQ_W = 2 * (Q_W + 2 * KV_W)
TN = 512
N_PF_TILES = PF_W // TN
N_PROJ_TILES = (PF_W + PQ_W) // TN

VMEM_LIMIT = 56 * 1024 * 1024


def _cparams(sem):
    return pltpu.CompilerParams(dimension_semantics=sem, vmem_limit_bytes=VMEM_LIMIT)


def _sigmoid(z):
    return 1.0 / (1.0 + jnp.exp(-z))


def _ada_kernel(c_ref, w_ref, b_ref, o_ref):
    c = c_ref[...]
    s = (c * _sigmoid(c)).astype(BF16)
    o_ref[...] = jnp.dot(s, w_ref[...].astype(BF16), preferred_element_type=F32) + b_ref[...]


ADA_ROWS = 16


def _ada(cvec, w, b, *, layer):
    n = w.shape[2]
    tn = 1024
    return pl.pallas_call(
        _ada_kernel,
        out_shape=jax.ShapeDtypeStruct((ADA_ROWS, n), F32),
        grid=(n // tn,),
        in_specs=[pl.BlockSpec((ADA_ROWS, D_MODEL), lambda j: (0, 0)),
                  pl.BlockSpec((None, D_MODEL, tn), lambda j: (layer, 0, j)),
                  pl.BlockSpec((1, tn), lambda j: (0, j))],
        out_specs=pl.BlockSpec((ADA_ROWS, tn), lambda j: (0, j)),
        compiler_params=_cparams(("arbitrary",)),
        name="ada",
    )(cvec, w, b)


def _rope(xh, cos, sin_signed):
    return xh * cos + pltpu.roll(xh, HEAD_DIM // 2, 1) * sin_signed


def _head_norm(xh, g):
    return xh * lax.rsqrt(jnp.mean(xh * xh, axis=-1, keepdims=True) + EPS) * g


NORM_CHUNK = 16
NORM_UNROLL = 8


def _norm_mod_rows(x_ref, g_ref, sc_ref, sh_ref, h_ref):
    gs = g_ref[...] * (1.0 + sc_ref[0])
    sh = sh_ref[0]

    def body(i, carry):
        rows = pl.ds(pl.multiple_of(i * NORM_CHUNK, NORM_CHUNK), NORM_CHUNK)
        x = x_ref[rows, :]
        r = lax.rsqrt(jnp.mean(x * x, axis=-1, keepdims=True) + EPS)
        h_ref[rows, :] = (x * r * gs + sh).astype(BF16)
        return carry

    lax.fori_loop(0, x_ref.shape[0] // NORM_CHUNK, body, 0, unroll=NORM_UNROLL)


def _inproj_kernel(x_ref, g_ref, sc_ref, sh_ref, w_ref, cos_ref, sin_ref, gq_ref, gk_ref,
                   pf_ref, pq_ref, h_ref, vt_ref):
    j = pl.program_id(1)

    @pl.when(j == 0)
    def _():
        _norm_mod_rows(x_ref, g_ref, sc_ref, sh_ref, h_ref)

    acc = jnp.dot(h_ref[...], w_ref[...], preferred_element_type=F32)

    @pl.when(j < N_PF_TILES)
    def _():
        pf_ref[...] = acc

    def rope(xh):
        return _rope(xh, cos_ref[...], sin_ref[...])

    def head(k):
        return acc[:, k * HEAD_DIM:(k + 1) * HEAD_DIM]

    def put(k, v):
        pq_ref[:, k * HEAD_DIM:(k + 1) * HEAD_DIM] = v.astype(BF16)

    @pl.when(j == N_PF_TILES)
    def _():
        for k in range(N_Q):
            put(k, rope(head(k)) * Q_SCALE)

    @pl.when(j == N_PF_TILES + 1)
    def _():
        for k in range(N_KV):
            put(k, rope(head(k)))
        pq_ref[:, KV_W:] = acc[:, KV_W:].astype(BF16)
        vt_ref[0:KV_W, :] = acc[:, KV_W:].T.astype(BF16)

    @pl.when(j == N_PF_TILES + 2)
    def _():
        for k in range(N_Q):
            put(k, rope(_head_norm(head(k), gq_ref[...])) * Q_SCALE)

    @pl.when(j == N_PF_TILES + 3)
    def _():
        for k in range(N_KV):
            put(k, rope(_head_norm(head(k), gk_ref[...])))
        pq_ref[:, KV_W:] = acc[:, KV_W:].astype(BF16)
        vt_ref[KV_W:, :] = acc[:, KV_W:].T.astype(BF16)


def _proj_block(j):
    n_a, n_b, n_c = 3 * W_BR // TN, (Q_W + 2 * KV_W) // TN, 2 * W_BR // TN
    return jnp.where(j < n_a, j, jnp.where(j < n_a + n_c, j + n_b, jnp.where(j < n_a + n_c + n_b, j - n_c, j)))


def _inproj(x, g, sc, sh, w, cos, sin_signed, gq, gk, *, layer, seq, tm):
    t = x.shape[0]
    nbt = seq // tm
    return pl.pallas_call(
        _inproj_kernel,
        out_shape=(jax.ShapeDtypeStruct((t, PF_W), F32),
                   jax.ShapeDtypeStruct((t, PQ_W), BF16),
                   jax.ShapeDtypeStruct((t, D_MODEL), BF16),
                   jax.ShapeDtypeStruct((2 * KV_W, t), BF16)),
        grid=(t // tm, N_PROJ_TILES),
        in_specs=[pl.BlockSpec((tm, D_MODEL), lambda i, j: (i, 0)),
                  pl.BlockSpec((1, D_MODEL), lambda i, j: (0, 0)),
                  pl.BlockSpec((1, 1, D_MODEL), lambda i, j: (i // nbt, 0, 0)),
                  pl.BlockSpec((1, 1, D_MODEL), lambda i, j: (i // nbt, 0, 0)),
                  pl.BlockSpec((None, D_MODEL, TN), lambda i, j: (layer, 0, _proj_block(j))),
                  pl.BlockSpec((tm, HEAD_DIM), lambda i, j: (i % nbt, 0)),
                  pl.BlockSpec((tm, HEAD_DIM), lambda i, j: (i % nbt, 0)),
                  pl.BlockSpec((1, HEAD_DIM), lambda i, j: (0, 0)),
                  pl.BlockSpec((1, HEAD_DIM), lambda i, j: (0, 0))],
        out_specs=(pl.BlockSpec((tm, TN), lambda i, j: (i, jnp.minimum(j, N_PF_TILES - 1))),
                   pl.BlockSpec((tm, TN), lambda i, j: (i, jnp.clip(j - N_PF_TILES, 0, PQ_W // TN - 1))),
                   pl.BlockSpec((tm, D_MODEL), lambda i, j: (i, 0)),
                   pl.BlockSpec((2 * KV_W, tm), lambda i, j: (0, i))),
        compiler_params=_cparams(("parallel", "arbitrary")),
        name="inproj",
    )(x, g, sc, sh, w, cos, sin_signed, gq, gk)


HALO_A = 8


def _conv_a_kernel(bg_ref, cg_ref, h_ref, cgp_ref, hp_ref, cgn_ref, hn_ref, w_ref, o_ref, u_ref, *, nbt):
    it = pl.program_id(0) % nbt
    tm = bg_ref.shape[0]
    u_ref[HALO_A:HALO_A + tm, :] = cg_ref[...] * h_ref[...]
    u_ref[0:HALO_A, :] = jnp.where(it > 0, cgp_ref[...] * hp_ref[...], 0.0)
    u_ref[HALO_A + tm:, :] = jnp.where(it < nbt - 1, cgn_ref[...] * hn_ref[...], 0.0)
    y = (w_ref[0:1, :] * u_ref[HALO_A - 1:HALO_A - 1 + tm, :]
         + w_ref[1:2, :] * u_ref[HALO_A:HALO_A + tm, :]
         + w_ref[2:3, :] * u_ref[HALO_A + 1:HALO_A + 1 + tm, :])
    o_ref[...] = (bg_ref[...] * y).astype(BF16)


def _conv_a(pf, w, *, seq, tm):
    t = pf.shape[0]
    nbt = seq // tm
    r = tm // HALO_A
    last = t // HALO_A - 1

    def cur(c):
        return pl.BlockSpec((tm, W_BR), lambda i: (i, c))

    def prev(c):
        return pl.BlockSpec((HALO_A, W_BR), lambda i: (jnp.maximum(i * r - 1, 0), c))

    def nxt(c):
        return pl.BlockSpec((HALO_A, W_BR), lambda i: (jnp.minimum((i + 1) * r, last), c))

    return pl.pallas_call(
        functools.partial(_conv_a_kernel, nbt=nbt),
        out_shape=jax.ShapeDtypeStruct((t, W_BR), BF16),
        grid=(t // tm,),
        in_specs=[cur(0), cur(1), cur(2), prev(1), prev(2), nxt(1), nxt(2),
                  pl.BlockSpec((CONV_A, W_BR), lambda i: (0, 0))],
        out_specs=pl.BlockSpec((tm, W_BR), lambda i: (i, 0)),
        scratch_shapes=[pltpu.VMEM((tm + 2 * HALO_A, W_BR), F32)],
        compiler_params=_cparams(("parallel",)),
        name="conv_a",
    )(pf, pf, pf, pf, pf, pf, pf, w)


HALO_C = 16
ROWS_C = 64
SUBLANES = 8


def _conv_c_kernel(v_ref, g_ref, vp_ref, gp_ref, vn_ref, gn_ref, w_ref, b_ref, lg_ref, lb_ref,
                   o_ref, u_ref, us_ref, *, nbt):
    it = pl.program_id(0) % nbt
    tm = v_ref.shape[0]
    u_ref[HALO_C:HALO_C + tm, :] = v_ref[...] * _sigmoid(g_ref[...])
    u_ref[0:HALO_C, :] = jnp.where(it > 0, vp_ref[...] * _sigmoid(gp_ref[...]), 0.0)
    u_ref[HALO_C + tm:, :] = jnp.where(it < nbt - 1, vn_ref[...] * _sigmoid(gn_ref[...]), 0.0)
    n_sh = us_ref.shape[1]
    for r in range(1, SUBLANES):
        us_ref[r - 1] = u_ref[r:r + n_sh, :]
    pad = CONV_C // 2

    def tap(row):
        a, r = divmod(row, SUBLANES)
        if r == 0:
            return u_ref[a * SUBLANES:a * SUBLANES + ROWS_C, :]
        return us_ref[r - 1, a * SUBLANES:a * SUBLANES + ROWS_C, :]

    for c in range(tm // ROWS_C):
        base = HALO_C + c * ROWS_C - pad
        acc = w_ref[0:1, :] * tap(base)
        for k in range(1, CONV_C):
            acc = acc + w_ref[k:k + 1, :] * tap(base + k)
        acc = acc + b_ref[...]
        mu = jnp.mean(acc, axis=-1, keepdims=True)
        xc = acc - mu
        var = jnp.mean(xc * xc, axis=-1, keepdims=True)
        y = xc * lax.rsqrt(var + EPS) * lg_ref[...] + lb_ref[...]
        o_ref[c * ROWS_C:(c + 1) * ROWS_C, :] = (y * _sigmoid(y)).astype(BF16)


def _conv_c(pf, w, b, lg, lb, *, seq, tm):
    t = pf.shape[0]
    nbt = seq // tm
    r = tm // HALO_C
    last = t // HALO_C - 1
    c0 = 3

    def cur(c):
        return pl.BlockSpec((tm, W_BR), lambda i: (i, c))

    def prev(c):
        return pl.BlockSpec((HALO_C, W_BR), lambda i: (jnp.maximum(i * r - 1, 0), c))

    def nxt(c):
        return pl.BlockSpec((HALO_C, W_BR), lambda i: (jnp.minimum((i + 1) * r, last), c))

    def vec():
        return pl.BlockSpec((1, W_BR), lambda i: (0, 0))

    return pl.pallas_call(
        functools.partial(_conv_c_kernel, nbt=nbt),
        out_shape=jax.ShapeDtypeStruct((t, W_BR), BF16),
        grid=(t // tm,),
        in_specs=[cur(c0), cur(c0 + 1), prev(c0), prev(c0 + 1), nxt(c0), nxt(c0 + 1),
                  pl.BlockSpec((CONV_C, W_BR), lambda i: (0, 0)), vec(), vec(), vec()],
        out_specs=pl.BlockSpec((tm, W_BR), lambda i: (i, 0)),
        scratch_shapes=[pltpu.VMEM((tm + 2 * HALO_C, W_BR), F32),
                        pltpu.VMEM((SUBLANES - 1, tm + 2 * HALO_C - SUBLANES, W_BR), F32)],
        compiler_params=_cparams(("parallel",)),
        name="conv_c",
    )(pf, pf, pf, pf, pf, pf, w, b, lg, lb)


def _dot_nt(a, b):
    return lax.dot_general(a, b, (((1,), (1,)), ((), ())), preferred_element_type=F32)


def _win_kernel(sink_ref, q_ref, kp_ref, kc_ref, kn_ref, kx_ref, vp_ref, vc_ref, vn_ref, vx_ref, o_ref, vext_ref,
                *, nq):
    hk = pl.program_id(1)
    i = pl.program_id(2)
    tq = q_ref.shape[0]
    n_loc = tq + 2 * WINDOW
    kk = jnp.concatenate([kp_ref[...], kc_ref[...], kn_ref[...], kx_ref[...]], axis=0)
    n_all = kk.shape[0]
    vext_ref[0:HEAD_DIM, 0:WINDOW] = vp_ref[...]
    vext_ref[0:HEAD_DIM, WINDOW:WINDOW + tq] = vc_ref[...]
    vext_ref[0:HEAD_DIM, WINDOW + tq:n_loc] = vn_ref[...]
    vext_ref[0:HEAD_DIM, n_loc:] = vx_ref[...]
    vext_ref[HEAD_DIM:, :] = jnp.ones((ONES_ROWS, n_all), BF16)
    key = lax.broadcasted_iota(jnp.int32, (n_all, tq), 0)
    qry = lax.broadcasted_iota(jnp.int32, (n_all, tq), 1)
    rel = key - WINDOW - qry
    lo = jnp.where(i > 0, 0, WINDOW)
    hi = jnp.where(i < nq - 1, n_loc, tq + WINDOW)
    valid = (key >= n_loc) | ((jnp.abs(rel) <= WINDOW) & (key >= lo) & (key < hi))

    def scores(g):
        return _dot_nt(kk, q_ref[:, g * HEAD_DIM:(g + 1) * HEAD_DIM])

    ahead = [scores(0)]
    outs = []
    for g in range(GROUP):
        s = jnp.where(valid, ahead.pop(0), NEG_BIG)
        if g + 1 < GROUP:
            ahead.append(scores(g + 1))
        sink = sink_ref[hk * GROUP + g] * LOG2E
        m = jnp.maximum(jnp.max(s, axis=0, keepdims=True), sink)
        p = jnp.exp2(s - m).astype(BF16)
        acc = jnp.dot(vext_ref[...], p, preferred_element_type=F32)
        l = acc[HEAD_DIM:HEAD_DIM + 1, :] + jnp.exp2(sink - m)
        outs.append((acc[0:HEAD_DIM, :] / l).T.astype(BF16))
    o_ref[...] = jnp.concatenate(outs, axis=1)


def _win_attn(pq, vt, pq_c, vt_c, sink, *, batch, seq, ctx, tq):
    nq = seq // tq
    r = tq // WINDOW
    nblk = seq // WINDOW
    kcol = Q_W // HEAD_DIM

    def prev_blk(b, i):
        return b * nblk + jnp.maximum(i * r - 1, 0)

    def next_blk(b, i):
        return b * nblk + jnp.minimum((i + 1) * r, nblk - 1)

    return pl.pallas_call(
        functools.partial(_win_kernel, nq=nq),
        out_shape=jax.ShapeDtypeStruct((batch * seq, Q_W), BF16),
        grid=(batch, N_KV, nq),
        in_specs=[pl.BlockSpec(memory_space=pltpu.SMEM),
                  pl.BlockSpec((tq, GROUP * HEAD_DIM), lambda b, h, i: (b * nq + i, h)),
                  pl.BlockSpec((WINDOW, HEAD_DIM), lambda b, h, i: (prev_blk(b, i), kcol + h)),
                  pl.BlockSpec((tq, HEAD_DIM), lambda b, h, i: (b * nq + i, kcol + h)),
                  pl.BlockSpec((WINDOW, HEAD_DIM), lambda b, h, i: (next_blk(b, i), kcol + h)),
                  pl.BlockSpec((ctx, HEAD_DIM), lambda b, h, i: (b, kcol + h)),
                  pl.BlockSpec((HEAD_DIM, WINDOW), lambda b, h, i: (h, prev_blk(b, i))),
                  pl.BlockSpec((HEAD_DIM, tq), lambda b, h, i: (h, b * nq + i)),
                  pl.BlockSpec((HEAD_DIM, WINDOW), lambda b, h, i: (h, next_blk(b, i))),
                  pl.BlockSpec((HEAD_DIM, ctx), lambda b, h, i: (h, b))],
        out_specs=pl.BlockSpec((tq, GROUP * HEAD_DIM), lambda b, h, i: (b * nq + i, h)),
        scratch_shapes=[pltpu.VMEM((HEAD_DIM + ONES_ROWS, tq + 2 * WINDOW + ctx), BF16)],
        compiler_params=_cparams(("parallel", "parallel", "parallel")),
        name="win_attn",
    )(sink, pq, pq, pq, pq, pq_c, vt, vt, vt, vt_c)


ONES_ROWS = 16
Q_CHUNK = 512
K_SUB = 1024
QK_AHEAD = 2


def _flash_kernel(sink_ref, q_ref, k_ref, vt_ref, *rest, n_main, has_extra, use_sink):
    if has_extra:
        kx_ref, vtx_ref, o_ref, q2_ref, m_ref, acc_ref, vext_ref, vextx_ref = rest
    else:
        o_ref, q2_ref, m_ref, acc_ref, vext_ref = rest
    hk = pl.program_id(1)
    j = pl.program_id(3)
    tq = q_ref.shape[0]

    @pl.when(j == 0)
    def _():
        for g in range(GROUP):
            q2_ref[g * tq:(g + 1) * tq, :] = q_ref[:, g * HEAD_DIM:(g + 1) * HEAD_DIM]
            if use_sink:
                m_ref[:, g * tq:(g + 1) * tq] = jnp.full((1, tq), sink_ref[hk * GROUP + g] * LOG2E, F32)
        if not use_sink:
            m_ref[...] = jnp.full(m_ref.shape, NEG_BIG, F32)
        acc_ref[0:HEAD_DIM, :] = jnp.zeros((HEAD_DIM, GROUP * tq), F32)
        acc_ref[HEAD_DIM:, :] = jnp.full((ONES_ROWS, GROUP * tq), 1.0 if use_sink else 0.0, F32)
        vext_ref[HEAD_DIM:, :] = jnp.ones((ONES_ROWS, vext_ref.shape[1]), BF16)
        if has_extra:
            vextx_ref[HEAD_DIM:, :] = jnp.ones((ONES_ROWS, vextx_ref.shape[1]), BF16)

    n_chunks = GROUP * tq // Q_CHUNK

    def step(kk_ref, vv_ref, vext):
        tkk = kk_ref.shape[0]
        kb = min(tkk, K_SUB)
        units = [(b, c) for b in range(tkk // kb) for c in range(n_chunks)]
        vext[0:HEAD_DIM, :] = vv_ref[...]

        def scores(u):
            b, c = u
            return _dot_nt(kk_ref[b * kb:(b + 1) * kb, :], q2_ref[c * Q_CHUNK:(c + 1) * Q_CHUNK, :])

        m = [m_ref[:, c * Q_CHUNK:(c + 1) * Q_CHUNK] for c in range(n_chunks)]
        acc = [acc_ref[:, c * Q_CHUNK:(c + 1) * Q_CHUNK] for c in range(n_chunks)]
        ahead = [scores(u) for u in units[:QK_AHEAD]]
        for idx, (b, c) in enumerate(units):
            s = ahead.pop(0)
            if idx + QK_AHEAD < len(units):
                ahead.append(scores(units[idx + QK_AHEAD]))
            m_next = jnp.maximum(m[c], jnp.max(s, axis=0, keepdims=True))
            alpha = jnp.exp2(m[c] - m_next)
            p = jnp.exp2(s - m_next).astype(BF16)
            acc[c] = alpha * acc[c] + jnp.dot(vext[:, b * kb:(b + 1) * kb], p, preferred_element_type=F32)
            m[c] = m_next
        acc_ref[...] = jnp.concatenate(acc, axis=1)
        m_ref[...] = jnp.concatenate(m, axis=1)

    @pl.when(j < n_main)
    def _():
        step(k_ref, vt_ref, vext_ref)

    if has_extra:
        @pl.when(j == n_main)
        def _():
            step(kx_ref, vtx_ref, vextx_ref)

    @pl.when(j == n_main + (1 if has_extra else 0) - 1)
    def _():
        o = (acc_ref[0:HEAD_DIM, :] / acc_ref[HEAD_DIM:HEAD_DIM + 1, :]).T
        for g in range(GROUP):
            o_ref[:, g * HEAD_DIM:(g + 1) * HEAD_DIM] = o[g * tq:(g + 1) * tq, :].astype(BF16)


def _flash(q_arr, k_arr, vt_arr, kx_arr, vtx_arr, sink, *, batch, seq_q, seq_k, seq_x, tq, tk, q_off, k_off,
           vt_row, use_sink):
    nq = seq_q // tq
    n_main = seq_k // tk
    has_extra = kx_arr is not None
    n_steps = n_main + (1 if has_extra else 0)
    qc = q_off // (GROUP * HEAD_DIM)
    kc = k_off // HEAD_DIM

    def jj(j):
        return jnp.minimum(j, n_main - 1)

    in_specs = [pl.BlockSpec(memory_space=pltpu.SMEM),
                pl.BlockSpec((tq, GROUP * HEAD_DIM), lambda b, h, i, j: (b * nq + i, qc + h)),
                pl.BlockSpec((tk, HEAD_DIM), lambda b, h, i, j: (b * n_main + jj(j), kc + h)),
                pl.BlockSpec((HEAD_DIM, tk), lambda b, h, i, j: (vt_row + h, b * n_main + jj(j)))]
    args = [sink, q_arr, k_arr, vt_arr]
    rows = GROUP * tq
    scratch = [pltpu.VMEM((rows, HEAD_DIM), BF16),
               pltpu.VMEM((1, rows), F32),
               pltpu.VMEM((HEAD_DIM + ONES_ROWS, rows), F32),
               pltpu.VMEM((HEAD_DIM + ONES_ROWS, tk), BF16)]
    if has_extra:
        in_specs += [pl.BlockSpec((seq_x, HEAD_DIM), lambda b, h, i, j: (b, kc + h)),
                     pl.BlockSpec((HEAD_DIM, seq_x), lambda b, h, i, j: (vt_row + h, b))]
        args += [kx_arr, vtx_arr]
        scratch += [pltpu.VMEM((HEAD_DIM + ONES_ROWS, seq_x), BF16)]
    return pl.pallas_call(
        functools.partial(_flash_kernel, n_main=n_main, has_extra=has_extra, use_sink=use_sink),
        out_shape=jax.ShapeDtypeStruct((batch * seq_q, Q_W), BF16),
        grid=(batch, N_KV, nq, n_steps),
        in_specs=in_specs,
        out_specs=pl.BlockSpec((tq, GROUP * HEAD_DIM), lambda b, h, i, j: (b * nq + i, h)),
        scratch_shapes=scratch,
        compiler_params=_cparams(("parallel", "parallel", "parallel", "arbitrary")),
        name="flash",
    )(*args)


def _merge_kernel(h_ref, ya_ref, yb_ref, yc_ref, yd_ref, wga_ref, wgb_ref, wgc_ref, wgd_ref,
                  wb_ref, bg_ref, o_ref):
    h = h_ref[...]
    m = None
    for k, (y_ref, wg_ref) in enumerate(((ya_ref, wga_ref), (yb_ref, wgb_ref), (yc_ref, wgc_ref), (yd_ref, wgd_ref))):
        logits = jnp.dot(h, wg_ref[...], preferred_element_type=F32) + bg_ref[k:k + 1, :]
        t = _sigmoid(logits) * jnp.dot(y_ref[...], wb_ref[k], preferred_element_type=F32)
        m = t if m is None else m + t
    o_ref[...] = m.astype(BF16)


def _merge(h, ys, wg, wb, bg, *, layer, tm, tn):
    t = h.shape[0]
    nct = D_MODEL // tn

    def y():
        return pl.BlockSpec((tm, W_BR), lambda i, j: (i, 0))

    def g(k):
        return pl.BlockSpec((None, D_MODEL, tn), lambda i, j: (layer, 0, OFF_G // tn + k * nct + j))

    return pl.pallas_call(
        _merge_kernel,
        out_shape=jax.ShapeDtypeStruct((t, D_MODEL), BF16),
        grid=(t // tm, nct),
        in_specs=[pl.BlockSpec((tm, D_MODEL), lambda i, j: (i, 0)), y(), y(), y(), y(),
                  g(0), g(1), g(2), g(3),
                  pl.BlockSpec((None, N_BRANCH, W_BR, tn), lambda i, j: (layer, 0, 0, j)),
                  pl.BlockSpec((N_BRANCH, tn), lambda i, j: (0, j))],
        out_specs=pl.BlockSpec((tm, tn), lambda i, j: (i, j)),
        compiler_params=_cparams(("parallel", "arbitrary")),
        name="merge",
    )(h, *ys, wg, wg, wg, wg, wb, bg)


def _resid_kernel(a_ref, w_ref, x_ref, gate_ref, o_ref, *scratch, nk):
    part = jnp.dot(a_ref[...], w_ref[...], preferred_element_type=F32)
    if nk == 1:
        o_ref[...] = x_ref[...] + gate_ref[0] * part
        return
    (acc_ref,) = scratch
    k = pl.program_id(1)

    @pl.when(k == 0)
    def _():
        acc_ref[...] = part

    if nk > 2:
        @pl.when((k > 0) & (k < nk - 1))
        def _():
            acc_ref[...] += part

    @pl.when(k == nk - 1)
    def _():
        o_ref[...] = x_ref[...] + gate_ref[0] * (acc_ref[...] + part)


def _resid(a, w, x, gate, *, layer, seq, tm, tk):
    t, k = a.shape
    nbt = seq // tm
    nk = k // tk
    return pl.pallas_call(
        functools.partial(_resid_kernel, nk=nk),
        out_shape=jax.ShapeDtypeStruct((t, D_MODEL), F32),
        grid=(t // tm, nk),
        in_specs=[pl.BlockSpec((tm, tk), lambda i, j: (i, j)),
                  pl.BlockSpec((None, tk, D_MODEL), lambda i, j: (layer, j, 0)),
                  pl.BlockSpec((tm, D_MODEL), lambda i, j: (i, 0)),
                  pl.BlockSpec((1, 1, D_MODEL), lambda i, j: (i // nbt, 0, 0))],
        out_specs=pl.BlockSpec((tm, D_MODEL), lambda i, j: (i, 0)),
        scratch_shapes=[pltpu.VMEM((tm, D_MODEL), F32)] if nk > 1 else [],
        compiler_params=_cparams(("parallel", "arbitrary")),
        name="resid",
    )(a, w, x, gate)


def _resid_norm_kernel(a_ref, w_ref, x_ref, gate_ref, gn_ref, o_ref, acc_ref, *, nk):
    k = pl.program_id(1)
    part = jnp.dot(a_ref[...], w_ref[...], preferred_element_type=F32)

    @pl.when(k == 0)
    def _():
        acc_ref[...] = part

    @pl.when(k > 0)
    def _():
        acc_ref[...] += part

    @pl.when(k == nk - 1)
    def _():
        gate = gate_ref[0]
        gn = gn_ref[...]

        def body(i, carry):
            rows = pl.ds(pl.multiple_of(i * NORM_CHUNK, NORM_CHUNK), NORM_CHUNK)
            y = x_ref[rows, :] + gate * acc_ref[rows, :]
            o_ref[rows, :] = y * lax.rsqrt(jnp.mean(y * y, axis=-1, keepdims=True) + EPS) * gn
            return carry

        lax.fori_loop(0, o_ref.shape[0] // NORM_CHUNK, body, 0, unroll=NORM_UNROLL)


def _resid_norm(a, w, x, gate, gn, *, layer, seq, tm, tk):
    t, k = a.shape
    nbt = seq // tm
    nk = k // tk
    return pl.pallas_call(
        functools.partial(_resid_norm_kernel, nk=nk),
        out_shape=jax.ShapeDtypeStruct((t, D_MODEL), F32),
        grid=(t // tm, nk),
        in_specs=[pl.BlockSpec((tm, tk), lambda i, j: (i, j)),
                  pl.BlockSpec((None, tk, D_MODEL), lambda i, j: (layer, j, 0)),
                  pl.BlockSpec((tm, D_MODEL), lambda i, j: (i, 0)),
                  pl.BlockSpec((1, 1, D_MODEL), lambda i, j: (i // nbt, 0, 0)),
                  pl.BlockSpec((1, D_MODEL), lambda i, j: (0, 0))],
        out_specs=pl.BlockSpec((tm, D_MODEL), lambda i, j: (i, 0)),
        scratch_shapes=[pltpu.VMEM((tm, D_MODEL), F32)],
        compiler_params=_cparams(("parallel", "arbitrary")),
        name="resid_norm",
    )(a, w, x, gate, gn)


def _ffn_in_kernel(x_ref, g_ref, sc_ref, sh_ref, wa_ref, wb_ref, o_ref, h_ref):
    @pl.when(pl.program_id(1) == 0)
    def _():
        _norm_mod_rows(x_ref, g_ref, sc_ref, sh_ref, h_ref)

    h = h_ref[...]
    a = jnp.dot(h, wa_ref[...], preferred_element_type=F32)
    b = jnp.dot(h, wb_ref[...], preferred_element_type=F32)
    o_ref[...] = (a * _sigmoid(a) * b).astype(BF16)


def _ffn_in(x, g, sc, sh, w, *, layer, seq, tm, tn):
    t = x.shape[0]
    nbt = seq // tm
    nct = FFN_HIDDEN // tn
    return pl.pallas_call(
        _ffn_in_kernel,
        out_shape=jax.ShapeDtypeStruct((t, FFN_HIDDEN), BF16),
        grid=(t // tm, nct),
        in_specs=[pl.BlockSpec((tm, D_MODEL), lambda i, j: (i, 0)),
                  pl.BlockSpec((1, D_MODEL), lambda i, j: (0, 0)),
                  pl.BlockSpec((1, 1, D_MODEL), lambda i, j: (i // nbt, 0, 0)),
                  pl.BlockSpec((1, 1, D_MODEL), lambda i, j: (i // nbt, 0, 0)),
                  pl.BlockSpec((None, D_MODEL, tn), lambda i, j: (layer, 0, j)),
                  pl.BlockSpec((None, D_MODEL, tn), lambda i, j: (layer, 0, nct + j))],
        out_specs=pl.BlockSpec((tm, tn), lambda i, j: (i, j)),
        scratch_shapes=[pltpu.VMEM((tm, D_MODEL), BF16)],
        compiler_params=_cparams(("parallel", "arbitrary")),
        name="ffn_in",
    )(x, g, sc, sh, w, w)


def _rope_tables(seq):
    pos = jnp.arange(seq)
    row = (pos // GRID_W).astype(F32)
    col = (pos % GRID_W).astype(F32)
    inv = ROPE_THETA ** (-jnp.arange(0, ROT_AXIS, 2, dtype=F32) / ROT_AXIS)
    ar = row[:, None] * inv
    ac = col[:, None] * inv
    ang = jnp.concatenate([ar, ac, ar, ac], axis=-1)
    first = jnp.arange(HEAD_DIM) < HEAD_DIM // 2
    return jnp.cos(ang), jnp.where(first, -jnp.sin(ang), jnp.sin(ang))


def _pair_split(v):
    quarter = HEAD_DIM // 4
    q = v.reshape(v.shape[:-1] + (v.shape[-1] // HEAD_DIM, 4, quarter))
    q = jnp.concatenate([q[..., 0:1, :], q[..., 2:3, :], q[..., 1:2, :], q[..., 3:4, :]], axis=-2)
    return q.reshape(v.shape)


def _pair_split_heads(w):
    return jnp.concatenate([w[..., :OFF_BQ], _pair_split(w[..., OFF_BQ:OFF_BV]), w[..., OFF_BV:OFF_DQ],
                            _pair_split(w[..., OFF_DQ:OFF_DV]), w[..., OFF_DV:]], axis=-1)


def _tile(n, pref):
    return pref if n % pref == 0 else n


def _forward(x, c, ctx, c_ctx, w_ada, b_ada, norm_mix, norm_ffn, w_in, b_gate, conv_a_w, sink_b,
             qk_norm_q, qk_norm_k, conv_c_w, conv_c_b, ln_c_g, ln_c_b, w_branch, w_out,
             w_ffn_in, w_ffn_out, norm_final):
    batch, seq, _ = x.shape
    n_ctx = ctx.shape[1]
    depth = w_in.shape[0]
    xs = x.reshape(batch * seq, D_MODEL)
    xc = ctx.reshape(batch * n_ctx, D_MODEL)

    tm = _tile(seq, 1024)
    tm_rows = _tile(seq, 512)
    tm_conv_a = _tile(seq, 512)
    tm_conv_c = _tile(seq, 256)
    tq_win = _tile(seq, 512)
    tq = _tile(seq, 1024)
    tk = _tile(seq, 4096)

    rope_x = _rope_tables(seq)
    rope_c = (jnp.ones((n_ctx, HEAD_DIM), F32), jnp.zeros((n_ctx, HEAD_DIM), F32))
    cvec = jnp.zeros((ADA_ROWS, D_MODEL), F32).at[:batch].set(c).at[batch].set(c_ctx)

    def row(v):
        return v.reshape(1, -1)

    w_proj = w_gate = _pair_split_heads(w_in).astype(BF16)
    w_br = w_branch.astype(BF16)
    w_o = w_out.astype(BF16)
    w_f1 = w_ffn_in.astype(BF16)
    w_f2 = w_ffn_out.astype(BF16)

    def mixers(pf, pq, vt, pq_c, vt_c, l, *, s, tma, tmc, is_ctx):
        y_a = _conv_a(pf, conv_a_w[l], seq=s, tm=tma)
        y_c = _conv_c(pf, conv_c_w[l], row(conv_c_b[l]), row(ln_c_g[l]), row(ln_c_b[l]), seq=s, tm=tmc)
        off_d = Q_W + 2 * KV_W
        if is_ctx:
            y_b = _flash(pq, pq, vt, None, None, sink_b[l], batch=batch, seq_q=s, seq_k=s, seq_x=0, tq=s, tk=s,
                         q_off=0, k_off=Q_W, vt_row=0, use_sink=True)
            y_d = _flash(pq, pq, vt, None, None, sink_b[l], batch=batch, seq_q=s, seq_k=s, seq_x=0, tq=s, tk=s,
                         q_off=off_d, k_off=off_d + Q_W, vt_row=N_KV, use_sink=False)
        else:
            y_b = _win_attn(pq, vt, pq_c, vt_c, sink_b[l], batch=batch, seq=s, ctx=n_ctx, tq=tq_win)
            y_d = _flash(pq, pq, vt, pq_c, vt_c, sink_b[l], batch=batch, seq_q=s, seq_k=s, seq_x=n_ctx,
                         tq=tq, tk=tk, q_off=off_d, k_off=off_d + Q_W, vt_row=N_KV, use_sink=False)
        return [y_a, y_b, y_c, y_d]

    for l in range(depth):
        last = l == depth - 1
        mod = _ada(cvec, w_ada, row(b_ada[l]), layer=l)
        mx = mod[:batch].reshape(batch, 1, 6, D_MODEL)
        mcx = jnp.broadcast_to(mod[batch].reshape(1, 1, 6, D_MODEL), (batch, 1, 6, D_MODEL))
        sh_m, sc_m, g_m, sh_f, sc_f, g_f = (mx[:, :, k] for k in range(6))
        csh_m, csc_m, cg_m, csh_f, csc_f, cg_f = (mcx[:, :, k] for k in range(6))
        gq, gk = row(_pair_split(qk_norm_q[l])), row(_pair_split(qk_norm_k[l]))

        pf_c, pq_c, h_c, vt_c = _inproj(xc, row(norm_mix[l]), csc_m, csh_m, w_proj, *rope_c, gq, gk,
                                         layer=l, seq=n_ctx, tm=n_ctx)
        pf, pq, h, vt = _inproj(xs, row(norm_mix[l]), sc_m, sh_m, w_proj, *rope_x, gq, gk,
                                 layer=l, seq=seq, tm=tm)

        ys = mixers(pf, pq, vt, pq_c, vt_c, l, s=seq, tma=tm_conv_a, tmc=tm_conv_c, is_ctx=False)
        m = _merge(h, ys, w_gate, w_br, b_gate[l], layer=l, tm=tm, tn=512)
        xs = _resid(m, w_o, xs, g_m, layer=l, seq=seq, tm=tm_rows, tk=D_MODEL)

        if not last:
            ys_c = mixers(pf_c, pq_c, vt_c, None, None, l, s=n_ctx, tma=n_ctx, tmc=n_ctx, is_ctx=True)
            m_c = _merge(h_c, ys_c, w_gate, w_br, b_gate[l], layer=l, tm=n_ctx, tn=512)
            xc = _resid(m_c, w_o, xc, cg_m, layer=l, seq=n_ctx, tm=n_ctx, tk=D_MODEL)
            hid_c = _ffn_in(xc, row(norm_ffn[l]), csc_f, csh_f, w_f1, layer=l, seq=n_ctx, tm=n_ctx, tn=512)
            xc = _resid(hid_c, w_f2, xc, cg_f, layer=l, seq=n_ctx, tm=n_ctx, tk=FFN_HIDDEN // 2)

        hid = _ffn_in(xs, row(norm_ffn[l]), sc_f, sh_f, w_f1, layer=l, seq=seq, tm=tm, tn=512)
        if last:
            xs = _resid_norm(hid, w_f2, xs, g_f, row(norm_final), layer=l, seq=seq, tm=tm_rows,
                             tk=FFN_HIDDEN // 2)
        else:
            xs = _resid(hid, w_f2, xs, g_f, layer=l, seq=seq, tm=tm_rows, tk=FFN_HIDDEN // 2)

    return xs.reshape(batch, seq, D_MODEL)


def kernel(x, c, ctx, c_ctx, w_ada, b_ada, norm_mix, norm_ffn, w_in, b_gate, conv_a_w, sink_b, qk_norm_q,
           qk_norm_k, conv_c_w, conv_c_b, ln_c_g, ln_c_b, w_branch, w_out, w_ffn_in, w_ffn_out, norm_final):
    return _forward(x, c, ctx, c_ctx, w_ada, b_ada, norm_mix, norm_ffn, w_in, b_gate, conv_a_w, sink_b,
                    qk_norm_q, qk_norm_k, conv_c_w, conv_c_b, ln_c_g, ln_c_b, w_branch, w_out,
                    w_ffn_in, w_ffn_out, norm_final)
```

```python
import functools

import jax
import jax.numpy as jnp
from jax import lax
from jax.experimental import pallas as pl
from jax.experimental.pallas import tpu as pltpu

F32 = jnp.float32
BF16 = jnp.bfloat16

D_MODEL = 2048
N_BRANCH = 4
W_BR = D_MODEL // 4
HEAD_DIM = 128
N_Q = W_BR // HEAD_DIM
N_KV = N_Q // 2
GROUP = N_Q // N_KV
Q_W = N_Q * HEAD_DIM
KV_W = N_KV * HEAD_DIM
CONV_A = 3
CONV_C = 31
WINDOW = 128
GRID_W = 64
ROT_AXIS = HEAD_DIM // 2
ROPE_THETA = 10000.0
FFN_HIDDEN = 5632
EPS = 1e-6
ATTN_SCALE = HEAD_DIM ** -0.5
LOG2E = 1.4426950408889634
Q_SCALE = ATTN_SCALE * LOG2E
NEG_BIG = -1e30

OFF_A = 0
OFF_BQ = OFF_A + 3 * W_BR
OFF_BK = OFF_BQ + Q_W
OFF_BV = OFF_BK + KV_W
OFF_C = OFF_BV + KV_W
OFF_DQ = OFF_C + 2 * W_BR
OFF_DK = OFF_DQ + Q_W
OFF_DV = OFF_DK + KV_W
OFF_G = OFF_DV + KV_W

PF_W = 3 * W_BR + 2 * W_BR
PQ_W = 2 * (Q_W + 2 * KV_W)
TN = 512
N_PF_TILES = PF_W // TN
N_PROJ_TILES = (PF_W + PQ_W) // TN

VMEM_LIMIT = 56 * 1024 * 1024


def _cparams(sem):
    return pltpu.CompilerParams(dimension_semantics=sem, vmem_limit_bytes=VMEM_LIMIT)


def _sigmoid(z):
    return 1.0 / (1.0 + jnp.exp(-z))


def _ada_kernel(c_ref, w_ref, b_ref, o_ref):
    c = c_ref[...]
    s = (c * _sigmoid(c)).astype(BF16)
    o_ref[...] = jnp.dot(s, w_ref[...].astype(BF16), preferred_element_type=F32) + b_ref[...]


ADA_ROWS = 16


def _ada(cvec, w, b, *, layer):
    n = w.shape[2]
    tn = 1024
    return pl.pallas_call(
        _ada_kernel,
        out_shape=jax.ShapeDtypeStruct((ADA_ROWS, n), F32),
        grid=(n // tn,),
        in_specs=[pl.BlockSpec((ADA_ROWS, D_MODEL), lambda j: (0, 0)),
                  pl.BlockSpec((None, D_MODEL, tn), lambda j: (layer, 0, j)),
                  pl.BlockSpec((1, tn), lambda j: (0, j))],
        out_specs=pl.BlockSpec((ADA_ROWS, tn), lambda j: (0, j)),
        compiler_params=_cparams(("arbitrary",)),
        name="ada",
    )(cvec, w, b)


def _rope(xh, cos, sin_signed):
    return xh * cos + pltpu.roll(xh, HEAD_DIM // 2, 1) * sin_signed


def _head_norm(xh, g):
    return xh * lax.rsqrt(jnp.mean(xh * xh, axis=-1, keepdims=True) + EPS) * g


NORM_CHUNK = 16
NORM_UNROLL = 8


def _norm_mod_rows(x_ref, g_ref, sc_ref, sh_ref, h_ref):
    gs = g_ref[...] * (1.0 + sc_ref[0])
    sh = sh_ref[0]

    def body(i, carry):
        rows = pl.ds(pl.multiple_of(i * NORM_CHUNK, NORM_CHUNK), NORM_CHUNK)
        x = x_ref[rows, :]
        r = lax.rsqrt(jnp.mean(x * x, axis=-1, keepdims=True) + EPS)
        h_ref[rows, :] = (x * r * gs + sh).astype(BF16)
        return carry

    lax.fori_loop(0, x_ref.shape[0] // NORM_CHUNK, body, 0, unroll=NORM_UNROLL)


def _inproj_kernel(x_ref, g_ref, sc_ref, sh_ref, w_ref, cos_ref, sin_ref, gq_ref, gk_ref,
                   pf_ref, q_ref, h_ref, k_ref, vt_ref):
    j = pl.program_id(1)

    @pl.when(j == 0)
    def _():
        _norm_mod_rows(x_ref, g_ref, sc_ref, sh_ref, h_ref)

    acc = jnp.dot(h_ref[...], w_ref[...], preferred_element_type=F32)

    @pl.when(j < N_PF_TILES)
    def _():
        pf_ref[...] = acc

    def rope(xh):
        return _rope(xh, cos_ref[...], sin_ref[...])

    def head(k):
        return acc[:, k * HEAD_DIM:(k + 1) * HEAD_DIM]

    def put_q(k, v):
        q_ref[:, k * HEAD_DIM:(k + 1) * HEAD_DIM] = v.astype(BF16)

    @pl.when(j == N_PF_TILES)
    def _():
        for k in range(N_Q):
            put_q(k, rope(head(k)) * Q_SCALE)

    @pl.when(j == N_PF_TILES + 1)
    def _():
        for k in range(N_KV):
            k_ref[k] = rope(head(k)).astype(BF16)
        vt_ref[0:KV_W, :] = acc[:, KV_W:].T.astype(BF16)

    @pl.when(j == N_PF_TILES + 2)
    def _():
        for k in range(N_Q):
            put_q(k, rope(_head_norm(head(k), gq_ref[...])) * Q_SCALE)

    @pl.when(j == N_PF_TILES + 3)
    def _():
        for k in range(N_KV):
            k_ref[N_KV + k] = rope(_head_norm(head(k), gk_ref[...])).astype(BF16)
        vt_ref[KV_W:, :] = acc[:, KV_W:].T.astype(BF16)


def _proj_block(j):
    n_a, n_b, n_c = 3 * W_BR // TN, (Q_W + 2 * KV_W) // TN, 2 * W_BR // TN
    return jnp.where(j < n_a, j, jnp.where(j < n_a + n_c, j + n_b, jnp.where(j < n_a + n_c + n_b, j - n_c, j)))


def _inproj(x, g, sc, sh, w, cos, sin_signed, gq, gk, *, layer, seq, tm):
    t = x.shape[0]
    nbt = seq // tm
    return pl.pallas_call(
        _inproj_kernel,
        out_shape=(jax.ShapeDtypeStruct((t, PF_W), F32),
                   jax.ShapeDtypeStruct((t, 2 * Q_W), BF16),
                   jax.ShapeDtypeStruct((t, D_MODEL), BF16),
                   jax.ShapeDtypeStruct((2 * N_KV, t, HEAD_DIM), BF16),
                   jax.ShapeDtypeStruct((2 * KV_W, t), BF16)),
        grid=(t // tm, N_PROJ_TILES),
        in_specs=[pl.BlockSpec((tm, D_MODEL), lambda i, j: (i, 0)),
                  pl.BlockSpec((1, D_MODEL), lambda i, j: (0, 0)),
                  pl.BlockSpec((1, 1, D_MODEL), lambda i, j: (i // nbt, 0, 0)),
                  pl.BlockSpec((1, 1, D_MODEL), lambda i, j: (i // nbt, 0, 0)),
                  pl.BlockSpec((None, D_MODEL, TN), lambda i, j: (layer, 0, _proj_block(j))),
                  pl.BlockSpec((tm, HEAD_DIM), lambda i, j: (i % nbt, 0)),
                  pl.BlockSpec((tm, HEAD_DIM), lambda i, j: (i % nbt, 0)),
                  pl.BlockSpec((1, HEAD_DIM), lambda i, j: (0, 0)),
                  pl.BlockSpec((1, HEAD_DIM), lambda i, j: (0, 0))],
        out_specs=(pl.BlockSpec((tm, TN), lambda i, j: (i, jnp.minimum(j, N_PF_TILES - 1))),
                   pl.BlockSpec((tm, Q_W), lambda i, j: (i, jnp.where(j >= N_PF_TILES + 2, 1, 0))),
                   pl.BlockSpec((tm, D_MODEL), lambda i, j: (i, 0)),
                   pl.BlockSpec((2 * N_KV, tm, HEAD_DIM), lambda i, j: (0, i, 0)),
                   pl.BlockSpec((2 * KV_W, tm), lambda i, j: (0, i))),
        compiler_params=_cparams(("parallel", "arbitrary")),
        name="inproj",
    )(x, g, sc, sh, w, cos, sin_signed, gq, gk)


HALO_A = 8


def _conv_a_kernel(bg_ref, cg_ref, h_ref, cgp_ref, hp_ref, cgn_ref, hn_ref, w_ref, o_ref, u_ref, *, nbt):
    it = pl.program_id(0) % nbt
    tm = bg_ref.shape[0]
    u_ref[HALO_A:HALO_A + tm, :] = cg_ref[...] * h_ref[...]
    u_ref[0:HALO_A, :] = jnp.where(it > 0, cgp_ref[...] * hp_ref[...], 0.0)
    u_ref[HALO_A + tm:, :] = jnp.where(it < nbt - 1, cgn_ref[...] * hn_ref[...], 0.0)
    y = (w_ref[0:1, :] * u_ref[HALO_A - 1:HALO_A - 1 + tm, :]
         + w_ref[1:2, :] * u_ref[HALO_A:HALO_A + tm, :]
         + w_ref[2:3, :] * u_ref[HALO_A + 1:HALO_A + 1 + tm, :])
    o_ref[...] = (bg_ref[...] * y).astype(BF16)


def _conv_a(pf, w, *, seq, tm):
    t = pf.shape[0]
    nbt = seq // tm
    r = tm // HALO_A
    last = t // HALO_A - 1

    def cur(c):
        return pl.BlockSpec((tm, W_BR), lambda i: (i, c))

    def prev(c):
        return pl.BlockSpec((HALO_A, W_BR), lambda i: (jnp.maximum(i * r - 1, 0), c))

    def nxt(c):
        return pl.BlockSpec((HALO_A, W_BR), lambda i: (jnp.minimum((i + 1) * r, last), c))

    return pl.pallas_call(
        functools.partial(_conv_a_kernel, nbt=nbt),
        out_shape=jax.ShapeDtypeStruct((t, W_BR), BF16),
        grid=(t // tm,),
        in_specs=[cur(0), cur(1), cur(2), prev(1), prev(2), nxt(1), nxt(2),
                  pl.BlockSpec((CONV_A, W_BR), lambda i: (0, 0))],
        out_specs=pl.BlockSpec((tm, W_BR), lambda i: (i, 0)),
        scratch_shapes=[pltpu.VMEM((tm + 2 * HALO_A, W_BR), F32)],
        compiler_params=_cparams(("parallel",)),
        name="conv_a",
    )(pf, pf, pf, pf, pf, pf, pf, w)


HALO_C = 16
ROWS_C = 64
SUBLANES = 8


def _conv_c_kernel(v_ref, g_ref, vp_ref, gp_ref, vn_ref, gn_ref, w_ref, b_ref, lg_ref, lb_ref,
                   o_ref, u_ref, us_ref, *, nbt):
    it = pl.program_id(0) % nbt
    tm = v_ref.shape[0]
    u_ref[HALO_C:HALO_C + tm, :] = v_ref[...] * _sigmoid(g_ref[...])
    u_ref[0:HALO_C, :] = jnp.where(it > 0, vp_ref[...] * _sigmoid(gp_ref[...]), 0.0)
    u_ref[HALO_C + tm:, :] = jnp.where(it < nbt - 1, vn_ref[...] * _sigmoid(gn_ref[...]), 0.0)
    n_sh = us_ref.shape[1]
    for r in range(1, SUBLANES):
        us_ref[r - 1] = u_ref[r:r + n_sh, :]
    pad = CONV_C // 2

    def tap(row):
        a, r = divmod(row, SUBLANES)
        if r == 0:
            return u_ref[a * SUBLANES:a * SUBLANES + ROWS_C, :]
        return us_ref[r - 1, a * SUBLANES:a * SUBLANES + ROWS_C, :]

    for c in range(tm // ROWS_C):
        base = HALO_C + c * ROWS_C - pad
        acc = w_ref[0:1, :] * tap(base)
        for k in range(1, CONV_C):
            acc = acc + w_ref[k:k + 1, :] * tap(base + k)
        acc = acc + b_ref[...]
        mu = jnp.mean(acc, axis=-1, keepdims=True)
        xc = acc - mu
        var = jnp.mean(xc * xc, axis=-1, keepdims=True)
        y = xc * lax.rsqrt(var + EPS) * lg_ref[...] + lb_ref[...]
        o_ref[c * ROWS_C:(c + 1) * ROWS_C, :] = (y * _sigmoid(y)).astype(BF16)


def _conv_c(pf, w, b, lg, lb, *, seq, tm):
    t = pf.shape[0]
    nbt = seq // tm
    r = tm // HALO_C
    last = t // HALO_C - 1
    c0 = 3

    def cur(c):
        return pl.BlockSpec((tm, W_BR), lambda i: (i, c))

    def prev(c):
        return pl.BlockSpec((HALO_C, W_BR), lambda i: (jnp.maximum(i * r - 1, 0), c))

    def nxt(c):
        return pl.BlockSpec((HALO_C, W_BR), lambda i: (jnp.minimum((i + 1) * r, last), c))

    def vec():
        return pl.BlockSpec((1, W_BR), lambda i: (0, 0))

    return pl.pallas_call(
        functools.partial(_conv_c_kernel, nbt=nbt),
        out_shape=jax.ShapeDtypeStruct((t, W_BR), BF16),
        grid=(t // tm,),
        in_specs=[cur(c0), cur(c0 + 1), prev(c0), prev(c0 + 1), nxt(c0), nxt(c0 + 1),
                  pl.BlockSpec((CONV_C, W_BR), lambda i: (0, 0)), vec(), vec(), vec()],
        out_specs=pl.BlockSpec((tm, W_BR), lambda i: (i, 0)),
        scratch_shapes=[pltpu.VMEM((tm + 2 * HALO_C, W_BR), F32),
                        pltpu.VMEM((SUBLANES - 1, tm + 2 * HALO_C - SUBLANES, W_BR), F32)],
        compiler_params=_cparams(("parallel",)),
        name="conv_c",
    )(pf, pf, pf, pf, pf, pf, w, b, lg, lb)


def _dot_nt(a, b):
    return lax.dot_general(a, b, (((1,), (1,)), ((), ())), preferred_element_type=F32)


def _win_kernel(sink_ref, q_ref, kp_ref, kc_ref, kn_ref, kx_ref, vp_ref, vc_ref, vn_ref, vx_ref, o_ref, vext_ref,
                *, nq):
    hk = pl.program_id(1)
    i = pl.program_id(2)
    tq = q_ref.shape[0]
    n_loc = tq + 2 * WINDOW
    kk = jnp.concatenate([kp_ref[...], kc_ref[...], kn_ref[...], kx_ref[...]], axis=0)
    n_all = kk.shape[0]
    vext_ref[0:HEAD_DIM, 0:WINDOW] = vp_ref[...]
    vext_ref[0:HEAD_DIM, WINDOW:WINDOW + tq] = vc_ref[...]
    vext_ref[0:HEAD_DIM, WINDOW + tq:n_loc] = vn_ref[...]
    vext_ref[0:HEAD_DIM, n_loc:] = vx_ref[...]
    vext_ref[HEAD_DIM:, :] = jnp.ones((ONES_ROWS, n_all), BF16)
    key = lax.broadcasted_iota(jnp.int32, (n_all, tq), 0)
    qry = lax.broadcasted_iota(jnp.int32, (n_all, tq), 1)
    rel = key - WINDOW - qry
    lo = jnp.where(i > 0, 0, WINDOW)
    hi = jnp.where(i < nq - 1, n_loc, tq + WINDOW)
    valid = (key >= n_loc) | ((jnp.abs(rel) <= WINDOW) & (key >= lo) & (key < hi))

    def scores(g):
        return _dot_nt(kk, q_ref[:, g * HEAD_DIM:(g + 1) * HEAD_DIM])

    ahead = [scores(0)]
    outs = []
    for g in range(GROUP):
        s = jnp.where(valid, ahead.pop(0), NEG_BIG)
        if g + 1 < GROUP:
            ahead.append(scores(g + 1))
        sink = sink_ref[hk * GROUP + g] * LOG2E
        m = jnp.maximum(jnp.max(s, axis=0, keepdims=True), sink)
        p = jnp.exp2(s - m).astype(BF16)
        acc = jnp.dot(vext_ref[...], p, preferred_element_type=F32)
        l = acc[HEAD_DIM:HEAD_DIM + 1, :] + jnp.exp2(sink - m)
        outs.append((acc[0:HEAD_DIM, :] / l).T.astype(BF16))
    o_ref[...] = jnp.concatenate(outs, axis=1)


def _win_attn(q, k, vt, k_c, vt_c, sink, *, batch, seq, ctx, tq):
    nq = seq // tq
    r = tq // WINDOW
    nblk = seq // WINDOW

    def prev_blk(b, i):
        return b * nblk + jnp.maximum(i * r - 1, 0)

    def next_blk(b, i):
        return b * nblk + jnp.minimum((i + 1) * r, nblk - 1)

    return pl.pallas_call(
        functools.partial(_win_kernel, nq=nq),
        out_shape=jax.ShapeDtypeStruct((batch * seq, Q_W), BF16),
        grid=(batch, N_KV, nq),
        in_specs=[pl.BlockSpec(memory_space=pltpu.SMEM),
                  pl.BlockSpec((tq, GROUP * HEAD_DIM), lambda b, h, i: (b * nq + i, h)),
                  pl.BlockSpec((None, WINDOW, HEAD_DIM), lambda b, h, i: (h, prev_blk(b, i), 0)),
                  pl.BlockSpec((None, tq, HEAD_DIM), lambda b, h, i: (h, b * nq + i, 0)),
                  pl.BlockSpec((None, WINDOW, HEAD_DIM), lambda b, h, i: (h, next_blk(b, i), 0)),
                  pl.BlockSpec((None, ctx, HEAD_DIM), lambda b, h, i: (h, b, 0)),
                  pl.BlockSpec((HEAD_DIM, WINDOW), lambda b, h, i: (h, prev_blk(b, i))),
                  pl.BlockSpec((HEAD_DIM, tq), lambda b, h, i: (h, b * nq + i)),
                  pl.BlockSpec((HEAD_DIM, WINDOW), lambda b, h, i: (h, next_blk(b, i))),
                  pl.BlockSpec((HEAD_DIM, ctx), lambda b, h, i: (h, b))],
        out_specs=pl.BlockSpec((tq, GROUP * HEAD_DIM), lambda b, h, i: (b * nq + i, h)),
        scratch_shapes=[pltpu.VMEM((HEAD_DIM + ONES_ROWS, tq + 2 * WINDOW + ctx), BF16)],
        compiler_params=_cparams(("parallel", "parallel", "parallel")),
        name="win_attn",
    )(sink, q, k, k, k, k_c, vt, vt, vt, vt_c)


ONES_ROWS = 16
Q_CHUNK = 512
K_SUB = 1024
QK_AHEAD = 2


def _flash_kernel(sink_ref, q_ref, k_ref, vt_ref, *rest, n_main, has_extra, use_sink):
    if has_extra:
        kx_ref, vtx_ref, o_ref, q2_ref, m_ref, acc_ref, vext_ref, vextx_ref = rest
    else:
        o_ref, q2_ref, m_ref, acc_ref, vext_ref = rest
    hk = pl.program_id(1)
    j = pl.program_id(3)
    tq = q_ref.shape[0]

    @pl.when(j == 0)
    def _():
        for g in range(GROUP):
            q2_ref[g * tq:(g + 1) * tq, :] = q_ref[:, g * HEAD_DIM:(g + 1) * HEAD_DIM]
            if use_sink:
                m_ref[:, g * tq:(g + 1) * tq] = jnp.full((1, tq), sink_ref[hk * GROUP + g] * LOG2E, F32)
        if not use_sink:
            m_ref[...] = jnp.full(m_ref.shape, NEG_BIG, F32)
        acc_ref[0:HEAD_DIM, :] = jnp.zeros((HEAD_DIM, GROUP * tq), F32)
        acc_ref[HEAD_DIM:, :] = jnp.full((ONES_ROWS, GROUP * tq), 1.0 if use_sink else 0.0, F32)
        vext_ref[HEAD_DIM:, :] = jnp.ones((ONES_ROWS, vext_ref.shape[1]), BF16)
        if has_extra:
            vextx_ref[HEAD_DIM:, :] = jnp.ones((ONES_ROWS, vextx_ref.shape[1]), BF16)

    n_chunks = GROUP * tq // Q_CHUNK

    def step(kk_ref, vv_ref, vext):
        tkk = kk_ref.shape[0]
        kb = min(tkk, K_SUB)
        units = [(b, c) for b in range(tkk // kb) for c in range(n_chunks)]
        vext[0:HEAD_DIM, :] = vv_ref[...]

        def scores(u):
            b, c = u
            return _dot_nt(kk_ref[b * kb:(b + 1) * kb, :], q2_ref[c * Q_CHUNK:(c + 1) * Q_CHUNK, :])

        m = [m_ref[:, c * Q_CHUNK:(c + 1) * Q_CHUNK] for c in range(n_chunks)]
        acc = [acc_ref[:, c * Q_CHUNK:(c + 1) * Q_CHUNK] for c in range(n_chunks)]
        ahead = [scores(u) for u in units[:QK_AHEAD]]
        for idx, (b, c) in enumerate(units):
            s = ahead.pop(0)
            if idx + QK_AHEAD < len(units):
                ahead.append(scores(units[idx + QK_AHEAD]))
            m_next = jnp.maximum(m[c], jnp.max(s, axis=0, keepdims=True))
            alpha = jnp.exp2(m[c] - m_next)
            p = jnp.exp2(s - m_next).astype(BF16)
            acc[c] = alpha * acc[c] + jnp.dot(vext[:, b * kb:(b + 1) * kb], p, preferred_element_type=F32)
            m[c] = m_next
        acc_ref[...] = jnp.concatenate(acc, axis=1)
        m_ref[...] = jnp.concatenate(m, axis=1)

    @pl.when(j < n_main)
    def _():
        step(k_ref, vt_ref, vext_ref)

    if has_extra:
        @pl.when(j == n_main)
        def _():
            step(kx_ref, vtx_ref, vextx_ref)

    @pl.when(j == n_main + (1 if has_extra else 0) - 1)
    def _():
        o = (acc_ref[0:HEAD_DIM, :] / acc_ref[HEAD_DIM:HEAD_DIM + 1, :]).T
        for g in range(GROUP):
            o_ref[:, g * HEAD_DIM:(g + 1) * HEAD_DIM] = o[g * tq:(g + 1) * tq, :].astype(BF16)


def _flash(q_arr, k_arr, vt_arr, kx_arr, vtx_arr, sink, *, batch, seq_q, seq_k, seq_x, tq, tk, branch, use_sink):
    nq = seq_q // tq
    n_main = seq_k // tk
    has_extra = kx_arr is not None
    n_steps = n_main + (1 if has_extra else 0)
    qc = branch * N_KV
    kr = branch * N_KV

    def jj(j):
        return jnp.minimum(j, n_main - 1)

    in_specs = [pl.BlockSpec(memory_space=pltpu.SMEM),
                pl.BlockSpec((tq, GROUP * HEAD_DIM), lambda b, h, i, j: (b * nq + i, qc + h)),
                pl.BlockSpec((None, tk, HEAD_DIM), lambda b, h, i, j: (kr + h, b * n_main + jj(j), 0)),
                pl.BlockSpec((HEAD_DIM, tk), lambda b, h, i, j: (kr + h, b * n_main + jj(j)))]
    args = [sink, q_arr, k_arr, vt_arr]
    rows = GROUP * tq
    scratch = [pltpu.VMEM((rows, HEAD_DIM), BF16),
               pltpu.VMEM((1, rows), F32),
               pltpu.VMEM((HEAD_DIM + ONES_ROWS, rows), F32),
               pltpu.VMEM((HEAD_DIM + ONES_ROWS, tk), BF16)]
    if has_extra:
        in_specs += [pl.BlockSpec((None, seq_x, HEAD_DIM), lambda b, h, i, j: (kr + h, b, 0)),
                     pl.BlockSpec((HEAD_DIM, seq_x), lambda b, h, i, j: (kr + h, b))]
        args += [kx_arr, vtx_arr]
        scratch += [pltpu.VMEM((HEAD_DIM + ONES_ROWS, seq_x), BF16)]
    return pl.pallas_call(
        functools.partial(_flash_kernel, n_main=n_main, has_extra=has_extra, use_sink=use_sink),
        out_shape=jax.ShapeDtypeStruct((batch * seq_q, Q_W), BF16),
        grid=(batch, N_KV, nq, n_steps),
        in_specs=in_specs,
        out_specs=pl.BlockSpec((tq, GROUP * HEAD_DIM), lambda b, h, i, j: (b * nq + i, h)),
        scratch_shapes=scratch,
        compiler_params=_cparams(("parallel", "parallel", "parallel", "arbitrary")),
        name="flash",
    )(*args)


def _merge_kernel(h_ref, ya_ref, yb_ref, yc_ref, yd_ref, wga_ref, wgb_ref, wgc_ref, wgd_ref,
                  wb_ref, bg_ref, o_ref):
    h = h_ref[...]
    m = None
    for k, (y_ref, wg_ref) in enumerate(((ya_ref, wga_ref), (yb_ref, wgb_ref), (yc_ref, wgc_ref), (yd_ref, wgd_ref))):
        logits = jnp.dot(h, wg_ref[...], preferred_element_type=F32) + bg_ref[k:k + 1, :]
        t = _sigmoid(logits) * jnp.dot(y_ref[...], wb_ref[k], preferred_element_type=F32)
        m = t if m is None else m + t
    o_ref[...] = m.astype(BF16)


def _merge(h, ys, wg, wb, bg, *, layer, tm, tn):
    t = h.shape[0]
    nct = D_MODEL // tn

    def y():
        return pl.BlockSpec((tm, W_BR), lambda i, j: (i, 0))

    def g(k):
        return pl.BlockSpec((None, D_MODEL, tn), lambda i, j: (layer, 0, OFF_G // tn + k * nct + j))

    return pl.pallas_call(
        _merge_kernel,
        out_shape=jax.ShapeDtypeStruct((t, D_MODEL), BF16),
        grid=(t // tm, nct),
        in_specs=[pl.BlockSpec((tm, D_MODEL), lambda i, j: (i, 0)), y(), y(), y(), y(),
                  g(0), g(1), g(2), g(3),
                  pl.BlockSpec((None, N_BRANCH, W_BR, tn), lambda i, j: (layer, 0, 0, j)),
                  pl.BlockSpec((N_BRANCH, tn), lambda i, j: (0, j))],
        out_specs=pl.BlockSpec((tm, tn), lambda i, j: (i, j)),
        compiler_params=_cparams(("parallel", "arbitrary")),
        name="merge",
    )(h, *ys, wg, wg, wg, wg, wb, bg)


def _resid_kernel(a_ref, w_ref, x_ref, gate_ref, o_ref, *scratch, nk):
    part = jnp.dot(a_ref[...], w_ref[...], preferred_element_type=F32)
    if nk == 1:
        o_ref[...] = x_ref[...] + gate_ref[0] * part
        return
    (acc_ref,) = scratch
    k = pl.program_id(1)

    @pl.when(k == 0)
    def _():
        acc_ref[...] = part

    if nk > 2:
        @pl.when((k > 0) & (k < nk - 1))
        def _():
            acc_ref[...] += part

    @pl.when(k == nk - 1)
    def _():
        o_ref[...] = x_ref[...] + gate_ref[0] * (acc_ref[...] + part)


def _resid(a, w, x, gate, *, layer, seq, tm, tk):
    t, k = a.shape
    nbt = seq // tm
    nk = k // tk
    return pl.pallas_call(
        functools.partial(_resid_kernel, nk=nk),
        out_shape=jax.ShapeDtypeStruct((t, D_MODEL), F32),
        grid=(t // tm, nk),
        in_specs=[pl.BlockSpec((tm, tk), lambda i, j: (i, j)),
                  pl.BlockSpec((None, tk, D_MODEL), lambda i, j: (layer, j, 0)),
                  pl.BlockSpec((tm, D_MODEL), lambda i, j: (i, 0)),
                  pl.BlockSpec((1, 1, D_MODEL), lambda i, j: (i // nbt, 0, 0))],
        out_specs=pl.BlockSpec((tm, D_MODEL), lambda i, j: (i, 0)),
        scratch_shapes=[pltpu.VMEM((tm, D_MODEL), F32)] if nk > 1 else [],
        compiler_params=_cparams(("parallel", "arbitrary")),
        name="resid",
    )(a, w, x, gate)


def _resid_norm_kernel(a_ref, w_ref, x_ref, gate_ref, gn_ref, o_ref, acc_ref, *, nk):
    k = pl.program_id(1)
    part = jnp.dot(a_ref[...], w_ref[...], preferred_element_type=F32)

    @pl.when(k == 0)
    def _():
        acc_ref[...] = part

    @pl.when(k > 0)
    def _():
        acc_ref[...] += part

    @pl.when(k == nk - 1)
    def _():
        gate = gate_ref[0]
        gn = gn_ref[...]

        def body(i, carry):
            rows = pl.ds(pl.multiple_of(i * NORM_CHUNK, NORM_CHUNK), NORM_CHUNK)
            y = x_ref[rows, :] + gate * acc_ref[rows, :]
            o_ref[rows, :] = y * lax.rsqrt(jnp.mean(y * y, axis=-1, keepdims=True) + EPS) * gn
            return carry

        lax.fori_loop(0, o_ref.shape[0] // NORM_CHUNK, body, 0, unroll=NORM_UNROLL)


def _resid_norm(a, w, x, gate, gn, *, layer, seq, tm, tk):
    t, k = a.shape
    nbt = seq // tm
    nk = k // tk
    return pl.pallas_call(
        functools.partial(_resid_norm_kernel, nk=nk),
        out_shape=jax.ShapeDtypeStruct((t, D_MODEL), F32),
        grid=(t // tm, nk),
        in_specs=[pl.BlockSpec((tm, tk), lambda i, j: (i, j)),
                  pl.BlockSpec((None, tk, D_MODEL), lambda i, j: (layer, j, 0)),
                  pl.BlockSpec((tm, D_MODEL), lambda i, j: (i, 0)),
                  pl.BlockSpec((1, 1, D_MODEL), lambda i, j: (i // nbt, 0, 0)),
                  pl.BlockSpec((1, D_MODEL), lambda i, j: (0, 0))],
        out_specs=pl.BlockSpec((tm, D_MODEL), lambda i, j: (i, 0)),
        scratch_shapes=[pltpu.VMEM((tm, D_MODEL), F32)],
        compiler_params=_cparams(("parallel", "arbitrary")),
        name="resid_norm",
    )(a, w, x, gate, gn)


def _ffn_in_kernel(x_ref, g_ref, sc_ref, sh_ref, wa_ref, wb_ref, o_ref, h_ref):
    @pl.when(pl.program_id(1) == 0)
    def _():
        _norm_mod_rows(x_ref, g_ref, sc_ref, sh_ref, h_ref)

    h = h_ref[...]
    a = jnp.dot(h, wa_ref[...], preferred_element_type=F32)
    b = jnp.dot(h, wb_ref[...], preferred_element_type=F32)
    o_ref[...] = (a * _sigmoid(a) * b).astype(BF16)


def _ffn_in(x, g, sc, sh, w, *, layer, seq, tm, tn):
    t = x.shape[0]
    nbt = seq // tm
    nct = FFN_HIDDEN // tn
    return pl.pallas_call(
        _ffn_in_kernel,
        out_shape=jax.ShapeDtypeStruct((t, FFN_HIDDEN), BF16),
        grid=(t // tm, nct),
        in_specs=[pl.BlockSpec((tm, D_MODEL), lambda i, j: (i, 0)),
                  pl.BlockSpec((1, D_MODEL), lambda i, j: (0, 0)),
                  pl.BlockSpec((1, 1, D_MODEL), lambda i, j: (i // nbt, 0, 0)),
                  pl.BlockSpec((1, 1, D_MODEL), lambda i, j: (i // nbt, 0, 0)),
                  pl.BlockSpec((None, D_MODEL, tn), lambda i, j: (layer, 0, j)),
                  pl.BlockSpec((None, D_MODEL, tn), lambda i, j: (layer, 0, nct + j))],
        out_specs=pl.BlockSpec((tm, tn), lambda i, j: (i, j)),
        scratch_shapes=[pltpu.VMEM((tm, D_MODEL), BF16)],
        compiler_params=_cparams(("parallel", "arbitrary")),
        name="ffn_in",
    )(x, g, sc, sh, w, w)


def _rope_tables(seq):
    pos = jnp.arange(seq)
    row = (pos // GRID_W).astype(F32)
    col = (pos % GRID_W).astype(F32)
    inv = ROPE_THETA ** (-jnp.arange(0, ROT_AXIS, 2, dtype=F32) / ROT_AXIS)
    ar = row[:, None] * inv
    ac = col[:, None] * inv
    ang = jnp.concatenate([ar, ac, ar, ac], axis=-1)
    first = jnp.arange(HEAD_DIM) < HEAD_DIM // 2
    return jnp.cos(ang), jnp.where(first, -jnp.sin(ang), jnp.sin(ang))


def _pair_split(v):
    quarter = HEAD_DIM // 4
    q = v.reshape(v.shape[:-1] + (v.shape[-1] // HEAD_DIM, 4, quarter))
    q = jnp.concatenate([q[..., 0:1, :], q[..., 2:3, :], q[..., 1:2, :], q[..., 3:4, :]], axis=-2)
    return q.reshape(v.shape)


def _pair_split_heads(w):
    return jnp.concatenate([w[..., :OFF_BQ], _pair_split(w[..., OFF_BQ:OFF_BV]), w[..., OFF_BV:OFF_DQ],
                            _pair_split(w[..., OFF_DQ:OFF_DV]), w[..., OFF_DV:]], axis=-1)


def _tile(n, pref):
    return pref if n % pref == 0 else n


def _forward(x, c, ctx, c_ctx, w_ada, b_ada, norm_mix, norm_ffn, w_in, b_gate, conv_a_w, sink_b,
             qk_norm_q, qk_norm_k, conv_c_w, conv_c_b, ln_c_g, ln_c_b, w_branch, w_out,
             w_ffn_in, w_ffn_out, norm_final):
    batch, seq, _ = x.shape
    n_ctx = ctx.shape[1]
    depth = w_in.shape[0]
    xs = x.reshape(batch * seq, D_MODEL)
    xc = ctx.reshape(batch * n_ctx, D_MODEL)

    tm = _tile(seq, 1024)
    tm_rows = _tile(seq, 512)
    tm_conv_a = _tile(seq, 512)
    tm_conv_c = _tile(seq, 256)
    tq_win = _tile(seq, 512)
    tq = _tile(seq, 1024)
    tk = _tile(seq, 4096)

    rope_x = _rope_tables(seq)
    rope_c = (jnp.ones((n_ctx, HEAD_DIM), F32), jnp.zeros((n_ctx, HEAD_DIM), F32))
    cvec = jnp.zeros((ADA_ROWS, D_MODEL), F32).at[:batch].set(c).at[batch].set(c_ctx)

    def row(v):
        return v.reshape(1, -1)

    w_proj = w_gate = _pair_split_heads(w_in).astype(BF16)
    w_br = w_branch.astype(BF16)
    w_o = w_out.astype(BF16)
    w_f1 = w_ffn_in.astype(BF16)
    w_f2 = w_ffn_out.astype(BF16)

    def mixers(pf, q, k, vt, k_c, vt_c, l, *, s, tma, tmc, is_ctx):
        y_a = _conv_a(pf, conv_a_w[l], seq=s, tm=tma)
        y_c = _conv_c(pf, conv_c_w[l], row(conv_c_b[l]), row(ln_c_g[l]), row(ln_c_b[l]), seq=s, tm=tmc)
        if is_ctx:
            y_b = _flash(q, k, vt, None, None, sink_b[l], batch=batch, seq_q=s, seq_k=s, seq_x=0, tq=s, tk=s,
                         branch=0, use_sink=True)
            y_d = _flash(q, k, vt, None, None, sink_b[l], batch=batch, seq_q=s, seq_k=s, seq_x=0, tq=s, tk=s,
                         branch=1, use_sink=False)
        else:
            y_b = _win_attn(q, k, vt, k_c, vt_c, sink_b[l], batch=batch, seq=s, ctx=n_ctx, tq=tq_win)
            y_d = _flash(q, k, vt, k_c, vt_c, sink_b[l], batch=batch, seq_q=s, seq_k=s, seq_x=n_ctx,
                         tq=tq, tk=tk, branch=1, use_sink=False)
        return [y_a, y_b, y_c, y_d]

    for l in range(depth):
        last = l == depth - 1
        mod = _ada(cvec, w_ada, row(b_ada[l]), layer=l)
        mx = mod[:batch].reshape(batch, 1, 6, D_MODEL)
        mcx = jnp.broadcast_to(mod[batch].reshape(1, 1, 6, D_MODEL), (batch, 1, 6, D_MODEL))
        sh_m, sc_m, g_m, sh_f, sc_f, g_f = (mx[:, :, k] for k in range(6))
        csh_m, csc_m, cg_m, csh_f, csc_f, cg_f = (mcx[:, :, k] for k in range(6))
        gq, gk = row(_pair_split(qk_norm_q[l])), row(_pair_split(qk_norm_k[l]))

        pf_c, q_c, h_c, k_c, vt_c = _inproj(xc, row(norm_mix[l]), csc_m, csh_m, w_proj, *rope_c, gq, gk,
                                            layer=l, seq=n_ctx, tm=n_ctx)
        pf, q, h, k, vt = _inproj(xs, row(norm_mix[l]), sc_m, sh_m, w_proj, *rope_x, gq, gk,
                                  layer=l, seq=seq, tm=tm)

        ys = mixers(pf, q, k, vt, k_c, vt_c, l, s=seq, tma=tm_conv_a, tmc=tm_conv_c, is_ctx=False)
        m = _merge(h, ys, w_gate, w_br, b_gate[l], layer=l, tm=tm, tn=512)
        xs = _resid(m, w_o, xs, g_m, layer=l, seq=seq, tm=tm_rows, tk=D_MODEL)

        if not last:
            ys_c = mixers(pf_c, q_c, k_c, vt_c, None, None, l, s=n_ctx, tma=n_ctx, tmc=n_ctx, is_ctx=True)
            m_c = _merge(h_c, ys_c, w_gate, w_br, b_gate[l], layer=l, tm=n_ctx, tn=512)
            xc = _resid(m_c, w_o, xc, cg_m, layer=l, seq=n_ctx, tm=n_ctx, tk=D_MODEL)
            hid_c = _ffn_in(xc, row(norm_ffn[l]), csc_f, csh_f, w_f1, layer=l, seq=n_ctx, tm=n_ctx, tn=512)
            xc = _resid(hid_c, w_f2, xc, cg_f, layer=l, seq=n_ctx, tm=n_ctx, tk=FFN_HIDDEN // 2)

        hid = _ffn_in(xs, row(norm_ffn[l]), sc_f, sh_f, w_f1, layer=l, seq=seq, tm=tm, tn=512)
        if last:
            xs = _resid_norm(hid, w_f2, xs, g_f, row(norm_final), layer=l, seq=seq, tm=tm_rows,
                             tk=FFN_HIDDEN // 2)
        else:
            xs = _resid(hid, w_f2, xs, g_f, layer=l, seq=seq, tm=tm_rows, tk=FFN_HIDDEN // 2)

    return xs.reshape(batch, seq, D_MODEL)


def kernel(x, c, ctx, c_ctx, w_ada, b_ada, norm_mix, norm_ffn, w_in, b_gate, conv_a_w, sink_b, qk_norm_q,
           qk_norm_k, conv_c_w, conv_c_b, ln_c_g, ln_c_b, w_branch, w_out, w_ffn_in, w_ffn_out, norm_final):
    return _forward(x, c, ctx, c_ctx, w_ada, b_ada, norm_mix, norm_ffn, w_in, b_gate, conv_a_w, sink_b,
                    qk_norm_q, qk_norm_k, conv_c_w, conv_c_b, ln_c_g, ln_c_b, w_branch, w_out,
                    w_ffn_in, w_ffn_out, norm_final)
```

```python
import functools

import jax
import jax.numpy as jnp
from jax import lax
from jax.experimental import pallas as pl
from jax.experimental.pallas import tpu as pltpu

F32 = jnp.float32
BF16 = jnp.bfloat16

D_MODEL = 2048
N_BRANCH = 4
W_BR = D_MODEL // 4
HEAD_DIM = 128
N_Q = W_BR // HEAD_DIM
N_KV = N_Q // 2
GROUP = N_Q // N_KV
Q_W = N_Q * HEAD_DIM
KV_W = N_KV * HEAD_DIM
CONV_A = 3
CONV_C = 31
WINDOW = 128
GRID_W = 64
ROT_AXIS = HEAD_DIM // 2
ROPE_THETA = 10000.0
FFN_HIDDEN = 5632
EPS = 1e-6
ATTN_SCALE = HEAD_DIM ** -0.5
LOG2E = 1.4426950408889634
Q_SCALE = ATTN_SCALE * LOG2E
NEG_BIG = -1e30

OFF_A = 0
OFF_BQ = OFF_A + 3 * W_BR
OFF_BK = OFF_BQ + Q_W
OFF_BV = OFF_BK + KV_W
OFF_C = OFF_BV + KV_W
OFF_DQ = OFF_C + 2 * W_BR
OFF_DK = OFF_DQ + Q_W
OFF_DV = OFF_DK + KV_W
OFF_G = OFF_DV + KV_W

PF_W = 3 * W_BR + 2 * W_BR
PQ_W = 2 * (Q_W + 2 * KV_W)
TN = 512
N_PF_TILES = PF_W // TN
N_PROJ_TILES = (PF_W + PQ_W) // TN

VMEM_LIMIT = 56 * 1024 * 1024


def _cparams(sem):
    return pltpu.CompilerParams(dimension_semantics=sem, vmem_limit_bytes=VMEM_LIMIT)


def _sigmoid(z):
    return 1.0 / (1.0 + jnp.exp(-z))


def _ada_kernel(c_ref, w_ref, b_ref, o_ref):
    c = c_ref[...]
    s = (c * _sigmoid(c)).astype(BF16)
    o_ref[...] = jnp.dot(s, w_ref[...].astype(BF16), preferred_element_type=F32) + b_ref[...]


ADA_ROWS = 16


def _ada(cvec, w, b, *, layer):
    n = w.shape[2]
    tn = 1024
    return pl.pallas_call(
        _ada_kernel,
        out_shape=jax.ShapeDtypeStruct((ADA_ROWS, n), F32),
        grid=(n // tn,),
        in_specs=[pl.BlockSpec((ADA_ROWS, D_MODEL), lambda j: (0, 0)),
                  pl.BlockSpec((None, D_MODEL, tn), lambda j: (layer, 0, j)),
                  pl.BlockSpec((1, tn), lambda j: (0, j))],
        out_specs=pl.BlockSpec((ADA_ROWS, tn), lambda j: (0, j)),
        compiler_params=_cparams(("arbitrary",)),
        name="ada",
    )(cvec, w, b)


def _rope(xh, cos, sin_signed):
    return xh * cos + pltpu.roll(xh, HEAD_DIM // 2, 1) * sin_signed


def _head_norm(xh, g):
    return xh * lax.rsqrt(jnp.mean(xh * xh, axis=-1, keepdims=True) + EPS) * g


NORM_CHUNK = 16
NORM_UNROLL = 8


def _norm_mod_rows(x_ref, g_ref, sc_ref, sh_ref, h_ref):
    gs = g_ref[...] * (1.0 + sc_ref[0])
    sh = sh_ref[0]

    def body(i, carry):
        rows = pl.ds(pl.multiple_of(i * NORM_CHUNK, NORM_CHUNK), NORM_CHUNK)
        x = x_ref[rows, :]
        r = lax.rsqrt(jnp.mean(x * x, axis=-1, keepdims=True) + EPS)
        h_ref[rows, :] = (x * r * gs + sh).astype(BF16)
        return carry

    lax.fori_loop(0, x_ref.shape[0] // NORM_CHUNK, body, 0, unroll=NORM_UNROLL)


def _inproj_kernel(x_ref, g_ref, sc_ref, sh_ref, w_ref, cos_ref, sin_ref, gq_ref, gk_ref,
                   pf_ref, q_ref, h_ref, k_ref, vt_ref):
    j = pl.program_id(1)

    @pl.when(j == 0)
    def _():
        _norm_mod_rows(x_ref, g_ref, sc_ref, sh_ref, h_ref)

    acc = jnp.dot(h_ref[...], w_ref[...], preferred_element_type=F32)

    @pl.when(j < N_PF_TILES)
    def _():
        pf_ref[...] = acc

    def rope(xh):
        return _rope(xh, cos_ref[...], sin_ref[...])

    def head(k):
        return acc[:, k * HEAD_DIM:(k + 1) * HEAD_DIM]

    def put_q(k, v):
        q_ref[:, k * HEAD_DIM:(k + 1) * HEAD_DIM] = v.astype(BF16)

    @pl.when(j == N_PF_TILES)
    def _():
        for k in range(N_Q):
            put_q(k, rope(head(k)) * Q_SCALE)

    @pl.when(j == N_PF_TILES + 1)
    def _():
        for k in range(N_KV):
            k_ref[k] = rope(head(k)).astype(BF16)
        vt_ref[0:KV_W, :] = acc[:, KV_W:].T.astype(BF16)

    @pl.when(j == N_PF_TILES + 2)
    def _():
        for k in range(N_Q):
            put_q(k, rope(_head_norm(head(k), gq_ref[...])) * Q_SCALE)

    @pl.when(j == N_PF_TILES + 3)
    def _():
        for k in range(N_KV):
            k_ref[N_KV + k] = rope(_head_norm(head(k), gk_ref[...])).astype(BF16)
        vt_ref[KV_W:, :] = acc[:, KV_W:].T.astype(BF16)


def _proj_block(j):
    n_a, n_b, n_c = 3 * W_BR // TN, (Q_W + 2 * KV_W) // TN, 2 * W_BR // TN
    return jnp.where(j < n_a, j, jnp.where(j < n_a + n_c, j + n_b, jnp.where(j < n_a + n_c + n_b, j - n_c, j)))


def _inproj(x, g, sc, sh, w, cos, sin_signed, gq, gk, *, layer, seq, tm):
    t = x.shape[0]
    nbt = seq // tm
    return pl.pallas_call(
        _inproj_kernel,
        out_shape=(jax.ShapeDtypeStruct((t, PF_W), F32),
                   jax.ShapeDtypeStruct((t, 2 * Q_W), BF16),
                   jax.ShapeDtypeStruct((t, D_MODEL), BF16),
                   jax.ShapeDtypeStruct((2 * N_KV, t, HEAD_DIM), BF16),
                   jax.ShapeDtypeStruct((2 * KV_W, t), BF16)),
        grid=(t // tm, N_PROJ_TILES),
        in_specs=[pl.BlockSpec((tm, D_MODEL), lambda i, j: (i, 0)),
                  pl.BlockSpec((1, D_MODEL), lambda i, j: (0, 0)),
                  pl.BlockSpec((1, 1, D_MODEL), lambda i, j: (i // nbt, 0, 0)),
                  pl.BlockSpec((1, 1, D_MODEL), lambda i, j: (i // nbt, 0, 0)),
                  pl.BlockSpec((None, D_MODEL, TN), lambda i, j: (layer, 0, _proj_block(j))),
                  pl.BlockSpec((tm, HEAD_DIM), lambda i, j: (i % nbt, 0)),
                  pl.BlockSpec((tm, HEAD_DIM), lambda i, j: (i % nbt, 0)),
                  pl.BlockSpec((1, HEAD_DIM), lambda i, j: (0, 0)),
                  pl.BlockSpec((1, HEAD_DIM), lambda i, j: (0, 0))],
        out_specs=(pl.BlockSpec((tm, TN), lambda i, j: (i, jnp.minimum(j, N_PF_TILES - 1))),
                   pl.BlockSpec((tm, Q_W), lambda i, j: (i, jnp.where(j >= N_PF_TILES + 2, 1, 0))),
                   pl.BlockSpec((tm, D_MODEL), lambda i, j: (i, 0)),
                   pl.BlockSpec((2 * N_KV, tm, HEAD_DIM), lambda i, j: (0, i, 0)),
                   pl.BlockSpec((2 * KV_W, tm), lambda i, j: (0, i))),
        compiler_params=_cparams(("parallel", "arbitrary")),
        name="inproj",
    )(x, g, sc, sh, w, cos, sin_signed, gq, gk)


HALO_A = 8


def _conv_a_kernel(bg_ref, cg_ref, h_ref, cgp_ref, hp_ref, cgn_ref, hn_ref, w_ref, o_ref, u_ref, *, nbt):
    it = pl.program_id(0) % nbt
    tm = bg_ref.shape[0]
    u_ref[HALO_A:HALO_A + tm, :] = cg_ref[...] * h_ref[...]
    u_ref[0:HALO_A, :] = jnp.where(it > 0, cgp_ref[...] * hp_ref[...], 0.0)
    u_ref[HALO_A + tm:, :] = jnp.where(it < nbt - 1, cgn_ref[...] * hn_ref[...], 0.0)
    y = (w_ref[0:1, :] * u_ref[HALO_A - 1:HALO_A - 1 + tm, :]
         + w_ref[1:2, :] * u_ref[HALO_A:HALO_A + tm, :]
         + w_ref[2:3, :] * u_ref[HALO_A + 1:HALO_A + 1 + tm, :])
    o_ref[...] = (bg_ref[...] * y).astype(BF16)


def _conv_a(pf, w, *, seq, tm):
    t = pf.shape[0]
    nbt = seq // tm
    r = tm // HALO_A
    last = t // HALO_A - 1

    def cur(c):
        return pl.BlockSpec((tm, W_BR), lambda i: (i, c))

    def prev(c):
        return pl.BlockSpec((HALO_A, W_BR), lambda i: (jnp.maximum(i * r - 1, 0), c))

    def nxt(c):
        return pl.BlockSpec((HALO_A, W_BR), lambda i: (jnp.minimum((i + 1) * r, last), c))

    return pl.pallas_call(
        functools.partial(_conv_a_kernel, nbt=nbt),
        out_shape=jax.ShapeDtypeStruct((t, W_BR), BF16),
        grid=(t // tm,),
        in_specs=[cur(0), cur(1), cur(2), prev(1), prev(2), nxt(1), nxt(2),
                  pl.BlockSpec((CONV_A, W_BR), lambda i: (0, 0))],
        out_specs=pl.BlockSpec((tm, W_BR), lambda i: (i, 0)),
        scratch_shapes=[pltpu.VMEM((tm + 2 * HALO_A, W_BR), F32)],
        compiler_params=_cparams(("parallel",)),
        name="conv_a",
    )(pf, pf, pf, pf, pf, pf, pf, w)


HALO_C = 16
ROWS_C = 64
SUBLANES = 8


def _conv_c_kernel(v_ref, g_ref, vp_ref, gp_ref, vn_ref, gn_ref, w_ref, b_ref, lg_ref, lb_ref,
                   o_ref, u_ref, us_ref, *, nbt):
    it = pl.program_id(0) % nbt
    tm = v_ref.shape[0]
    u_ref[HALO_C:HALO_C + tm, :] = v_ref[...] * _sigmoid(g_ref[...])
    u_ref[0:HALO_C, :] = jnp.where(it > 0, vp_ref[...] * _sigmoid(gp_ref[...]), 0.0)
    u_ref[HALO_C + tm:, :] = jnp.where(it < nbt - 1, vn_ref[...] * _sigmoid(gn_ref[...]), 0.0)
    n_sh = us_ref.shape[1]
    for r in range(1, SUBLANES):
        us_ref[r - 1] = u_ref[r:r + n_sh, :]
    pad = CONV_C // 2

    def tap(row):
        a, r = divmod(row, SUBLANES)
        if r == 0:
            return u_ref[a * SUBLANES:a * SUBLANES + ROWS_C, :]
        return us_ref[r - 1, a * SUBLANES:a * SUBLANES + ROWS_C, :]

    for c in range(tm // ROWS_C):
        base = HALO_C + c * ROWS_C - pad
        acc = w_ref[0:1, :] * tap(base)
        for k in range(1, CONV_C):
            acc = acc + w_ref[k:k + 1, :] * tap(base + k)
        acc = acc + b_ref[...]
        mu = jnp.mean(acc, axis=-1, keepdims=True)
        xc = acc - mu
        var = jnp.mean(xc * xc, axis=-1, keepdims=True)
        y = xc * lax.rsqrt(var + EPS) * lg_ref[...] + lb_ref[...]
        o_ref[c * ROWS_C:(c + 1) * ROWS_C, :] = (y * _sigmoid(y)).astype(BF16)


def _conv_c(pf, w, b, lg, lb, *, seq, tm):
    t = pf.shape[0]
    nbt = seq // tm
    r = tm // HALO_C
    last = t // HALO_C - 1
    c0 = 3

    def cur(c):
        return pl.BlockSpec((tm, W_BR), lambda i: (i, c))

    def prev(c):
        return pl.BlockSpec((HALO_C, W_BR), lambda i: (jnp.maximum(i * r - 1, 0), c))

    def nxt(c):
        return pl.BlockSpec((HALO_C, W_BR), lambda i: (jnp.minimum((i + 1) * r, last), c))

    def vec():
        return pl.BlockSpec((1, W_BR), lambda i: (0, 0))

    return pl.pallas_call(
        functools.partial(_conv_c_kernel, nbt=nbt),
        out_shape=jax.ShapeDtypeStruct((t, W_BR), BF16),
        grid=(t // tm,),
        in_specs=[cur(c0), cur(c0 + 1), prev(c0), prev(c0 + 1), nxt(c0), nxt(c0 + 1),
                  pl.BlockSpec((CONV_C, W_BR), lambda i: (0, 0)), vec(), vec(), vec()],
        out_specs=pl.BlockSpec((tm, W_BR), lambda i: (i, 0)),
        scratch_shapes=[pltpu.VMEM((tm + 2 * HALO_C, W_BR), F32),
                        pltpu.VMEM((SUBLANES - 1, tm + 2 * HALO_C - SUBLANES, W_BR), F32)],
        compiler_params=_cparams(("parallel",)),
        name="conv_c",
    )(pf, pf, pf, pf, pf, pf, w, b, lg, lb)


def _dot_nt(a, b):
    return lax.dot_general(a, b, (((1,), (1,)), ((), ())), preferred_element_type=F32)


def _win_kernel(sink_ref, q_ref, kp_ref, kc_ref, kn_ref, kx_ref, vp_ref, vc_ref, vn_ref, vx_ref, o_ref, vext_ref,
                *, nq):
    hk = pl.program_id(1)
    i = pl.program_id(2)
    tq = q_ref.shape[0]
    n_loc = tq + 2 * WINDOW
    kk = jnp.concatenate([kp_ref[...], kc_ref[...], kn_ref[...], kx_ref[...]], axis=0)
    n_all = kk.shape[0]
    vext_ref[0:HEAD_DIM, 0:WINDOW] = vp_ref[...]
    vext_ref[0:HEAD_DIM, WINDOW:WINDOW + tq] = vc_ref[...]
    vext_ref[0:HEAD_DIM, WINDOW + tq:n_loc] = vn_ref[...]
    vext_ref[0:HEAD_DIM, n_loc:] = vx_ref[...]
    vext_ref[HEAD_DIM:, :] = jnp.ones((ONES_ROWS, n_all), BF16)
    key = lax.broadcasted_iota(jnp.int32, (n_all, tq), 0)
    qry = lax.broadcasted_iota(jnp.int32, (n_all, tq), 1)
    rel = key - WINDOW - qry
    lo = jnp.where(i > 0, 0, WINDOW)
    hi = jnp.where(i < nq - 1, n_loc, tq + WINDOW)
    valid = (key >= n_loc) | ((jnp.abs(rel) <= WINDOW) & (key >= lo) & (key < hi))

    def scores(g):
        return _dot_nt(kk, q_ref[:, g * HEAD_DIM:(g + 1) * HEAD_DIM])

    ahead = [scores(0)]
    outs = []
    for g in range(GROUP):
        s = jnp.where(valid, ahead.pop(0), NEG_BIG)
        if g + 1 < GROUP:
            ahead.append(scores(g + 1))
        sink = sink_ref[hk * GROUP + g] * LOG2E
        m = jnp.maximum(jnp.max(s, axis=0, keepdims=True), sink)
        p = jnp.exp2(s - m).astype(BF16)
        acc = jnp.dot(vext_ref[...], p, preferred_element_type=F32)
        l = acc[HEAD_DIM:HEAD_DIM + 1, :] + jnp.exp2(sink - m)
        outs.append((acc[0:HEAD_DIM, :] / l).T.astype(BF16))
    o_ref[...] = jnp.concatenate(outs, axis=1)


def _win_attn(q, k, vt, k_c, vt_c, sink, *, batch, seq, ctx, tq):
    nq = seq // tq
    r = tq // WINDOW
    nblk = seq // WINDOW

    def prev_blk(b, i):
        return b * nblk + jnp.maximum(i * r - 1, 0)

    def next_blk(b, i):
        return b * nblk + jnp.minimum((i + 1) * r, nblk - 1)

    return pl.pallas_call(
        functools.partial(_win_kernel, nq=nq),
        out_shape=jax.ShapeDtypeStruct((batch * seq, Q_W), BF16),
        grid=(batch, N_KV, nq),
        in_specs=[pl.BlockSpec(memory_space=pltpu.SMEM),
                  pl.BlockSpec((tq, GROUP * HEAD_DIM), lambda b, h, i: (b * nq + i, h)),
                  pl.BlockSpec((None, WINDOW, HEAD_DIM), lambda b, h, i: (h, prev_blk(b, i), 0)),
                  pl.BlockSpec((None, tq, HEAD_DIM), lambda b, h, i: (h, b * nq + i, 0)),
                  pl.BlockSpec((None, WINDOW, HEAD_DIM), lambda b, h, i: (h, next_blk(b, i), 0)),
                  pl.BlockSpec((None, ctx, HEAD_DIM), lambda b, h, i: (h, b, 0)),
                  pl.BlockSpec((HEAD_DIM, WINDOW), lambda b, h, i: (h, prev_blk(b, i))),
                  pl.BlockSpec((HEAD_DIM, tq), lambda b, h, i: (h, b * nq + i)),
                  pl.BlockSpec((HEAD_DIM, WINDOW), lambda b, h, i: (h, next_blk(b, i))),
                  pl.BlockSpec((HEAD_DIM, ctx), lambda b, h, i: (h, b))],
        out_specs=pl.BlockSpec((tq, GROUP * HEAD_DIM), lambda b, h, i: (b * nq + i, h)),
        scratch_shapes=[pltpu.VMEM((HEAD_DIM + ONES_ROWS, tq + 2 * WINDOW + ctx), BF16)],
        compiler_params=_cparams(("parallel", "parallel", "parallel")),
        name="win_attn",
    )(sink, q, k, k, k, k_c, vt, vt, vt, vt_c)


ONES_ROWS = 16
Q_CHUNK = 256
K_SUB = 256
QK_AHEAD = 5


def _flash_kernel(sink_ref, q_ref, k_ref, vt_ref, *rest, n_main, has_extra, use_sink):
    if has_extra:
        kx_ref, vtx_ref, o_ref, q2_ref, m_ref, acc_ref, vext_ref, vextx_ref = rest
    else:
        o_ref, q2_ref, m_ref, acc_ref, vext_ref = rest
    hk = pl.program_id(1)
    j = pl.program_id(3)
    tq = q_ref.shape[0]

    @pl.when(j == 0)
    def _():
        for g in range(GROUP):
            q2_ref[g * tq:(g + 1) * tq, :] = q_ref[:, g * HEAD_DIM:(g + 1) * HEAD_DIM]
            if use_sink:
                m_ref[:, g * tq:(g + 1) * tq] = jnp.full((1, tq), sink_ref[hk * GROUP + g] * LOG2E, F32)
        if not use_sink:
            m_ref[...] = jnp.full(m_ref.shape, NEG_BIG, F32)
        acc_ref[0:HEAD_DIM, :] = jnp.zeros((HEAD_DIM, GROUP * tq), F32)
        acc_ref[HEAD_DIM:, :] = jnp.full((ONES_ROWS, GROUP * tq), 1.0 if use_sink else 0.0, F32)
        vext_ref[HEAD_DIM:, :] = jnp.ones((ONES_ROWS, vext_ref.shape[1]), BF16)
        if has_extra:
            vextx_ref[HEAD_DIM:, :] = jnp.ones((ONES_ROWS, vextx_ref.shape[1]), BF16)

    n_chunks = GROUP * tq // Q_CHUNK

    def step(kk_ref, vv_ref, vext):
        tkk = kk_ref.shape[0]
        kb = min(tkk, K_SUB)
        units = [(b, c) for b in range(tkk // kb) for c in range(n_chunks)]
        vext[0:HEAD_DIM, :] = vv_ref[...]

        def scores(u):
            b, c = u
            return _dot_nt(kk_ref[b * kb:(b + 1) * kb, :], q2_ref[c * Q_CHUNK:(c + 1) * Q_CHUNK, :])

        m = [m_ref[:, c * Q_CHUNK:(c + 1) * Q_CHUNK] for c in range(n_chunks)]
        acc = [acc_ref[:, c * Q_CHUNK:(c + 1) * Q_CHUNK] for c in range(n_chunks)]
        ahead = [scores(u) for u in units[:QK_AHEAD]]
        for idx, (b, c) in enumerate(units):
            s = ahead.pop(0)
            if idx + QK_AHEAD < len(units):
                ahead.append(scores(units[idx + QK_AHEAD]))
            m_next = jnp.maximum(m[c], jnp.max(s, axis=0, keepdims=True))
            alpha = jnp.exp2(m[c] - m_next)
            p = jnp.exp2(s - m_next).astype(BF16)
            acc[c] = alpha * acc[c] + jnp.dot(vext[:, b * kb:(b + 1) * kb], p, preferred_element_type=F32)
            m[c] = m_next
        acc_ref[...] = jnp.concatenate(acc, axis=1)
        m_ref[...] = jnp.concatenate(m, axis=1)

    @pl.when(j < n_main)
    def _():
        step(k_ref, vt_ref, vext_ref)

    if has_extra:
        @pl.when(j == n_main)
        def _():
            step(kx_ref, vtx_ref, vextx_ref)

    @pl.when(j == n_main + (1 if has_extra else 0) - 1)
    def _():
        o = (acc_ref[0:HEAD_DIM, :] / acc_ref[HEAD_DIM:HEAD_DIM + 1, :]).T
        for g in range(GROUP):
            o_ref[:, g * HEAD_DIM:(g + 1) * HEAD_DIM] = o[g * tq:(g + 1) * tq, :].astype(BF16)


def _flash(q_arr, k_arr, vt_arr, kx_arr, vtx_arr, sink, *, batch, seq_q, seq_k, seq_x, tq, tk, branch, use_sink):
    nq = seq_q // tq
    n_main = seq_k // tk
    has_extra = kx_arr is not None
    n_steps = n_main + (1 if has_extra else 0)
    qc = branch * N_KV
    kr = branch * N_KV

    def jj(j):
        return jnp.minimum(j, n_main - 1)

    in_specs = [pl.BlockSpec(memory_space=pltpu.SMEM),
                pl.BlockSpec((tq, GROUP * HEAD_DIM), lambda b, h, i, j: (b * nq + i, qc + h)),
                pl.BlockSpec((None, tk, HEAD_DIM), lambda b, h, i, j: (kr + h, b * n_main + jj(j), 0)),
                pl.BlockSpec((HEAD_DIM, tk), lambda b, h, i, j: (kr + h, b * n_main + jj(j)))]
    args = [sink, q_arr, k_arr, vt_arr]
    rows = GROUP * tq
    scratch = [pltpu.VMEM((rows, HEAD_DIM), BF16),
               pltpu.VMEM((1, rows), F32),
               pltpu.VMEM((HEAD_DIM + ONES_ROWS, rows), F32),
               pltpu.VMEM((HEAD_DIM + ONES_ROWS, tk), BF16)]
    if has_extra:
        in_specs += [pl.BlockSpec((None, seq_x, HEAD_DIM), lambda b, h, i, j: (kr + h, b, 0)),
                     pl.BlockSpec((HEAD_DIM, seq_x), lambda b, h, i, j: (kr + h, b))]
        args += [kx_arr, vtx_arr]
        scratch += [pltpu.VMEM((HEAD_DIM + ONES_ROWS, seq_x), BF16)]
    return pl.pallas_call(
        functools.partial(_flash_kernel, n_main=n_main, has_extra=has_extra, use_sink=use_sink),
        out_shape=jax.ShapeDtypeStruct((batch * seq_q, Q_W), BF16),
        grid=(batch, N_KV, nq, n_steps),
        in_specs=in_specs,
        out_specs=pl.BlockSpec((tq, GROUP * HEAD_DIM), lambda b, h, i, j: (b * nq + i, h)),
        scratch_shapes=scratch,
        compiler_params=_cparams(("parallel", "parallel", "parallel", "arbitrary")),
        name="flash",
    )(*args)


def _merge_kernel(h_ref, ya_ref, yb_ref, yc_ref, yd_ref, wga_ref, wgb_ref, wgc_ref, wgd_ref,
                  wb_ref, bg_ref, o_ref):
    h = h_ref[...]
    m = None
    for k, (y_ref, wg_ref) in enumerate(((ya_ref, wga_ref), (yb_ref, wgb_ref), (yc_ref, wgc_ref), (yd_ref, wgd_ref))):
        logits = jnp.dot(h, wg_ref[...], preferred_element_type=F32) + bg_ref[k:k + 1, :]
        t = _sigmoid(logits) * jnp.dot(y_ref[...], wb_ref[k], preferred_element_type=F32)
        m = t if m is None else m + t
    o_ref[...] = m.astype(BF16)


def _merge(h, ys, wg, wb, bg, *, layer, tm, tn):
    t = h.shape[0]
    nct = D_MODEL // tn

    def y():
        return pl.BlockSpec((tm, W_BR), lambda i, j: (i, 0))

    def g(k):
        return pl.BlockSpec((None, D_MODEL, tn), lambda i, j: (layer, 0, OFF_G // tn + k * nct + j))

    return pl.pallas_call(
        _merge_kernel,
        out_shape=jax.ShapeDtypeStruct((t, D_MODEL), BF16),
        grid=(t // tm, nct),
        in_specs=[pl.BlockSpec((tm, D_MODEL), lambda i, j: (i, 0)), y(), y(), y(), y(),
                  g(0), g(1), g(2), g(3),
                  pl.BlockSpec((None, N_BRANCH, W_BR, tn), lambda i, j: (layer, 0, 0, j)),
                  pl.BlockSpec((N_BRANCH, tn), lambda i, j: (0, j))],
        out_specs=pl.BlockSpec((tm, tn), lambda i, j: (i, j)),
        compiler_params=_cparams(("parallel", "arbitrary")),
        name="merge",
    )(h, *ys, wg, wg, wg, wg, wb, bg)


def _resid_kernel(a_ref, w_ref, x_ref, gate_ref, o_ref, *scratch, nk):
    part = jnp.dot(a_ref[...], w_ref[...], preferred_element_type=F32)
    if nk == 1:
        o_ref[...] = x_ref[...] + gate_ref[0] * part
        return
    (acc_ref,) = scratch
    k = pl.program_id(1)

    @pl.when(k == 0)
    def _():
        acc_ref[...] = part

    if nk > 2:
        @pl.when((k > 0) & (k < nk - 1))
        def _():
            acc_ref[...] += part

    @pl.when(k == nk - 1)
    def _():
        o_ref[...] = x_ref[...] + gate_ref[0] * (acc_ref[...] + part)


def _resid(a, w, x, gate, *, layer, seq, tm, tk):
    t, k = a.shape
    nbt = seq // tm
    nk = k // tk
    return pl.pallas_call(
        functools.partial(_resid_kernel, nk=nk),
        out_shape=jax.ShapeDtypeStruct((t, D_MODEL), F32),
        grid=(t // tm, nk),
        in_specs=[pl.BlockSpec((tm, tk), lambda i, j: (i, j)),
                  pl.BlockSpec((None, tk, D_MODEL), lambda i, j: (layer, j, 0)),
                  pl.BlockSpec((tm, D_MODEL), lambda i, j: (i, 0)),
                  pl.BlockSpec((1, 1, D_MODEL), lambda i, j: (i // nbt, 0, 0))],
        out_specs=pl.BlockSpec((tm, D_MODEL), lambda i, j: (i, 0)),
        scratch_shapes=[pltpu.VMEM((tm, D_MODEL), F32)] if nk > 1 else [],
        compiler_params=_cparams(("parallel", "arbitrary")),
        name="resid",
    )(a, w, x, gate)


def _resid_norm_kernel(a_ref, w_ref, x_ref, gate_ref, gn_ref, o_ref, acc_ref, *, nk):
    k = pl.program_id(1)
    part = jnp.dot(a_ref[...], w_ref[...], preferred_element_type=F32)

    @pl.when(k == 0)
    def _():
        acc_ref[...] = part

    @pl.when(k > 0)
    def _():
        acc_ref[...] += part

    @pl.when(k == nk - 1)
    def _():
        gate = gate_ref[0]
        gn = gn_ref[...]

        def body(i, carry):
            rows = pl.ds(pl.multiple_of(i * NORM_CHUNK, NORM_CHUNK), NORM_CHUNK)
            y = x_ref[rows, :] + gate * acc_ref[rows, :]
            o_ref[rows, :] = y * lax.rsqrt(jnp.mean(y * y, axis=-1, keepdims=True) + EPS) * gn
            return carry

        lax.fori_loop(0, o_ref.shape[0] // NORM_CHUNK, body, 0, unroll=NORM_UNROLL)


def _resid_norm(a, w, x, gate, gn, *, layer, seq, tm, tk):
    t, k = a.shape
    nbt = seq // tm
    nk = k // tk
    return pl.pallas_call(
        functools.partial(_resid_norm_kernel, nk=nk),
        out_shape=jax.ShapeDtypeStruct((t, D_MODEL), F32),
        grid=(t // tm, nk),
        in_specs=[pl.BlockSpec((tm, tk), lambda i, j: (i, j)),
                  pl.BlockSpec((None, tk, D_MODEL), lambda i, j: (layer, j, 0)),
                  pl.BlockSpec((tm, D_MODEL), lambda i, j: (i, 0)),
                  pl.BlockSpec((1, 1, D_MODEL), lambda i, j: (i // nbt, 0, 0)),
                  pl.BlockSpec((1, D_MODEL), lambda i, j: (0, 0))],
        out_specs=pl.BlockSpec((tm, D_MODEL), lambda i, j: (i, 0)),
        scratch_shapes=[pltpu.VMEM((tm, D_MODEL), F32)],
        compiler_params=_cparams(("parallel", "arbitrary")),
        name="resid_norm",
    )(a, w, x, gate, gn)


def _ffn_in_kernel(x_ref, g_ref, sc_ref, sh_ref, wa_ref, wb_ref, o_ref, h_ref):
    @pl.when(pl.program_id(1) == 0)
    def _():
        _norm_mod_rows(x_ref, g_ref, sc_ref, sh_ref, h_ref)

    h = h_ref[...]
    a = jnp.dot(h, wa_ref[...], preferred_element_type=F32)
    b = jnp.dot(h, wb_ref[...], preferred_element_type=F32)
    o_ref[...] = (a * _sigmoid(a) * b).astype(BF16)


def _ffn_in(x, g, sc, sh, w, *, layer, seq, tm, tn):
    t = x.shape[0]
    nbt = seq // tm
    nct = FFN_HIDDEN // tn
    return pl.pallas_call(
        _ffn_in_kernel,
        out_shape=jax.ShapeDtypeStruct((t, FFN_HIDDEN), BF16),
        grid=(t // tm, nct),
        in_specs=[pl.BlockSpec((tm, D_MODEL), lambda i, j: (i, 0)),
                  pl.BlockSpec((1, D_MODEL), lambda i, j: (0, 0)),
                  pl.BlockSpec((1, 1, D_MODEL), lambda i, j: (i // nbt, 0, 0)),
                  pl.BlockSpec((1, 1, D_MODEL), lambda i, j: (i // nbt, 0, 0)),
                  pl.BlockSpec((None, D_MODEL, tn), lambda i, j: (layer, 0, j)),
                  pl.BlockSpec((None, D_MODEL, tn), lambda i, j: (layer, 0, nct + j))],
        out_specs=pl.BlockSpec((tm, tn), lambda i, j: (i, j)),
        scratch_shapes=[pltpu.VMEM((tm, D_MODEL), BF16)],
        compiler_params=_cparams(("parallel", "arbitrary")),
        name="ffn_in",
    )(x, g, sc, sh, w, w)


def _rope_tables(seq):
    pos = jnp.arange(seq)
    row = (pos // GRID_W).astype(F32)
    col = (pos % GRID_W).astype(F32)
    inv = ROPE_THETA ** (-jnp.arange(0, ROT_AXIS, 2, dtype=F32) / ROT_AXIS)
    ar = row[:, None] * inv
    ac = col[:, None] * inv
    ang = jnp.concatenate([ar, ac, ar, ac], axis=-1)
    first = jnp.arange(HEAD_DIM) < HEAD_DIM // 2
    return jnp.cos(ang), jnp.where(first, -jnp.sin(ang), jnp.sin(ang))


def _pair_split(v):
    quarter = HEAD_DIM // 4
    q = v.reshape(v.shape[:-1] + (v.shape[-1] // HEAD_DIM, 4, quarter))
    q = jnp.concatenate([q[..., 0:1, :], q[..., 2:3, :], q[..., 1:2, :], q[..., 3:4, :]], axis=-2)
    return q.reshape(v.shape)


def _pair_split_heads(w):
    return jnp.concatenate([w[..., :OFF_BQ], _pair_split(w[..., OFF_BQ:OFF_BV]), w[..., OFF_BV:OFF_DQ],
                            _pair_split(w[..., OFF_DQ:OFF_DV]), w[..., OFF_DV:]], axis=-1)


def _tile(n, pref):
    return pref if n % pref == 0 else n


def _forward(x, c, ctx, c_ctx, w_ada, b_ada, norm_mix, norm_ffn, w_in, b_gate, conv_a_w, sink_b,
             qk_norm_q, qk_norm_k, conv_c_w, conv_c_b, ln_c_g, ln_c_b, w_branch, w_out,
             w_ffn_in, w_ffn_out, norm_final):
    batch, seq, _ = x.shape
    n_ctx = ctx.shape[1]
    depth = w_in.shape[0]
    xs = x.reshape(batch * seq, D_MODEL)
    xc = ctx.reshape(batch * n_ctx, D_MODEL)

    tm = _tile(seq, 1024)
    tm_rows = _tile(seq, 512)
    tm_conv_a = _tile(seq, 512)
    tm_conv_c = _tile(seq, 512)
    tq_win = _tile(seq, 512)
    tq = _tile(seq, 1024)
    tk = _tile(seq, 4096)

    rope_x = _rope_tables(seq)
    rope_c = (jnp.ones((n_ctx, HEAD_DIM), F32), jnp.zeros((n_ctx, HEAD_DIM), F32))
    cvec = jnp.zeros((ADA_ROWS, D_MODEL), F32).at[:batch].set(c).at[batch].set(c_ctx)

    def row(v):
        return v.reshape(1, -1)

    w_proj = w_gate = _pair_split_heads(w_in).astype(BF16)
    w_br = w_branch.astype(BF16)
    w_o = w_out.astype(BF16)
    w_f1 = w_ffn_in.astype(BF16)
    w_f2 = w_ffn_out.astype(BF16)

    def mixers(pf, q, k, vt, k_c, vt_c, l, *, s, tma, tmc, is_ctx):
        y_a = _conv_a(pf, conv_a_w[l], seq=s, tm=tma)
        y_c = _conv_c(pf, conv_c_w[l], row(conv_c_b[l]), row(ln_c_g[l]), row(ln_c_b[l]), seq=s, tm=tmc)
        if is_ctx:
            y_b = _flash(q, k, vt, None, None, sink_b[l], batch=batch, seq_q=s, seq_k=s, seq_x=0, tq=s, tk=s,
                         branch=0, use_sink=True)
            y_d = _flash(q, k, vt, None, None, sink_b[l], batch=batch, seq_q=s, seq_k=s, seq_x=0, tq=s, tk=s,
                         branch=1, use_sink=False)
        else:
            y_b = _win_attn(q, k, vt, k_c, vt_c, sink_b[l], batch=batch, seq=s, ctx=n_ctx, tq=tq_win)
            y_d = _flash(q, k, vt, k_c, vt_c, sink_b[l], batch=batch, seq_q=s, seq_k=s, seq_x=n_ctx,
                         tq=tq, tk=tk, branch=1, use_sink=False)
        return [y_a, y_b, y_c, y_d]

    for l in range(depth):
        last = l == depth - 1
        mod = _ada(cvec, w_ada, row(b_ada[l]), layer=l)
        mx = mod[:batch].reshape(batch, 1, 6, D_MODEL)
        mcx = jnp.broadcast_to(mod[batch].reshape(1, 1, 6, D_MODEL), (batch, 1, 6, D_MODEL))
        sh_m, sc_m, g_m, sh_f, sc_f, g_f = (mx[:, :, k] for k in range(6))
        csh_m, csc_m, cg_m, csh_f, csc_f, cg_f = (mcx[:, :, k] for k in range(6))
        gq, gk = row(_pair_split(qk_norm_q[l])), row(_pair_split(qk_norm_k[l]))

        pf_c, q_c, h_c, k_c, vt_c = _inproj(xc, row(norm_mix[l]), csc_m, csh_m, w_proj, *rope_c, gq, gk,
                                            layer=l, seq=n_ctx, tm=n_ctx)
        pf, q, h, k, vt = _inproj(xs, row(norm_mix[l]), sc_m, sh_m, w_proj, *rope_x, gq, gk,
                                  layer=l, seq=seq, tm=tm)

        ys = mixers(pf, q, k, vt, k_c, vt_c, l, s=seq, tma=tm_conv_a, tmc=tm_conv_c, is_ctx=False)
        m = _merge(h, ys, w_gate, w_br, b_gate[l], layer=l, tm=tm, tn=512)
        xs = _resid(m, w_o, xs, g_m, layer=l, seq=seq, tm=tm_rows, tk=D_MODEL)

        if not last:
            ys_c = mixers(pf_c, q_c, k_c, vt_c, None, None, l, s=n_ctx, tma=n_ctx, tmc=n_ctx, is_ctx=True)
            m_c = _merge(h_c, ys_c, w_gate, w_br, b_gate[l], layer=l, tm=n_ctx, tn=512)
            xc = _resid(m_c, w_o, xc, cg_m, layer=l, seq=n_ctx, tm=n_ctx, tk=D_MODEL)
            hid_c = _ffn_in(xc, row(norm_ffn[l]), csc_f, csh_f, w_f1, layer=l, seq=n_ctx, tm=n_ctx, tn=512)
            xc = _resid(hid_c, w_f2, xc, cg_f, layer=l, seq=n_ctx, tm=n_ctx, tk=FFN_HIDDEN // 2)

        hid = _ffn_in(xs, row(norm_ffn[l]), sc_f, sh_f, w_f1, layer=l, seq=seq, tm=tm, tn=512)
        if last:
            xs = _resid_norm(hid, w_f2, xs, g_f, row(norm_final), layer=l, seq=seq, tm=tm_rows,
                             tk=FFN_HIDDEN // 2)
        else:
            xs = _resid(hid, w_f2, xs, g_f, layer=l, seq=seq, tm=tm_rows, tk=FFN_HIDDEN // 2)

    return xs.reshape(batch, seq, D_MODEL)


def kernel(x, c, ctx, c_ctx, w_ada, b_ada, norm_mix, norm_ffn, w_in, b_gate, conv_a_w, sink_b, qk_norm_q,
           qk_norm_k, conv_c_w, conv_c_b, ln_c_g, ln_c_b, w_branch, w_out, w_ffn_in, w_ffn_out, norm_final):
    return _forward(x, c, ctx, c_ctx, w_ada, b_ada, norm_mix, norm_ffn, w_in, b_gate, conv_a_w, sink_b,
                    qk_norm_q, qk_norm_k, conv_c_w, conv_c_b, ln_c_g, ln_c_b, w_branch, w_out,
                    w_ffn_in, w_ffn_out, norm_final)
```

```python
import functools

import jax
import jax.numpy as jnp
from jax import lax
from jax.experimental import pallas as pl
from jax.experimental.pallas import tpu as pltpu

F32 = jnp.float32
BF16 = jnp.bfloat16

D_MODEL = 2048
N_BRANCH = 4
W_BR = D_MODEL // 4
HEAD_DIM = 128
N_Q = W_BR // HEAD_DIM
N_KV = N_Q // 2
GROUP = N_Q // N_KV
Q_W = N_Q * HEAD_DIM
KV_W = N_KV * HEAD_DIM
CONV_A = 3
CONV_C = 31
WINDOW = 128
GRID_W = 64
ROT_AXIS = HEAD_DIM // 2
ROPE_THETA = 10000.0
FFN_HIDDEN = 5632
EPS = 1e-6
ATTN_SCALE = HEAD_DIM ** -0.5
LOG2E = 1.4426950408889634
Q_SCALE = ATTN_SCALE * LOG2E
NEG_BIG = -1e30

OFF_A = 0
OFF_BQ = OFF_A + 3 * W_BR
OFF_BK = OFF_BQ + Q_W
OFF_BV = OFF_BK + KV_W
OFF_C = OFF_BV + KV_W
OFF_DQ = OFF_C + 2 * W_BR
OFF_DK = OFF_DQ + Q_W
OFF_DV = OFF_DK + KV_W
OFF_G = OFF_DV + KV_W

PF_W = 3 * W_BR + 2 * W_BR
PQ_W = 2 * (Q_W + 2 * KV_W)
TN = 512
N_PF_TILES = PF_W // TN
N_PROJ_TILES = (PF_W + PQ_W) // TN

VMEM_LIMIT = 56 * 1024 * 1024


def _cparams(sem):
    return pltpu.CompilerParams(dimension_semantics=sem, vmem_limit_bytes=VMEM_LIMIT)


def _sigmoid(z):
    return 1.0 / (1.0 + jnp.exp(-z))


def _ada_kernel(c_ref, w_ref, b_ref, o_ref):
    c = c_ref[...]
    s = (c * _sigmoid(c)).astype(BF16)
    o_ref[...] = jnp.dot(s, w_ref[...].astype(BF16), preferred_element_type=F32) + b_ref[...]


ADA_ROWS = 16


def _ada(cvec, w, b, *, layer):
    n = w.shape[2]
    tn = 1024
    return pl.pallas_call(
        _ada_kernel,
        out_shape=jax.ShapeDtypeStruct((ADA_ROWS, n), F32),
        grid=(n // tn,),
        in_specs=[pl.BlockSpec((ADA_ROWS, D_MODEL), lambda j: (0, 0)),
                  pl.BlockSpec((None, D_MODEL, tn), lambda j: (layer, 0, j)),
                  pl.BlockSpec((1, tn), lambda j: (0, j))],
        out_specs=pl.BlockSpec((ADA_ROWS, tn), lambda j: (0, j)),
        compiler_params=_cparams(("arbitrary",)),
        name="ada",
    )(cvec, w, b)


def _rope(xh, cos, sin_signed):
    return xh * cos + pltpu.roll(xh, HEAD_DIM // 2, 1) * sin_signed


def _head_norm(xh, g):
    return xh * lax.rsqrt(jnp.mean(xh * xh, axis=-1, keepdims=True) + EPS) * g


NORM_CHUNK = 16
NORM_UNROLL = 8


def _norm_mod_rows(x_ref, g_ref, sc_ref, sh_ref, h_ref):
    gs = g_ref[...] * (1.0 + sc_ref[0])
    sh = sh_ref[0]

    def body(i, carry):
        rows = pl.ds(pl.multiple_of(i * NORM_CHUNK, NORM_CHUNK), NORM_CHUNK)
        x = x_ref[rows, :]
        r = lax.rsqrt(jnp.mean(x * x, axis=-1, keepdims=True) + EPS)
        h_ref[rows, :] = (x * r * gs + sh).astype(BF16)
        return carry

    lax.fori_loop(0, x_ref.shape[0] // NORM_CHUNK, body, 0, unroll=NORM_UNROLL)


def _inproj_kernel(x_ref, g_ref, sc_ref, sh_ref, w_ref, cos_ref, sin_ref, gq_ref, gk_ref,
                   pf_ref, q_ref, h_ref, k_ref, vt_ref):
    j = pl.program_id(1)

    @pl.when(j == 0)
    def _():
        _norm_mod_rows(x_ref, g_ref, sc_ref, sh_ref, h_ref)

    acc = jnp.dot(h_ref[...], w_ref[...], preferred_element_type=F32)

    @pl.when(j < N_PF_TILES)
    def _():
        pf_ref[...] = acc

    def rope(xh):
        return _rope(xh, cos_ref[...], sin_ref[...])

    def head(k):
        return acc[:, k * HEAD_DIM:(k + 1) * HEAD_DIM]

    def put_q(k, v):
        q_ref[:, k * HEAD_DIM:(k + 1) * HEAD_DIM] = v.astype(BF16)

    @pl.when(j == N_PF_TILES)
    def _():
        for k in range(N_Q):
            put_q(k, rope(head(k)) * Q_SCALE)

    @pl.when(j == N_PF_TILES + 1)
    def _():
        for k in range(N_KV):
            k_ref[k] = rope(head(k)).astype(BF16)
        vt_ref[0:KV_W, :] = acc[:, KV_W:].T.astype(BF16)

    @pl.when(j == N_PF_TILES + 2)
    def _():
        for k in range(N_Q):
            put_q(k, rope(_head_norm(head(k), gq_ref[...])) * Q_SCALE)

    @pl.when(j == N_PF_TILES + 3)
    def _():
        for k in range(N_KV):
            k_ref[N_KV + k] = rope(_head_norm(head(k), gk_ref[...])).astype(BF16)
        vt_ref[KV_W:, :] = acc[:, KV_W:].T.astype(BF16)


def _proj_block(j):
    n_a, n_b, n_c = 3 * W_BR // TN, (Q_W + 2 * KV_W) // TN, 2 * W_BR // TN
    return jnp.where(j < n_a, j, jnp.where(j < n_a + n_c, j + n_b, jnp.where(j < n_a + n_c + n_b, j - n_c, j)))


def _inproj(x, g, sc, sh, w, cos, sin_signed, gq, gk, *, layer, seq, tm):
    t = x.shape[0]
    nbt = seq // tm
    return pl.pallas_call(
        _inproj_kernel,
        out_shape=(jax.ShapeDtypeStruct((t, PF_W), F32),
                   jax.ShapeDtypeStruct((t, 2 * Q_W), BF16),
                   jax.ShapeDtypeStruct((t, D_MODEL), BF16),
                   jax.ShapeDtypeStruct((2 * N_KV, t, HEAD_DIM), BF16),
                   jax.ShapeDtypeStruct((2 * KV_W, t), BF16)),
        grid=(t // tm, N_PROJ_TILES),
        in_specs=[pl.BlockSpec((tm, D_MODEL), lambda i, j: (i, 0)),
                  pl.BlockSpec((1, D_MODEL), lambda i, j: (0, 0)),
                  pl.BlockSpec((1, 1, D_MODEL), lambda i, j: (i // nbt, 0, 0)),
                  pl.BlockSpec((1, 1, D_MODEL), lambda i, j: (i // nbt, 0, 0)),
                  pl.BlockSpec((None, D_MODEL, TN), lambda i, j: (layer, 0, _proj_block(j))),
                  pl.BlockSpec((tm, HEAD_DIM), lambda i, j: (i % nbt, 0)),
                  pl.BlockSpec((tm, HEAD_DIM), lambda i, j: (i % nbt, 0)),
                  pl.BlockSpec((1, HEAD_DIM), lambda i, j: (0, 0)),
                  pl.BlockSpec((1, HEAD_DIM), lambda i, j: (0, 0))],
        out_specs=(pl.BlockSpec((tm, TN), lambda i, j: (i, jnp.minimum(j, N_PF_TILES - 1))),
                   pl.BlockSpec((tm, Q_W), lambda i, j: (i, jnp.where(j >= N_PF_TILES + 2, 1, 0))),
                   pl.BlockSpec((tm, D_MODEL), lambda i, j: (i, 0)),
                   pl.BlockSpec((2 * N_KV, tm, HEAD_DIM), lambda i, j: (0, i, 0)),
                   pl.BlockSpec((2 * KV_W, tm), lambda i, j: (0, i))),
        compiler_params=_cparams(("parallel", "arbitrary")),
        name="inproj",
    )(x, g, sc, sh, w, cos, sin_signed, gq, gk)


HALO_A = 8


def _conv_a_kernel(bg_ref, cg_ref, h_ref, cgp_ref, hp_ref, cgn_ref, hn_ref, w_ref, o_ref, u_ref, *, nbt):
    it = pl.program_id(0) % nbt
    tm = bg_ref.shape[0]
    u_ref[HALO_A:HALO_A + tm, :] = cg_ref[...] * h_ref[...]
    u_ref[0:HALO_A, :] = jnp.where(it > 0, cgp_ref[...] * hp_ref[...], 0.0)
    u_ref[HALO_A + tm:, :] = jnp.where(it < nbt - 1, cgn_ref[...] * hn_ref[...], 0.0)
    y = (w_ref[0:1, :] * u_ref[HALO_A - 1:HALO_A - 1 + tm, :]
         + w_ref[1:2, :] * u_ref[HALO_A:HALO_A + tm, :]
         + w_ref[2:3, :] * u_ref[HALO_A + 1:HALO_A + 1 + tm, :])
    o_ref[...] = (bg_ref[...] * y).astype(BF16)


def _conv_a(pf, w, *, seq, tm):
    t = pf.shape[0]
    nbt = seq // tm
    r = tm // HALO_A
    last = t // HALO_A - 1

    def cur(c):
        return pl.BlockSpec((tm, W_BR), lambda i: (i, c))

    def prev(c):
        return pl.BlockSpec((HALO_A, W_BR), lambda i: (jnp.maximum(i * r - 1, 0), c))

    def nxt(c):
        return pl.BlockSpec((HALO_A, W_BR), lambda i: (jnp.minimum((i + 1) * r, last), c))

    return pl.pallas_call(
        functools.partial(_conv_a_kernel, nbt=nbt),
        out_shape=jax.ShapeDtypeStruct((t, W_BR), BF16),
        grid=(t // tm,),
        in_specs=[cur(0), cur(1), cur(2), prev(1), prev(2), nxt(1), nxt(2),
                  pl.BlockSpec((CONV_A, W_BR), lambda i: (0, 0))],
        out_specs=pl.BlockSpec((tm, W_BR), lambda i: (i, 0)),
        scratch_shapes=[pltpu.VMEM((tm + 2 * HALO_A, W_BR), F32)],
        compiler_params=_cparams(("parallel",)),
        name="conv_a",
    )(pf, pf, pf, pf, pf, pf, pf, w)


HALO_C = 16
ROWS_C = 64
SUBLANES = 8


def _conv_c_kernel(v_ref, g_ref, vp_ref, gp_ref, vn_ref, gn_ref, w_ref, b_ref, lg_ref, lb_ref,
                   o_ref, u_ref, us_ref, *, nbt):
    it = pl.program_id(0) % nbt
    tm = v_ref.shape[0]
    u_ref[HALO_C:HALO_C + tm, :] = v_ref[...] * _sigmoid(g_ref[...])
    u_ref[0:HALO_C, :] = jnp.where(it > 0, vp_ref[...] * _sigmoid(gp_ref[...]), 0.0)
    u_ref[HALO_C + tm:, :] = jnp.where(it < nbt - 1, vn_ref[...] * _sigmoid(gn_ref[...]), 0.0)
    n_sh = us_ref.shape[1]
    for r in range(1, SUBLANES):
        us_ref[r - 1] = u_ref[r:r + n_sh, :]
    pad = CONV_C // 2

    def tap(row):
        a, r = divmod(row, SUBLANES)
        if r == 0:
            return u_ref[a * SUBLANES:a * SUBLANES + ROWS_C, :]
        return us_ref[r - 1, a * SUBLANES:a * SUBLANES + ROWS_C, :]

    for c in range(tm // ROWS_C):
        base = HALO_C + c * ROWS_C - pad
        acc = w_ref[0:1, :] * tap(base)
        for k in range(1, CONV_C):
            acc = acc + w_ref[k:k + 1, :] * tap(base + k)
        acc = acc + b_ref[...]
        mu = jnp.mean(acc, axis=-1, keepdims=True)
        xc = acc - mu
        var = jnp.mean(xc * xc, axis=-1, keepdims=True)
        y = xc * lax.rsqrt(var + EPS) * lg_ref[...] + lb_ref[...]
        o_ref[c * ROWS_C:(c + 1) * ROWS_C, :] = (y * _sigmoid(y)).astype(BF16)


def _conv_c(pf, w, b, lg, lb, *, seq, tm):
    t = pf.shape[0]
    nbt = seq // tm
    r = tm // HALO_C
    last = t // HALO_C - 1
    c0 = 3

    def cur(c):
        return pl.BlockSpec((tm, W_BR), lambda i: (i, c))

    def prev(c):
        return pl.BlockSpec((HALO_C, W_BR), lambda i: (jnp.maximum(i * r - 1, 0), c))

    def nxt(c):
        return pl.BlockSpec((HALO_C, W_BR), lambda i: (jnp.minimum((i + 1) * r, last), c))

    def vec():
        return pl.BlockSpec((1, W_BR), lambda i: (0, 0))

    return pl.pallas_call(
        functools.partial(_conv_c_kernel, nbt=nbt),
        out_shape=jax.ShapeDtypeStruct((t, W_BR), BF16),
        grid=(t // tm,),
        in_specs=[cur(c0), cur(c0 + 1), prev(c0), prev(c0 + 1), nxt(c0), nxt(c0 + 1),
                  pl.BlockSpec((CONV_C, W_BR), lambda i: (0, 0)), vec(), vec(), vec()],
        out_specs=pl.BlockSpec((tm, W_BR), lambda i: (i, 0)),
        scratch_shapes=[pltpu.VMEM((tm + 2 * HALO_C, W_BR), F32),
                        pltpu.VMEM((SUBLANES - 1, tm + 2 * HALO_C - SUBLANES, W_BR), F32)],
        compiler_params=_cparams(("parallel",)),
        name="conv_c",
    )(pf, pf, pf, pf, pf, pf, w, b, lg, lb)


def _dot_nt(a, b):
    return lax.dot_general(a, b, (((1,), (1,)), ((), ())), preferred_element_type=F32)


WIN_UNIT = 2 * WINDOW


def _win_kernel(sink_ref, q_ref, kp_ref, kc_ref, kn_ref, kx_ref, vp_ref, vc_ref, vn_ref, vx_ref, o_ref,
                kk_ref, vext_ref, *, nq):
    hk = pl.program_id(1)
    i = pl.program_id(2)
    tq = q_ref.shape[0]
    n_loc = tq + 2 * WINDOW
    n_all = kk_ref.shape[0]
    u = WIN_UNIT
    kk_ref[0:WINDOW, :] = kp_ref[...]
    kk_ref[WINDOW:WINDOW + tq, :] = kc_ref[...]
    kk_ref[WINDOW + tq:n_loc, :] = kn_ref[...]
    kk_ref[n_loc:, :] = kx_ref[...]
    vext_ref[0:HEAD_DIM, 0:WINDOW] = vp_ref[...]
    vext_ref[0:HEAD_DIM, WINDOW:WINDOW + tq] = vc_ref[...]
    vext_ref[0:HEAD_DIM, WINDOW + tq:n_loc] = vn_ref[...]
    vext_ref[0:HEAD_DIM, n_loc:] = vx_ref[...]
    vext_ref[HEAD_DIM:, :] = jnp.ones((ONES_ROWS, n_all), BF16)

    key = lax.broadcasted_iota(jnp.int32, (u, u), 0)
    qry = lax.broadcasted_iota(jnp.int32, (u, u), 1)
    n_loc_blocks = n_loc // u
    n_chunks = tq // u

    def mask(c, kb):
        ok = jnp.abs((kb - c) * u - WINDOW + key - qry) <= WINDOW
        if kb == 0:
            ok = ok & (key >= jnp.where(i > 0, 0, WINDOW))
        if kb == n_loc_blocks - 1:
            ok = ok & (key < jnp.where(i < nq - 1, u, u - WINDOW))
        return ok

    units = []
    for g in range(GROUP):
        for c in range(n_chunks):
            units += [(g, c, kb) for kb in (c, c + 1)] + [(g, c, kb) for kb in range(n_loc_blocks, n_all // u)]

    def scores(unit):
        g, c, kb = unit
        return _dot_nt(kk_ref[kb * u:(kb + 1) * u, :], q_ref[c * u:(c + 1) * u, g * HEAD_DIM:(g + 1) * HEAD_DIM])

    m, acc = {}, {}
    for g in range(GROUP):
        sink = sink_ref[hk * GROUP + g] * LOG2E
        for c in range(n_chunks):
            m[g, c] = jnp.full((1, u), sink, F32)
            acc[g, c] = jnp.concatenate([jnp.zeros((HEAD_DIM, u), F32), jnp.ones((ONES_ROWS, u), F32)], axis=0)

    ahead = [scores(unit) for unit in units[:QK_AHEAD]]
    for idx, (g, c, kb) in enumerate(units):
        s = ahead.pop(0)
        if idx + QK_AHEAD < len(units):
            ahead.append(scores(units[idx + QK_AHEAD]))
        if kb < n_loc_blocks:
            s = jnp.where(mask(c, kb), s, NEG_BIG)
        m_next = jnp.maximum(m[g, c], jnp.max(s, axis=0, keepdims=True))
        alpha = jnp.exp2(m[g, c] - m_next)
        p = jnp.exp2(s - m_next).astype(BF16)
        acc[g, c] = alpha * acc[g, c] + jnp.dot(vext_ref[:, kb * u:(kb + 1) * u], p, preferred_element_type=F32)
        m[g, c] = m_next

    outs = []
    for g in range(GROUP):
        o_t = jnp.concatenate([acc[g, c][0:HEAD_DIM, :] / acc[g, c][HEAD_DIM:HEAD_DIM + 1, :]
                               for c in range(n_chunks)], axis=1)
        outs.append(o_t.T.astype(BF16))
    o_ref[...] = jnp.concatenate(outs, axis=1)


def _win_attn(q, k, vt, k_c, vt_c, sink, *, batch, seq, ctx, tq):
    nq = seq // tq
    r = tq // WINDOW
    nblk = seq // WINDOW

    def prev_blk(b, i):
        return b * nblk + jnp.maximum(i * r - 1, 0)

    def next_blk(b, i):
        return b * nblk + jnp.minimum((i + 1) * r, nblk - 1)

    return pl.pallas_call(
        functools.partial(_win_kernel, nq=nq),
        out_shape=jax.ShapeDtypeStruct((batch * seq, Q_W), BF16),
        grid=(batch, N_KV, nq),
        in_specs=[pl.BlockSpec(memory_space=pltpu.SMEM),
                  pl.BlockSpec((tq, GROUP * HEAD_DIM), lambda b, h, i: (b * nq + i, h)),
                  pl.BlockSpec((None, WINDOW, HEAD_DIM), lambda b, h, i: (h, prev_blk(b, i), 0)),
                  pl.BlockSpec((None, tq, HEAD_DIM), lambda b, h, i: (h, b * nq + i, 0)),
                  pl.BlockSpec((None, WINDOW, HEAD_DIM), lambda b, h, i: (h, next_blk(b, i), 0)),
                  pl.BlockSpec((None, ctx, HEAD_DIM), lambda b, h, i: (h, b, 0)),
                  pl.BlockSpec((HEAD_DIM, WINDOW), lambda b, h, i: (h, prev_blk(b, i))),
                  pl.BlockSpec((HEAD_DIM, tq), lambda b, h, i: (h, b * nq + i)),
                  pl.BlockSpec((HEAD_DIM, WINDOW), lambda b, h, i: (h, next_blk(b, i))),
                  pl.BlockSpec((HEAD_DIM, ctx), lambda b, h, i: (h, b))],
        out_specs=pl.BlockSpec((tq, GROUP * HEAD_DIM), lambda b, h, i: (b * nq + i, h)),
        scratch_shapes=[pltpu.VMEM((tq + 2 * WINDOW + ctx, HEAD_DIM), BF16),
                        pltpu.VMEM((HEAD_DIM + ONES_ROWS, tq + 2 * WINDOW + ctx), BF16)],
        compiler_params=_cparams(("parallel", "parallel", "parallel")),
        name="win_attn",
    )(sink, q, k, k, k, k_c, vt, vt, vt, vt_c)


ONES_ROWS = 16
Q_CHUNK = 256
K_SUB = 256
QK_AHEAD = 5


def _flash_kernel(sink_ref, q_ref, k_ref, vt_ref, *rest, n_main, has_extra, use_sink):
    if has_extra:
        kx_ref, vtx_ref, o_ref, q2_ref, m_ref, acc_ref, vext_ref, vextx_ref = rest
    else:
        o_ref, q2_ref, m_ref, acc_ref, vext_ref = rest
    hk = pl.program_id(1)
    j = pl.program_id(3)
    tq = q_ref.shape[0]

    @pl.when(j == 0)
    def _():
        for g in range(GROUP):
            q2_ref[g * tq:(g + 1) * tq, :] = q_ref[:, g * HEAD_DIM:(g + 1) * HEAD_DIM]
            if use_sink:
                m_ref[:, g * tq:(g + 1) * tq] = jnp.full((1, tq), sink_ref[hk * GROUP + g] * LOG2E, F32)
        if not use_sink:
            m_ref[...] = jnp.full(m_ref.shape, NEG_BIG, F32)
        acc_ref[0:HEAD_DIM, :] = jnp.zeros((HEAD_DIM, GROUP * tq), F32)
        acc_ref[HEAD_DIM:, :] = jnp.full((ONES_ROWS, GROUP * tq), 1.0 if use_sink else 0.0, F32)
        vext_ref[HEAD_DIM:, :] = jnp.ones((ONES_ROWS, vext_ref.shape[1]), BF16)
        if has_extra:
            vextx_ref[HEAD_DIM:, :] = jnp.ones((ONES_ROWS, vextx_ref.shape[1]), BF16)

    n_chunks = GROUP * tq // Q_CHUNK

    def step(kk_ref, vv_ref, vext):
        tkk = kk_ref.shape[0]
        kb = min(tkk, K_SUB)
        units = [(b, c) for b in range(tkk // kb) for c in range(n_chunks)]
        vext[0:HEAD_DIM, :] = vv_ref[...]

        def scores(u):
            b, c = u
            return _dot_nt(kk_ref[b * kb:(b + 1) * kb, :], q2_ref[c * Q_CHUNK:(c + 1) * Q_CHUNK, :])

        m = [m_ref[:, c * Q_CHUNK:(c + 1) * Q_CHUNK] for c in range(n_chunks)]
        acc = [acc_ref[:, c * Q_CHUNK:(c + 1) * Q_CHUNK] for c in range(n_chunks)]
        ahead = [scores(u) for u in units[:QK_AHEAD]]
        for idx, (b, c) in enumerate(units):
            s = ahead.pop(0)
            if idx + QK_AHEAD < len(units):
                ahead.append(scores(units[idx + QK_AHEAD]))
            m_next = jnp.maximum(m[c], jnp.max(s, axis=0, keepdims=True))
            alpha = jnp.exp2(m[c] - m_next)
            p = jnp.exp2(s - m_next).astype(BF16)
            acc[c] = alpha * acc[c] + jnp.dot(vext[:, b * kb:(b + 1) * kb], p, preferred_element_type=F32)
            m[c] = m_next
        acc_ref[...] = jnp.concatenate(acc, axis=1)
        m_ref[...] = jnp.concatenate(m, axis=1)

    @pl.when(j < n_main)
    def _():
        step(k_ref, vt_ref, vext_ref)

    if has_extra:
        @pl.when(j == n_main)
        def _():
            step(kx_ref, vtx_ref, vextx_ref)

    @pl.when(j == n_main + (1 if has_extra else 0) - 1)
    def _():
        o = (acc_ref[0:HEAD_DIM, :] / acc_ref[HEAD_DIM:HEAD_DIM + 1, :]).T
        for g in range(GROUP):
            o_ref[:, g * HEAD_DIM:(g + 1) * HEAD_DIM] = o[g * tq:(g + 1) * tq, :].astype(BF16)


def _flash(q_arr, k_arr, vt_arr, kx_arr, vtx_arr, sink, *, batch, seq_q, seq_k, seq_x, tq, tk, branch, use_sink):
    nq = seq_q // tq
    n_main = seq_k // tk
    has_extra = kx_arr is not None
    n_steps = n_main + (1 if has_extra else 0)
    qc = branch * N_KV
    kr = branch * N_KV

    def jj(j):
        return jnp.minimum(j, n_main - 1)

    in_specs = [pl.BlockSpec(memory_space=pltpu.SMEM),
                pl.BlockSpec((tq, GROUP * HEAD_DIM), lambda b, h, i, j: (b * nq + i, qc + h)),
                pl.BlockSpec((None, tk, HEAD_DIM), lambda b, h, i, j: (kr + h, b * n_main + jj(j), 0)),
                pl.BlockSpec((HEAD_DIM, tk), lambda b, h, i, j: (kr + h, b * n_main + jj(j)))]
    args = [sink, q_arr, k_arr, vt_arr]
    rows = GROUP * tq
    scratch = [pltpu.VMEM((rows, HEAD_DIM), BF16),
               pltpu.VMEM((1, rows), F32),
               pltpu.VMEM((HEAD_DIM + ONES_ROWS, rows), F32),
               pltpu.VMEM((HEAD_DIM + ONES_ROWS, tk), BF16)]
    if has_extra:
        in_specs += [pl.BlockSpec((None, seq_x, HEAD_DIM), lambda b, h, i, j: (kr + h, b, 0)),
                     pl.BlockSpec((HEAD_DIM, seq_x), lambda b, h, i, j: (kr + h, b))]
        args += [kx_arr, vtx_arr]
        scratch += [pltpu.VMEM((HEAD_DIM + ONES_ROWS, seq_x), BF16)]
    return pl.pallas_call(
        functools.partial(_flash_kernel, n_main=n_main, has_extra=has_extra, use_sink=use_sink),
        out_shape=jax.ShapeDtypeStruct((batch * seq_q, Q_W), BF16),
        grid=(batch, N_KV, nq, n_steps),
        in_specs=in_specs,
        out_specs=pl.BlockSpec((tq, GROUP * HEAD_DIM), lambda b, h, i, j: (b * nq + i, h)),
        scratch_shapes=scratch,
        compiler_params=_cparams(("parallel", "parallel", "parallel", "arbitrary")),
        name="flash",
    )(*args)


def _merge_kernel(h_ref, ya_ref, yb_ref, yc_ref, yd_ref, wga_ref, wgb_ref, wgc_ref, wgd_ref,
                  wb_ref, bg_ref, o_ref):
    h = h_ref[...]
    m = None
    for k, (y_ref, wg_ref) in enumerate(((ya_ref, wga_ref), (yb_ref, wgb_ref), (yc_ref, wgc_ref), (yd_ref, wgd_ref))):
        logits = jnp.dot(h, wg_ref[...], preferred_element_type=F32) + bg_ref[k:k + 1, :]
        t = _sigmoid(logits) * jnp.dot(y_ref[...], wb_ref[k], preferred_element_type=F32)
        m = t if m is None else m + t
    o_ref[...] = m.astype(BF16)


def _merge(h, ys, wg, wb, bg, *, layer, tm, tn):
    t = h.shape[0]
    nct = D_MODEL // tn

    def y():
        return pl.BlockSpec((tm, W_BR), lambda i, j: (i, 0))

    def g(k):
        return pl.BlockSpec((None, D_MODEL, tn), lambda i, j: (layer, 0, OFF_G // tn + k * nct + j))

    return pl.pallas_call(
        _merge_kernel,
        out_shape=jax.ShapeDtypeStruct((t, D_MODEL), BF16),
        grid=(t // tm, nct),
        in_specs=[pl.BlockSpec((tm, D_MODEL), lambda i, j: (i, 0)), y(), y(), y(), y(),
                  g(0), g(1), g(2), g(3),
                  pl.BlockSpec((None, N_BRANCH, W_BR, tn), lambda i, j: (layer, 0, 0, j)),
                  pl.BlockSpec((N_BRANCH, tn), lambda i, j: (0, j))],
        out_specs=pl.BlockSpec((tm, tn), lambda i, j: (i, j)),
        compiler_params=_cparams(("parallel", "arbitrary")),
        name="merge",
    )(h, *ys, wg, wg, wg, wg, wb, bg)


def _resid_kernel(a_ref, w_ref, x_ref, gate_ref, o_ref, *scratch, nk):
    part = jnp.dot(a_ref[...], w_ref[...], preferred_element_type=F32)
    if nk == 1:
        o_ref[...] = x_ref[...] + gate_ref[0] * part
        return
    (acc_ref,) = scratch
    k = pl.program_id(1)

    @pl.when(k == 0)
    def _():
        acc_ref[...] = part

    if nk > 2:
        @pl.when((k > 0) & (k < nk - 1))
        def _():
            acc_ref[...] += part

    @pl.when(k == nk - 1)
    def _():
        o_ref[...] = x_ref[...] + gate_ref[0] * (acc_ref[...] + part)


def _resid(a, w, x, gate, *, layer, seq, tm, tk):
    t, k = a.shape
    nbt = seq // tm
    nk = k // tk
    return pl.pallas_call(
        functools.partial(_resid_kernel, nk=nk),
        out_shape=jax.ShapeDtypeStruct((t, D_MODEL), F32),
        grid=(t // tm, nk),
        in_specs=[pl.BlockSpec((tm, tk), lambda i, j: (i, j)),
                  pl.BlockSpec((None, tk, D_MODEL), lambda i, j: (layer, j, 0)),
                  pl.BlockSpec((tm, D_MODEL), lambda i, j: (i, 0)),
                  pl.BlockSpec((1, 1, D_MODEL), lambda i, j: (i // nbt, 0, 0))],
        out_specs=pl.BlockSpec((tm, D_MODEL), lambda i, j: (i, 0)),
        scratch_shapes=[pltpu.VMEM((tm, D_MODEL), F32)] if nk > 1 else [],
        compiler_params=_cparams(("parallel", "arbitrary")),
        name="resid",
    )(a, w, x, gate)


def _resid_norm_kernel(a_ref, w_ref, x_ref, gate_ref, gn_ref, o_ref, acc_ref, *, nk):
    k = pl.program_id(1)
    part = jnp.dot(a_ref[...], w_ref[...], preferred_element_type=F32)

    @pl.when(k == 0)
    def _():
        acc_ref[...] = part

    @pl.when(k > 0)
    def _():
        acc_ref[...] += part

    @pl.when(k == nk - 1)
    def _():
        gate = gate_ref[0]
        gn = gn_ref[...]

        def body(i, carry):
            rows = pl.ds(pl.multiple_of(i * NORM_CHUNK, NORM_CHUNK), NORM_CHUNK)
            y = x_ref[rows, :] + gate * acc_ref[rows, :]
            o_ref[rows, :] = y * lax.rsqrt(jnp.mean(y * y, axis=-1, keepdims=True) + EPS) * gn
            return carry

        lax.fori_loop(0, o_ref.shape[0] // NORM_CHUNK, body, 0, unroll=NORM_UNROLL)


def _resid_norm(a, w, x, gate, gn, *, layer, seq, tm, tk):
    t, k = a.shape
    nbt = seq // tm
    nk = k // tk
    return pl.pallas_call(
        functools.partial(_resid_norm_kernel, nk=nk),
        out_shape=jax.ShapeDtypeStruct((t, D_MODEL), F32),
        grid=(t // tm, nk),
        in_specs=[pl.BlockSpec((tm, tk), lambda i, j: (i, j)),
                  pl.BlockSpec((None, tk, D_MODEL), lambda i, j: (layer, j, 0)),
                  pl.BlockSpec((tm, D_MODEL), lambda i, j: (i, 0)),
                  pl.BlockSpec((1, 1, D_MODEL), lambda i, j: (i // nbt, 0, 0)),
                  pl.BlockSpec((1, D_MODEL), lambda i, j: (0, 0))],
        out_specs=pl.BlockSpec((tm, D_MODEL), lambda i, j: (i, 0)),
        scratch_shapes=[pltpu.VMEM((tm, D_MODEL), F32)],
        compiler_params=_cparams(("parallel", "arbitrary")),
        name="resid_norm",
    )(a, w, x, gate, gn)


AHEAD_PARTS = 8


def _ffn_in_kernel(x_ref, g_ref, sc_ref, sh_ref, wa_ref, wb_ref, o_ref, h_ref):
    i = pl.program_id(0)
    j = pl.program_id(1)
    part = x_ref.shape[0] // AHEAD_PARTS

    def normalise_part():
        gs = g_ref[...] * (1.0 + sc_ref[0])
        sh = sh_ref[0]
        base = jnp.minimum(j, AHEAD_PARTS - 1) * part
        for c in range(part // NORM_CHUNK):
            rows = pl.ds(pl.multiple_of(base + c * NORM_CHUNK, NORM_CHUNK), NORM_CHUNK)
            x = x_ref[rows, :]
            r = lax.rsqrt(jnp.mean(x * x, axis=-1, keepdims=True) + EPS)
            h_ref[i % 2, rows, :] = (x * r * gs + sh).astype(BF16)

    @pl.when(i == 0)
    def _():
        normalise_part()

    @pl.when(i > 0)
    def _():
        h = h_ref[(i + 1) % 2]
        a = jnp.dot(h, wa_ref[...], preferred_element_type=F32)
        b = jnp.dot(h, wb_ref[...], preferred_element_type=F32)
        o_ref[...] = (a * _sigmoid(a) * b).astype(BF16)
        normalise_part()


def _ffn_in(x, g, sc, sh, w, *, layer, seq, tm, tn):
    t = x.shape[0]
    nbt = seq // tm
    nct = FFN_HIDDEN // tn
    n = t // tm
    assert nct >= AHEAD_PARTS and tm % (AHEAD_PARTS * NORM_CHUNK) == 0

    def tile(i):
        return jnp.minimum(i, n - 1)

    return pl.pallas_call(
        _ffn_in_kernel,
        out_shape=jax.ShapeDtypeStruct((t, FFN_HIDDEN), BF16),
        grid=(n + 1, nct),
        in_specs=[pl.BlockSpec((tm, D_MODEL), lambda i, j: (tile(i), 0)),
                  pl.BlockSpec((1, D_MODEL), lambda i, j: (0, 0)),
                  pl.BlockSpec((1, 1, D_MODEL), lambda i, j: (tile(i) // nbt, 0, 0)),
                  pl.BlockSpec((1, 1, D_MODEL), lambda i, j: (tile(i) // nbt, 0, 0)),
                  pl.BlockSpec((None, D_MODEL, tn), lambda i, j: (layer, 0, j)),
                  pl.BlockSpec((None, D_MODEL, tn), lambda i, j: (layer, 0, nct + j))],
        out_specs=pl.BlockSpec((tm, tn), lambda i, j: (jnp.maximum(i - 1, 0), jnp.where(i == 0, 0, j))),
        scratch_shapes=[pltpu.VMEM((2, tm, D_MODEL), BF16)],
        compiler_params=_cparams(("arbitrary", "arbitrary")),
        name="ffn_in",
    )(x, g, sc, sh, w, w)


def _rope_tables(seq):
    pos = jnp.arange(seq)
    row = (pos // GRID_W).astype(F32)
    col = (pos % GRID_W).astype(F32)
    inv = ROPE_THETA ** (-jnp.arange(0, ROT_AXIS, 2, dtype=F32) / ROT_AXIS)
    ar = row[:, None] * inv
    ac = col[:, None] * inv
    ang = jnp.concatenate([ar, ac, ar, ac], axis=-1)
    first = jnp.arange(HEAD_DIM) < HEAD_DIM // 2
    return jnp.cos(ang), jnp.where(first, -jnp.sin(ang), jnp.sin(ang))


def _pair_split(v):
    quarter = HEAD_DIM // 4
    q = v.reshape(v.shape[:-1] + (v.shape[-1] // HEAD_DIM, 4, quarter))
    q = jnp.concatenate([q[..., 0:1, :], q[..., 2:3, :], q[..., 1:2, :], q[..., 3:4, :]], axis=-2)
    return q.reshape(v.shape)


def _pair_split_heads(w):
    return jnp.concatenate([w[..., :OFF_BQ], _pair_split(w[..., OFF_BQ:OFF_BV]), w[..., OFF_BV:OFF_DQ],
                            _pair_split(w[..., OFF_DQ:OFF_DV]), w[..., OFF_DV:]], axis=-1)


def _tile(n, pref):
    return pref if n % pref == 0 else n


def _forward(x, c, ctx, c_ctx, w_ada, b_ada, norm_mix, norm_ffn, w_in, b_gate, conv_a_w, sink_b,
             qk_norm_q, qk_norm_k, conv_c_w, conv_c_b, ln_c_g, ln_c_b, w_branch, w_out,
             w_ffn_in, w_ffn_out, norm_final):
    batch, seq, _ = x.shape
    n_ctx = ctx.shape[1]
    depth = w_in.shape[0]
    xs = x.reshape(batch * seq, D_MODEL)
    xc = ctx.reshape(batch * n_ctx, D_MODEL)

    tm = _tile(seq, 1024)
    tm_rows = _tile(seq, 512)
    tm_conv_a = _tile(seq, 512)
    tm_conv_c = _tile(seq, 512)
    tq_win = _tile(seq, 512)
    tq = _tile(seq, 1024)
    tk = _tile(seq, 4096)

    rope_x = _rope_tables(seq)
    rope_c = (jnp.ones((n_ctx, HEAD_DIM), F32), jnp.zeros((n_ctx, HEAD_DIM), F32))
    cvec = jnp.zeros((ADA_ROWS, D_MODEL), F32).at[:batch].set(c).at[batch].set(c_ctx)

    def row(v):
        return v.reshape(1, -1)

    w_proj = w_gate = _pair_split_heads(w_in).astype(BF16)
    w_br = w_branch.astype(BF16)
    w_o = w_out.astype(BF16)
    w_f1 = w_ffn_in.astype(BF16)
    w_f2 = w_ffn_out.astype(BF16)

    def mixers(pf, q, k, vt, k_c, vt_c, l, *, s, tma, tmc, is_ctx):
        y_a = _conv_a(pf, conv_a_w[l], seq=s, tm=tma)
        y_c = _conv_c(pf, conv_c_w[l], row(conv_c_b[l]), row(ln_c_g[l]), row(ln_c_b[l]), seq=s, tm=tmc)
        if is_ctx:
            y_b = _flash(q, k, vt, None, None, sink_b[l], batch=batch, seq_q=s, seq_k=s, seq_x=0, tq=s, tk=s,
                         branch=0, use_sink=True)
            y_d = _flash(q, k, vt, None, None, sink_b[l], batch=batch, seq_q=s, seq_k=s, seq_x=0, tq=s, tk=s,
                         branch=1, use_sink=False)
        else:
            y_b = _win_attn(q, k, vt, k_c, vt_c, sink_b[l], batch=batch, seq=s, ctx=n_ctx, tq=tq_win)
            y_d = _flash(q, k, vt, k_c, vt_c, sink_b[l], batch=batch, seq_q=s, seq_k=s, seq_x=n_ctx,
                         tq=tq, tk=tk, branch=1, use_sink=False)
        return [y_a, y_b, y_c, y_d]

    for l in range(depth):
        last = l == depth - 1
        mod = _ada(cvec, w_ada, row(b_ada[l]), layer=l)
        mx = mod[:batch].reshape(batch, 1, 6, D_MODEL)
        mcx = jnp.broadcast_to(mod[batch].reshape(1, 1, 6, D_MODEL), (batch, 1, 6, D_MODEL))
        sh_m, sc_m, g_m, sh_f, sc_f, g_f = (mx[:, :, k] for k in range(6))
        csh_m, csc_m, cg_m, csh_f, csc_f, cg_f = (mcx[:, :, k] for k in range(6))
        gq, gk = row(_pair_split(qk_norm_q[l])), row(_pair_split(qk_norm_k[l]))

        pf_c, q_c, h_c, k_c, vt_c = _inproj(xc, row(norm_mix[l]), csc_m, csh_m, w_proj, *rope_c, gq, gk,
                                            layer=l, seq=n_ctx, tm=n_ctx)
        pf, q, h, k, vt = _inproj(xs, row(norm_mix[l]), sc_m, sh_m, w_proj, *rope_x, gq, gk,
                                  layer=l, seq=seq, tm=tm)

        ys = mixers(pf, q, k, vt, k_c, vt_c, l, s=seq, tma=tm_conv_a, tmc=tm_conv_c, is_ctx=False)
        m = _merge(h, ys, w_gate, w_br, b_gate[l], layer=l, tm=tm, tn=512)
        xs = _resid(m, w_o, xs, g_m, layer=l, seq=seq, tm=tm_rows, tk=D_MODEL)

        if not last:
            ys_c = mixers(pf_c, q_c, k_c, vt_c, None, None, l, s=n_ctx, tma=n_ctx, tmc=n_ctx, is_ctx=True)
            m_c = _merge(h_c, ys_c, w_gate, w_br, b_gate[l], layer=l, tm=n_ctx, tn=512)
            xc = _resid(m_c, w_o, xc, cg_m, layer=l, seq=n_ctx, tm=n_ctx, tk=D_MODEL)
            hid_c = _ffn_in(xc, row(norm_ffn[l]), csc_f, csh_f, w_f1, layer=l, seq=n_ctx, tm=n_ctx, tn=512)
            xc = _resid(hid_c, w_f2, xc, cg_f, layer=l, seq=n_ctx, tm=n_ctx, tk=FFN_HIDDEN // 2)

        hid = _ffn_in(xs, row(norm_ffn[l]), sc_f, sh_f, w_f1, layer=l, seq=seq, tm=tm, tn=512)
        if last:
            xs = _resid_norm(hid, w_f2, xs, g_f, row(norm_final), layer=l, seq=seq, tm=tm_rows,
                             tk=FFN_HIDDEN // 2)
        else:
            xs = _resid(hid, w_f2, xs, g_f, layer=l, seq=seq, tm=tm_rows, tk=FFN_HIDDEN // 2)

    return xs.reshape(batch, seq, D_MODEL)


def kernel(x, c, ctx, c_ctx, w_ada, b_ada, norm_mix, norm_ffn, w_in, b_gate, conv_a_w, sink_b, qk_norm_q,
           qk_norm_k, conv_c_w, conv_c_b, ln_c_g, ln_c_b, w_branch, w_out, w_ffn_in, w_ffn_out, norm_final):
    return _forward(x, c, ctx, c_ctx, w_ada, b_ada, norm_mix, norm_ffn, w_in, b_gate, conv_a_w, sink_b,
                    qk_norm_q, qk_norm_k, conv_c_w, conv_c_b, ln_c_g, ln_c_b, w_branch, w_out,
                    w_ffn_in, w_ffn_out, norm_final)
```

```python
import functools

import jax
import jax.numpy as jnp
from jax import lax
from jax.experimental import pallas as pl
from jax.experimental.pallas import tpu as pltpu

F32 = jnp.float32
BF16 = jnp.bfloat16

D_MODEL = 2048
N_BRANCH = 4
W_BR = D_MODEL // 4
HEAD_DIM = 128
N_Q = W_BR // HEAD_DIM
N_KV = N_Q // 2
GROUP = N_Q // N_KV
Q_W = N_Q * HEAD_DIM
KV_W = N_KV * HEAD_DIM
CONV_A = 3
CONV_C = 31
WINDOW = 128
GRID_W = 64
ROT_AXIS = HEAD_DIM // 2
ROPE_THETA = 10000.0
FFN_HIDDEN = 5632
EPS = 1e-6
ATTN_SCALE = HEAD_DIM ** -0.5
LOG2E = 1.4426950408889634
Q_SCALE = ATTN_SCALE * LOG2E
NEG_BIG = -1e30

OFF_A = 0
OFF_BQ = OFF_A + 3 * W_BR
OFF_BK = OFF_BQ + Q_W
OFF_BV = OFF_BK + KV_W
OFF_C = OFF_BV + KV_W
OFF_DQ = OFF_C + 2 * W_BR
OFF_DK = OFF_DQ + Q_W
OFF_DV = OFF_DK + KV_W
OFF_G = OFF_DV + KV_W

PF_W = 3 * W_BR + 2 * W_BR
PQ_W = 2 * (Q_W + 2 * KV_W)
TN = 512
N_PF_TILES = PF_W // TN
N_PROJ_TILES = (PF_W + PQ_W) // TN

VMEM_LIMIT = 56 * 1024 * 1024


def _cparams(sem):
    return pltpu.CompilerParams(dimension_semantics=sem, vmem_limit_bytes=VMEM_LIMIT)


def _sigmoid(z):
    return 1.0 / (1.0 + jnp.exp(-z))


def _ada_kernel(c_ref, w_ref, b_ref, o_ref):
    c = c_ref[...]
    s = (c * _sigmoid(c)).astype(BF16)
    o_ref[...] = jnp.dot(s, w_ref[...].astype(BF16), preferred_element_type=F32) + b_ref[...]


ADA_ROWS = 16


def _ada(cvec, w, b, *, layer):
    n = w.shape[2]
    tn = 1024
    return pl.pallas_call(
        _ada_kernel,
        out_shape=jax.ShapeDtypeStruct((ADA_ROWS, n), F32),
        grid=(n // tn,),
        in_specs=[pl.BlockSpec((ADA_ROWS, D_MODEL), lambda j: (0, 0)),
                  pl.BlockSpec((None, D_MODEL, tn), lambda j: (layer, 0, j)),
                  pl.BlockSpec((1, tn), lambda j: (0, j))],
        out_specs=pl.BlockSpec((ADA_ROWS, tn), lambda j: (0, j)),
        compiler_params=_cparams(("arbitrary",)),
        name="ada",
    )(cvec, w, b)


def _rope(xh, cos, sin_signed):
    return xh * cos + pltpu.roll(xh, HEAD_DIM // 2, 1) * sin_signed


def _head_norm(xh, g):
    return xh * lax.rsqrt(jnp.mean(xh * xh, axis=-1, keepdims=True) + EPS) * g


NORM_CHUNK = 16
NORM_UNROLL = 8


def _norm_mod_rows(x_ref, g_ref, sc_ref, sh_ref, h_ref):
    gs = g_ref[...] * (1.0 + sc_ref[0])
    sh = sh_ref[0]

    def body(i, carry):
        rows = pl.ds(pl.multiple_of(i * NORM_CHUNK, NORM_CHUNK), NORM_CHUNK)
        x = x_ref[rows, :]
        r = lax.rsqrt(jnp.mean(x * x, axis=-1, keepdims=True) + EPS)
        h_ref[rows, :] = (x * r * gs + sh).astype(BF16)
        return carry

    lax.fori_loop(0, x_ref.shape[0] // NORM_CHUNK, body, 0, unroll=NORM_UNROLL)


def _inproj_kernel(x_ref, g_ref, sc_ref, sh_ref, w_ref, cos_ref, sin_ref, gq_ref, gk_ref,
                   pf_ref, q_ref, h_ref, k_ref, vt_ref):
    j = pl.program_id(1)

    @pl.when(j == 0)
    def _():
        _norm_mod_rows(x_ref, g_ref, sc_ref, sh_ref, h_ref)

    acc = jnp.dot(h_ref[...], w_ref[...], preferred_element_type=F32)

    @pl.when(j < N_PF_TILES)
    def _():
        pf_ref[...] = acc.astype(BF16)

    def rope(xh):
        return _rope(xh, cos_ref[...], sin_ref[...])

    def head(k):
        return acc[:, k * HEAD_DIM:(k + 1) * HEAD_DIM]

    def put_q(k, v):
        q_ref[:, k * HEAD_DIM:(k + 1) * HEAD_DIM] = v.astype(BF16)

    @pl.when(j == N_PF_TILES)
    def _():
        for k in range(N_Q):
            put_q(k, rope(head(k)) * Q_SCALE)

    @pl.when(j == N_PF_TILES + 1)
    def _():
        for k in range(N_KV):
            k_ref[k] = rope(head(k)).astype(BF16)
        vt_ref[0:KV_W, :] = acc[:, KV_W:].T.astype(BF16)

    @pl.when(j == N_PF_TILES + 2)
    def _():
        for k in range(N_Q):
            put_q(k, rope(_head_norm(head(k), gq_ref[...])) * Q_SCALE)

    @pl.when(j == N_PF_TILES + 3)
    def _():
        for k in range(N_KV):
            k_ref[N_KV + k] = rope(_head_norm(head(k), gk_ref[...])).astype(BF16)
        vt_ref[KV_W:, :] = acc[:, KV_W:].T.astype(BF16)


def _proj_block(j):
    n_a, n_b, n_c = 3 * W_BR // TN, (Q_W + 2 * KV_W) // TN, 2 * W_BR // TN
    return jnp.where(j < n_a, j, jnp.where(j < n_a + n_c, j + n_b, jnp.where(j < n_a + n_c + n_b, j - n_c, j)))


def _inproj(x, g, sc, sh, w, cos, sin_signed, gq, gk, *, layer, seq, tm):
    t = x.shape[0]
    nbt = seq // tm
    return pl.pallas_call(
        _inproj_kernel,
        out_shape=(jax.ShapeDtypeStruct((t, PF_W), BF16),
                   jax.ShapeDtypeStruct((t, 2 * Q_W), BF16),
                   jax.ShapeDtypeStruct((t, D_MODEL), BF16),
                   jax.ShapeDtypeStruct((2 * N_KV, t, HEAD_DIM), BF16),
                   jax.ShapeDtypeStruct((2 * KV_W, t), BF16)),
        grid=(t // tm, N_PROJ_TILES),
        in_specs=[pl.BlockSpec((tm, D_MODEL), lambda i, j: (i, 0)),
                  pl.BlockSpec((1, D_MODEL), lambda i, j: (0, 0)),
                  pl.BlockSpec((1, 1, D_MODEL), lambda i, j: (i // nbt, 0, 0)),
                  pl.BlockSpec((1, 1, D_MODEL), lambda i, j: (i // nbt, 0, 0)),
                  pl.BlockSpec((None, D_MODEL, TN), lambda i, j: (layer, 0, _proj_block(j))),
                  pl.BlockSpec((tm, HEAD_DIM), lambda i, j: (i % nbt, 0)),
                  pl.BlockSpec((tm, HEAD_DIM), lambda i, j: (i % nbt, 0)),
                  pl.BlockSpec((1, HEAD_DIM), lambda i, j: (0, 0)),
                  pl.BlockSpec((1, HEAD_DIM), lambda i, j: (0, 0))],
        out_specs=(pl.BlockSpec((tm, TN), lambda i, j: (i, jnp.minimum(j, N_PF_TILES - 1))),
                   pl.BlockSpec((tm, Q_W), lambda i, j: (i, jnp.where(j >= N_PF_TILES + 2, 1, 0))),
                   pl.BlockSpec((tm, D_MODEL), lambda i, j: (i, 0)),
                   pl.BlockSpec((2 * N_KV, tm, HEAD_DIM), lambda i, j: (0, i, 0)),
                   pl.BlockSpec((2 * KV_W, tm), lambda i, j: (0, i))),
        compiler_params=_cparams(("parallel", "arbitrary")),
        name="inproj",
    )(x, g, sc, sh, w, cos, sin_signed, gq, gk)


HALO_A = 16


def _conv_a_kernel(bg_ref, cg_ref, h_ref, cgp_ref, hp_ref, cgn_ref, hn_ref, w_ref, o_ref, u_ref, *, nbt):
    it = pl.program_id(0) % nbt
    tm = bg_ref.shape[0]
    def gated(gate_ref, val_ref):
        return gate_ref[...].astype(F32) * val_ref[...].astype(F32)

    u_ref[HALO_A:HALO_A + tm, :] = gated(cg_ref, h_ref)
    u_ref[0:HALO_A, :] = jnp.where(it > 0, gated(cgp_ref, hp_ref), 0.0)
    u_ref[HALO_A + tm:, :] = jnp.where(it < nbt - 1, gated(cgn_ref, hn_ref), 0.0)
    y = (w_ref[0:1, :] * u_ref[HALO_A - 1:HALO_A - 1 + tm, :]
         + w_ref[1:2, :] * u_ref[HALO_A:HALO_A + tm, :]
         + w_ref[2:3, :] * u_ref[HALO_A + 1:HALO_A + 1 + tm, :])
    o_ref[...] = (bg_ref[...].astype(F32) * y).astype(BF16)


def _conv_a(pf, w, *, seq, tm):
    t = pf.shape[0]
    nbt = seq // tm
    r = tm // HALO_A
    last = t // HALO_A - 1

    def cur(c):
        return pl.BlockSpec((tm, W_BR), lambda i: (i, c))

    def prev(c):
        return pl.BlockSpec((HALO_A, W_BR), lambda i: (jnp.maximum(i * r - 1, 0), c))

    def nxt(c):
        return pl.BlockSpec((HALO_A, W_BR), lambda i: (jnp.minimum((i + 1) * r, last), c))

    return pl.pallas_call(
        functools.partial(_conv_a_kernel, nbt=nbt),
        out_shape=jax.ShapeDtypeStruct((t, W_BR), BF16),
        grid=(t // tm,),
        in_specs=[cur(0), cur(1), cur(2), prev(1), prev(2), nxt(1), nxt(2),
                  pl.BlockSpec((CONV_A, W_BR), lambda i: (0, 0))],
        out_specs=pl.BlockSpec((tm, W_BR), lambda i: (i, 0)),
        scratch_shapes=[pltpu.VMEM((tm + 2 * HALO_A, W_BR), F32)],
        compiler_params=_cparams(("parallel",)),
        name="conv_a",
    )(pf, pf, pf, pf, pf, pf, pf, w)


HALO_C = 16
ROWS_C = 64
SUBLANES = 8


def _conv_c_kernel(v_ref, g_ref, vp_ref, gp_ref, vn_ref, gn_ref, w_ref, b_ref, lg_ref, lb_ref,
                   o_ref, u_ref, us_ref, *, nbt):
    it = pl.program_id(0) % nbt
    tm = v_ref.shape[0]

    def glu(val_ref, gate_ref):
        return val_ref[...].astype(F32) * _sigmoid(gate_ref[...].astype(F32))

    u_ref[HALO_C:HALO_C + tm, :] = glu(v_ref, g_ref)
    u_ref[0:HALO_C, :] = jnp.where(it > 0, glu(vp_ref, gp_ref), 0.0)
    u_ref[HALO_C + tm:, :] = jnp.where(it < nbt - 1, glu(vn_ref, gn_ref), 0.0)
    n_sh = us_ref.shape[1]
    for r in range(1, SUBLANES):
        us_ref[r - 1] = u_ref[r:r + n_sh, :]
    pad = CONV_C // 2

    def tap(row):
        a, r = divmod(row, SUBLANES)
        if r == 0:
            return u_ref[a * SUBLANES:a * SUBLANES + ROWS_C, :]
        return us_ref[r - 1, a * SUBLANES:a * SUBLANES + ROWS_C, :]

    for c in range(tm // ROWS_C):
        base = HALO_C + c * ROWS_C - pad
        acc = w_ref[0:1, :] * tap(base)
        for k in range(1, CONV_C):
            acc = acc + w_ref[k:k + 1, :] * tap(base + k)
        acc = acc + b_ref[...]
        mu = jnp.mean(acc, axis=-1, keepdims=True)
        xc = acc - mu
        var = jnp.mean(xc * xc, axis=-1, keepdims=True)
        y = xc * lax.rsqrt(var + EPS) * lg_ref[...] + lb_ref[...]
        o_ref[c * ROWS_C:(c + 1) * ROWS_C, :] = (y * _sigmoid(y)).astype(BF16)


def _conv_c(pf, w, b, lg, lb, *, seq, tm):
    t = pf.shape[0]
    nbt = seq // tm
    r = tm // HALO_C
    last = t // HALO_C - 1
    c0 = 3

    def cur(c):
        return pl.BlockSpec((tm, W_BR), lambda i: (i, c))

    def prev(c):
        return pl.BlockSpec((HALO_C, W_BR), lambda i: (jnp.maximum(i * r - 1, 0), c))

    def nxt(c):
        return pl.BlockSpec((HALO_C, W_BR), lambda i: (jnp.minimum((i + 1) * r, last), c))

    def vec():
        return pl.BlockSpec((1, W_BR), lambda i: (0, 0))

    return pl.pallas_call(
        functools.partial(_conv_c_kernel, nbt=nbt),
        out_shape=jax.ShapeDtypeStruct((t, W_BR), BF16),
        grid=(t // tm,),
        in_specs=[cur(c0), cur(c0 + 1), prev(c0), prev(c0 + 1), nxt(c0), nxt(c0 + 1),
                  pl.BlockSpec((CONV_C, W_BR), lambda i: (0, 0)), vec(), vec(), vec()],
        out_specs=pl.BlockSpec((tm, W_BR), lambda i: (i, 0)),
        scratch_shapes=[pltpu.VMEM((tm + 2 * HALO_C, W_BR), F32),
                        pltpu.VMEM((SUBLANES - 1, tm + 2 * HALO_C - SUBLANES, W_BR), F32)],
        compiler_params=_cparams(("parallel",)),
        name="conv_c",
    )(pf, pf, pf, pf, pf, pf, w, b, lg, lb)


def _dot_nt(a, b):
    return lax.dot_general(a, b, (((1,), (1,)), ((), ())), preferred_element_type=F32)


WIN_UNIT = 2 * WINDOW
WIN_AHEAD = 5


def _win_kernel(sink_ref, q_ref, kp_ref, kc_ref, kn_ref, kx_ref, vp_ref, vc_ref, vn_ref, vx_ref, o_ref,
                kk_ref, vext_ref, *, nq):
    hk = pl.program_id(1)
    i = pl.program_id(2)
    tq = q_ref.shape[0]
    n_loc = tq + 2 * WINDOW
    n_all = kk_ref.shape[0]
    u = WIN_UNIT
    kk_ref[0:WINDOW, :] = kp_ref[...]
    kk_ref[WINDOW:WINDOW + tq, :] = kc_ref[...]
    kk_ref[WINDOW + tq:n_loc, :] = kn_ref[...]
    kk_ref[n_loc:, :] = kx_ref[...]
    vext_ref[0:HEAD_DIM, 0:WINDOW] = vp_ref[...]
    vext_ref[0:HEAD_DIM, WINDOW:WINDOW + tq] = vc_ref[...]
    vext_ref[0:HEAD_DIM, WINDOW + tq:n_loc] = vn_ref[...]
    vext_ref[0:HEAD_DIM, n_loc:] = vx_ref[...]
    vext_ref[HEAD_DIM:, :] = jnp.ones((ONES_ROWS, n_all), BF16)

    key = lax.broadcasted_iota(jnp.int32, (u, u), 0)
    qry = lax.broadcasted_iota(jnp.int32, (u, u), 1)
    n_loc_blocks = n_loc // u
    n_chunks = tq // u

    def mask(c, kb):
        ok = jnp.abs((kb - c) * u - WINDOW + key - qry) <= WINDOW
        if kb == 0:
            ok = ok & (key >= jnp.where(i > 0, 0, WINDOW))
        if kb == n_loc_blocks - 1:
            ok = ok & (key < jnp.where(i < nq - 1, u, u - WINDOW))
        return ok

    units = []
    for g in range(GROUP):
        for c in range(n_chunks):
            units += [(g, c, kb) for kb in (c, c + 1)] + [(g, c, kb) for kb in range(n_loc_blocks, n_all // u)]

    def scores(unit):
        g, c, kb = unit
        return _dot_nt(kk_ref[kb * u:(kb + 1) * u, :], q_ref[c * u:(c + 1) * u, g * HEAD_DIM:(g + 1) * HEAD_DIM])

    m, acc = {}, {}
    for g in range(GROUP):
        sink = sink_ref[hk * GROUP + g] * LOG2E
        for c in range(n_chunks):
            m[g, c] = jnp.full((1, u), sink, F32)
            acc[g, c] = jnp.concatenate([jnp.zeros((HEAD_DIM, u), F32), jnp.ones((ONES_ROWS, u), F32)], axis=0)

    ahead = [scores(unit) for unit in units[:WIN_AHEAD]]
    for idx, (g, c, kb) in enumerate(units):
        s = ahead.pop(0)
        if idx + WIN_AHEAD < len(units):
            ahead.append(scores(units[idx + WIN_AHEAD]))
        if kb < n_loc_blocks:
            s = jnp.where(mask(c, kb), s, NEG_BIG)
        m_next = jnp.maximum(m[g, c], jnp.max(s, axis=0, keepdims=True))
        alpha = jnp.exp2(m[g, c] - m_next)
        p = jnp.exp2(s - m_next).astype(BF16)
        acc[g, c] = alpha * acc[g, c] + jnp.dot(vext_ref[:, kb * u:(kb + 1) * u], p, preferred_element_type=F32)
        m[g, c] = m_next

    outs = []
    for g in range(GROUP):
        o_t = jnp.concatenate([acc[g, c][0:HEAD_DIM, :] / acc[g, c][HEAD_DIM:HEAD_DIM + 1, :]
                               for c in range(n_chunks)], axis=1)
        outs.append(o_t.T.astype(BF16))
    o_ref[...] = jnp.concatenate(outs, axis=1)


def _win_attn(q, k, vt, k_c, vt_c, sink, *, batch, seq, ctx, tq):
    nq = seq // tq
    r = tq // WINDOW
    nblk = seq // WINDOW

    def prev_blk(b, i):
        return b * nblk + jnp.maximum(i * r - 1, 0)

    def next_blk(b, i):
        return b * nblk + jnp.minimum((i + 1) * r, nblk - 1)

    return pl.pallas_call(
        functools.partial(_win_kernel, nq=nq),
        out_shape=jax.ShapeDtypeStruct((batch * seq, Q_W), BF16),
        grid=(batch, N_KV, nq),
        in_specs=[pl.BlockSpec(memory_space=pltpu.SMEM),
                  pl.BlockSpec((tq, GROUP * HEAD_DIM), lambda b, h, i: (b * nq + i, h)),
                  pl.BlockSpec((None, WINDOW, HEAD_DIM), lambda b, h, i: (h, prev_blk(b, i), 0)),
                  pl.BlockSpec((None, tq, HEAD_DIM), lambda b, h, i: (h, b * nq + i, 0)),
                  pl.BlockSpec((None, WINDOW, HEAD_DIM), lambda b, h, i: (h, next_blk(b, i), 0)),
                  pl.BlockSpec((None, ctx, HEAD_DIM), lambda b, h, i: (h, b, 0)),
                  pl.BlockSpec((HEAD_DIM, WINDOW), lambda b, h, i: (h, prev_blk(b, i))),
                  pl.BlockSpec((HEAD_DIM, tq), lambda b, h, i: (h, b * nq + i)),
                  pl.BlockSpec((HEAD_DIM, WINDOW), lambda b, h, i: (h, next_blk(b, i))),
                  pl.BlockSpec((HEAD_DIM, ctx), lambda b, h, i: (h, b))],
        out_specs=pl.BlockSpec((tq, GROUP * HEAD_DIM), lambda b, h, i: (b * nq + i, h)),
        scratch_shapes=[pltpu.VMEM((tq + 2 * WINDOW + ctx, HEAD_DIM), BF16),
                        pltpu.VMEM((HEAD_DIM + ONES_ROWS, tq + 2 * WINDOW + ctx), BF16)],
        compiler_params=_cparams(("parallel", "parallel", "parallel")),
        name="win_attn",
    )(sink, q, k, k, k, k_c, vt, vt, vt, vt_c)


ONES_ROWS = 16
Q_CHUNK = 256
K_SUB = 256
QK_AHEAD = 4


def _flash_kernel(sink_ref, q_ref, k_ref, vt_ref, *rest, n_main, has_extra, use_sink):
    if has_extra:
        kx_ref, vtx_ref, o_ref, q2_ref, m_ref, acc_ref, vext_ref, vextx_ref = rest
    else:
        o_ref, q2_ref, m_ref, acc_ref, vext_ref = rest
    hk = pl.program_id(1)
    j = pl.program_id(3)
    tq = q_ref.shape[0]

    @pl.when(j == 0)
    def _():
        for g in range(GROUP):
            q2_ref[g * tq:(g + 1) * tq, :] = q_ref[:, g * HEAD_DIM:(g + 1) * HEAD_DIM]
            if use_sink:
                m_ref[:, g * tq:(g + 1) * tq] = jnp.full((1, tq), sink_ref[hk * GROUP + g] * LOG2E, F32)
        if not use_sink:
            m_ref[...] = jnp.full(m_ref.shape, NEG_BIG, F32)
        acc_ref[0:HEAD_DIM, :] = jnp.zeros((HEAD_DIM, GROUP * tq), F32)
        acc_ref[HEAD_DIM:, :] = jnp.full((ONES_ROWS, GROUP * tq), 1.0 if use_sink else 0.0, F32)
        vext_ref[HEAD_DIM:, :] = jnp.ones((ONES_ROWS, vext_ref.shape[1]), BF16)
        if has_extra:
            vextx_ref[HEAD_DIM:, :] = jnp.ones((ONES_ROWS, vextx_ref.shape[1]), BF16)

    n_chunks = GROUP * tq // Q_CHUNK

    def step(kk_ref, vv_ref, vext):
        tkk = kk_ref.shape[0]
        kb = min(tkk, K_SUB)
        units = [(b, c) for b in range(tkk // kb) for c in range(n_chunks)]
        vext[0:HEAD_DIM, :] = vv_ref[...]

        def scores(u):
            b, c = u
            return _dot_nt(kk_ref[b * kb:(b + 1) * kb, :], q2_ref[c * Q_CHUNK:(c + 1) * Q_CHUNK, :])

        m = [m_ref[:, c * Q_CHUNK:(c + 1) * Q_CHUNK] for c in range(n_chunks)]
        acc = [acc_ref[:, c * Q_CHUNK:(c + 1) * Q_CHUNK] for c in range(n_chunks)]
        ahead = [scores(u) for u in units[:QK_AHEAD]]
        for idx, (b, c) in enumerate(units):
            s = ahead.pop(0)
            if idx + QK_AHEAD < len(units):
                ahead.append(scores(units[idx + QK_AHEAD]))
            m_next = jnp.maximum(m[c], jnp.max(s, axis=0, keepdims=True))
            alpha = jnp.exp2(m[c] - m_next)
            p = jnp.exp2(s - m_next).astype(BF16)
            acc[c] = alpha * acc[c] + jnp.dot(vext[:, b * kb:(b + 1) * kb], p, preferred_element_type=F32)
            m[c] = m_next
        acc_ref[...] = jnp.concatenate(acc, axis=1)
        m_ref[...] = jnp.concatenate(m, axis=1)

    @pl.when(j < n_main)
    def _():
        step(k_ref, vt_ref, vext_ref)

    if has_extra:
        @pl.when(j == n_main)
        def _():
            step(kx_ref, vtx_ref, vextx_ref)

    @pl.when(j == n_main + (1 if has_extra else 0) - 1)
    def _():
        o = (acc_ref[0:HEAD_DIM, :] / acc_ref[HEAD_DIM:HEAD_DIM + 1, :]).T
        for g in range(GROUP):
            o_ref[:, g * HEAD_DIM:(g + 1) * HEAD_DIM] = o[g * tq:(g + 1) * tq, :].astype(BF16)


def _flash(q_arr, k_arr, vt_arr, kx_arr, vtx_arr, sink, *, batch, seq_q, seq_k, seq_x, tq, tk, branch, use_sink):
    nq = seq_q // tq
    n_main = seq_k // tk
    has_extra = kx_arr is not None
    n_steps = n_main + (1 if has_extra else 0)
    qc = branch * N_KV
    kr = branch * N_KV

    def jj(j):
        return jnp.minimum(j, n_main - 1)

    in_specs = [pl.BlockSpec(memory_space=pltpu.SMEM),
                pl.BlockSpec((tq, GROUP * HEAD_DIM), lambda b, h, i, j: (b * nq + i, qc + h)),
                pl.BlockSpec((None, tk, HEAD_DIM), lambda b, h, i, j: (kr + h, b * n_main + jj(j), 0)),
                pl.BlockSpec((HEAD_DIM, tk), lambda b, h, i, j: (kr + h, b * n_main + jj(j)))]
    args = [sink, q_arr, k_arr, vt_arr]
    rows = GROUP * tq
    scratch = [pltpu.VMEM((rows, HEAD_DIM), BF16),
               pltpu.VMEM((1, rows), F32),
               pltpu.VMEM((HEAD_DIM + ONES_ROWS, rows), F32),
               pltpu.VMEM((HEAD_DIM + ONES_ROWS, tk), BF16)]
    if has_extra:
        in_specs += [pl.BlockSpec((None, seq_x, HEAD_DIM), lambda b, h, i, j: (kr + h, b, 0)),
                     pl.BlockSpec((HEAD_DIM, seq_x), lambda b, h, i, j: (kr + h, b))]
        args += [kx_arr, vtx_arr]
        scratch += [pltpu.VMEM((HEAD_DIM + ONES_ROWS, seq_x), BF16)]
    return pl.pallas_call(
        functools.partial(_flash_kernel, n_main=n_main, has_extra=has_extra, use_sink=use_sink),
        out_shape=jax.ShapeDtypeStruct((batch * seq_q, Q_W), BF16),
        grid=(batch, N_KV, nq, n_steps),
        in_specs=in_specs,
        out_specs=pl.BlockSpec((tq, GROUP * HEAD_DIM), lambda b, h, i, j: (b * nq + i, h)),
        scratch_shapes=scratch,
        compiler_params=_cparams(("parallel", "parallel", "parallel", "arbitrary")),
        name="flash",
    )(*args)


def _merge_kernel(h_ref, ya_ref, yb_ref, yc_ref, yd_ref, wga_ref, wgb_ref, wgc_ref, wgd_ref,
                  wb_ref, bg_ref, o_ref):
    h = h_ref[...]
    m = None
    for k, (y_ref, wg_ref) in enumerate(((ya_ref, wga_ref), (yb_ref, wgb_ref), (yc_ref, wgc_ref), (yd_ref, wgd_ref))):
        logits = jnp.dot(h, wg_ref[...], preferred_element_type=F32) + bg_ref[k:k + 1, :]
        t = _sigmoid(logits) * jnp.dot(y_ref[...], wb_ref[k], preferred_element_type=F32)
        m = t if m is None else m + t
    o_ref[...] = m.astype(BF16)


def _merge(h, ys, wg, wb, bg, *, layer, tm, tn):
    t = h.shape[0]
    nct = D_MODEL // tn

    def y():
        return pl.BlockSpec((tm, W_BR), lambda i, j: (i, 0))

    def g(k):
        return pl.BlockSpec((None, D_MODEL, tn), lambda i, j: (layer, 0, OFF_G // tn + k * nct + j))

    return pl.pallas_call(
        _merge_kernel,
        out_shape=jax.ShapeDtypeStruct((t, D_MODEL), BF16),
        grid=(t // tm, nct),
        in_specs=[pl.BlockSpec((tm, D_MODEL), lambda i, j: (i, 0)), y(), y(), y(), y(),
                  g(0), g(1), g(2), g(3),
                  pl.BlockSpec((None, N_BRANCH, W_BR, tn), lambda i, j: (layer, 0, 0, j)),
                  pl.BlockSpec((N_BRANCH, tn), lambda i, j: (0, j))],
        out_specs=pl.BlockSpec((tm, tn), lambda i, j: (i, j)),
        compiler_params=_cparams(("parallel", "arbitrary")),
        name="merge",
    )(h, *ys, wg, wg, wg, wg, wb, bg)


def _resid_kernel(a_ref, w_ref, x_ref, gate_ref, o_ref, *scratch, nk):
    part = jnp.dot(a_ref[...], w_ref[...], preferred_element_type=F32)
    if nk == 1:
        o_ref[...] = x_ref[...] + gate_ref[0] * part
        return
    (acc_ref,) = scratch
    k = pl.program_id(1)

    @pl.when(k == 0)
    def _():
        acc_ref[...] = part

    if nk > 2:
        @pl.when((k > 0) & (k < nk - 1))
        def _():
            acc_ref[...] += part

    @pl.when(k == nk - 1)
    def _():
        o_ref[...] = x_ref[...] + gate_ref[0] * (acc_ref[...] + part)


def _resid(a, w, x, gate, *, layer, seq, tm, tk):
    t, k = a.shape
    nbt = seq // tm
    nk = k // tk
    return pl.pallas_call(
        functools.partial(_resid_kernel, nk=nk),
        out_shape=jax.ShapeDtypeStruct((t, D_MODEL), F32),
        grid=(t // tm, nk),
        in_specs=[pl.BlockSpec((tm, tk), lambda i, j: (i, j)),
                  pl.BlockSpec((None, tk, D_MODEL), lambda i, j: (layer, j, 0)),
                  pl.BlockSpec((tm, D_MODEL), lambda i, j: (i, 0)),
                  pl.BlockSpec((1, 1, D_MODEL), lambda i, j: (i // nbt, 0, 0))],
        out_specs=pl.BlockSpec((tm, D_MODEL), lambda i, j: (i, 0)),
        scratch_shapes=[pltpu.VMEM((tm, D_MODEL), F32)] if nk > 1 else [],
        compiler_params=_cparams(("parallel", "arbitrary")),
        name="resid",
    )(a, w, x, gate)


def _resid_norm_kernel(a_ref, w_ref, x_ref, gate_ref, gn_ref, o_ref, acc_ref, *, nk):
    k = pl.program_id(1)
    part = jnp.dot(a_ref[...], w_ref[...], preferred_element_type=F32)

    @pl.when(k == 0)
    def _():
        acc_ref[...] = part

    @pl.when(k > 0)
    def _():
        acc_ref[...] += part

    @pl.when(k == nk - 1)
    def _():
        gate = gate_ref[0]
        gn = gn_ref[...]

        def body(i, carry):
            rows = pl.ds(pl.multiple_of(i * NORM_CHUNK, NORM_CHUNK), NORM_CHUNK)
            y = x_ref[rows, :] + gate * acc_ref[rows, :]
            o_ref[rows, :] = y * lax.rsqrt(jnp.mean(y * y, axis=-1, keepdims=True) + EPS) * gn
            return carry

        lax.fori_loop(0, o_ref.shape[0] // NORM_CHUNK, body, 0, unroll=NORM_UNROLL)


def _resid_norm(a, w, x, gate, gn, *, layer, seq, tm, tk):
    t, k = a.shape
    nbt = seq // tm
    nk = k // tk
    return pl.pallas_call(
        functools.partial(_resid_norm_kernel, nk=nk),
        out_shape=jax.ShapeDtypeStruct((t, D_MODEL), F32),
        grid=(t // tm, nk),
        in_specs=[pl.BlockSpec((tm, tk), lambda i, j: (i, j)),
                  pl.BlockSpec((None, tk, D_MODEL), lambda i, j: (layer, j, 0)),
                  pl.BlockSpec((tm, D_MODEL), lambda i, j: (i, 0)),
                  pl.BlockSpec((1, 1, D_MODEL), lambda i, j: (i // nbt, 0, 0)),
                  pl.BlockSpec((1, D_MODEL), lambda i, j: (0, 0))],
        out_specs=pl.BlockSpec((tm, D_MODEL), lambda i, j: (i, 0)),
        scratch_shapes=[pltpu.VMEM((tm, D_MODEL), F32)],
        compiler_params=_cparams(("parallel", "arbitrary")),
        name="resid_norm",
    )(a, w, x, gate, gn)


AHEAD_PARTS = 8


def _ffn_in_kernel(x_ref, g_ref, sc_ref, sh_ref, wa_ref, wb_ref, o_ref, h_ref):
    i = pl.program_id(0)
    j = pl.program_id(1)
    part = x_ref.shape[0] // AHEAD_PARTS

    def normalise_part():
        gs = g_ref[...] * (1.0 + sc_ref[0])
        sh = sh_ref[0]
        base = jnp.minimum(j, AHEAD_PARTS - 1) * part
        for c in range(part // NORM_CHUNK):
            rows = pl.ds(pl.multiple_of(base + c * NORM_CHUNK, NORM_CHUNK), NORM_CHUNK)
            x = x_ref[rows, :]
            r = lax.rsqrt(jnp.mean(x * x, axis=-1, keepdims=True) + EPS)
            h_ref[i % 2, rows, :] = (x * r * gs + sh).astype(BF16)

    @pl.when(i == 0)
    def _():
        normalise_part()

    @pl.when(i > 0)
    def _():
        h = h_ref[(i + 1) % 2]
        a = jnp.dot(h, wa_ref[...], preferred_element_type=F32)
        b = jnp.dot(h, wb_ref[...], preferred_element_type=F32)
        o_ref[...] = (a * _sigmoid(a) * b).astype(BF16)
        normalise_part()


def _ffn_in(x, g, sc, sh, w, *, layer, seq, tm, tn):
    t = x.shape[0]
    nbt = seq // tm
    nct = FFN_HIDDEN // tn
    n = t // tm
    assert nct >= AHEAD_PARTS and tm % (AHEAD_PARTS * NORM_CHUNK) == 0

    def tile(i):
        return jnp.minimum(i, n - 1)

    return pl.pallas_call(
        _ffn_in_kernel,
        out_shape=jax.ShapeDtypeStruct((t, FFN_HIDDEN), BF16),
        grid=(n + 1, nct),
        in_specs=[pl.BlockSpec((tm, D_MODEL), lambda i, j: (tile(i), 0)),
                  pl.BlockSpec((1, D_MODEL), lambda i, j: (0, 0)),
                  pl.BlockSpec((1, 1, D_MODEL), lambda i, j: (tile(i) // nbt, 0, 0)),
                  pl.BlockSpec((1, 1, D_MODEL), lambda i, j: (tile(i) // nbt, 0, 0)),
                  pl.BlockSpec((None, D_MODEL, tn), lambda i, j: (layer, 0, j)),
                  pl.BlockSpec((None, D_MODEL, tn), lambda i, j: (layer, 0, nct + j))],
        out_specs=pl.BlockSpec((tm, tn), lambda i, j: (jnp.maximum(i - 1, 0), jnp.where(i == 0, 0, j))),
        scratch_shapes=[pltpu.VMEM((2, tm, D_MODEL), BF16)],
        compiler_params=_cparams(("arbitrary", "arbitrary")),
        name="ffn_in",
    )(x, g, sc, sh, w, w)


def _rope_tables(seq):
    pos = jnp.arange(seq)
    row = (pos // GRID_W).astype(F32)
    col = (pos % GRID_W).astype(F32)
    inv = ROPE_THETA ** (-jnp.arange(0, ROT_AXIS, 2, dtype=F32) / ROT_AXIS)
    ar = row[:, None] * inv
    ac = col[:, None] * inv
    ang = jnp.concatenate([ar, ac, ar, ac], axis=-1)
    first = jnp.arange(HEAD_DIM) < HEAD_DIM // 2
    return jnp.cos(ang), jnp.where(first, -jnp.sin(ang), jnp.sin(ang))


def _pair_split(v):
    quarter = HEAD_DIM // 4
    q = v.reshape(v.shape[:-1] + (v.shape[-1] // HEAD_DIM, 4, quarter))
    q = jnp.concatenate([q[..., 0:1, :], q[..., 2:3, :], q[..., 1:2, :], q[..., 3:4, :]], axis=-2)
    return q.reshape(v.shape)


def _pair_split_heads(w):
    return jnp.concatenate([w[..., :OFF_BQ], _pair_split(w[..., OFF_BQ:OFF_BV]), w[..., OFF_BV:OFF_DQ],
                            _pair_split(w[..., OFF_DQ:OFF_DV]), w[..., OFF_DV:]], axis=-1)


def _tile(n, pref):
    return pref if n % pref == 0 else n


def _forward(x, c, ctx, c_ctx, w_ada, b_ada, norm_mix, norm_ffn, w_in, b_gate, conv_a_w, sink_b,
             qk_norm_q, qk_norm_k, conv_c_w, conv_c_b, ln_c_g, ln_c_b, w_branch, w_out,
             w_ffn_in, w_ffn_out, norm_final):
    batch, seq, _ = x.shape
    n_ctx = ctx.shape[1]
    depth = w_in.shape[0]
    xs = x.reshape(batch * seq, D_MODEL)
    xc = ctx.reshape(batch * n_ctx, D_MODEL)

    tm = _tile(seq, 1024)
    tm_rows = _tile(seq, 512)
    tm_conv_a = _tile(seq, 512)
    tm_conv_c = _tile(seq, 512)
    tq_win = _tile(seq, 512)
    tq = _tile(seq, 1024)
    tk = _tile(seq, 4096)

    rope_x = _rope_tables(seq)
    rope_c = (jnp.ones((n_ctx, HEAD_DIM), F32), jnp.zeros((n_ctx, HEAD_DIM), F32))
    cvec = jnp.zeros((ADA_ROWS, D_MODEL), F32).at[:batch].set(c).at[batch].set(c_ctx)

    def row(v):
        return v.reshape(1, -1)

    w_proj = w_gate = _pair_split_heads(w_in).astype(BF16)
    w_br = w_branch.astype(BF16)
    w_o = w_out.astype(BF16)
    w_f1 = w_ffn_in.astype(BF16)
    w_f2 = w_ffn_out.astype(BF16)

    def mixers(pf, q, k, vt, k_c, vt_c, l, *, s, tma, tmc, is_ctx):
        y_a = _conv_a(pf, conv_a_w[l], seq=s, tm=tma)
        y_c = _conv_c(pf, conv_c_w[l], row(conv_c_b[l]), row(ln_c_g[l]), row(ln_c_b[l]), seq=s, tm=tmc)
        if is_ctx:
            y_b = _flash(q, k, vt, None, None, sink_b[l], batch=batch, seq_q=s, seq_k=s, seq_x=0, tq=s, tk=s,
                         branch=0, use_sink=True)
            y_d = _flash(q, k, vt, None, None, sink_b[l], batch=batch, seq_q=s, seq_k=s, seq_x=0, tq=s, tk=s,
                         branch=1, use_sink=False)
        else:
            y_b = _win_attn(q, k, vt, k_c, vt_c, sink_b[l], batch=batch, seq=s, ctx=n_ctx, tq=tq_win)
            y_d = _flash(q, k, vt, k_c, vt_c, sink_b[l], batch=batch, seq_q=s, seq_k=s, seq_x=n_ctx,
                         tq=tq, tk=tk, branch=1, use_sink=False)
        return [y_a, y_b, y_c, y_d]

    for l in range(depth):
        last = l == depth - 1
        mod = _ada(cvec, w_ada, row(b_ada[l]), layer=l)
        mx = mod[:batch].reshape(batch, 1, 6, D_MODEL)
        mcx = jnp.broadcast_to(mod[batch].reshape(1, 1, 6, D_MODEL), (batch, 1, 6, D_MODEL))
        sh_m, sc_m, g_m, sh_f, sc_f, g_f = (mx[:, :, k] for k in range(6))
        csh_m, csc_m, cg_m, csh_f, csc_f, cg_f = (mcx[:, :, k] for k in range(6))
        gq, gk = row(_pair_split(qk_norm_q[l])), row(_pair_split(qk_norm_k[l]))

        pf_c, q_c, h_c, k_c, vt_c = _inproj(xc, row(norm_mix[l]), csc_m, csh_m, w_proj, *rope_c, gq, gk,
                                            layer=l, seq=n_ctx, tm=n_ctx)
        pf, q, h, k, vt = _inproj(xs, row(norm_mix[l]), sc_m, sh_m, w_proj, *rope_x, gq, gk,
                                  layer=l, seq=seq, tm=tm)

        ys = mixers(pf, q, k, vt, k_c, vt_c, l, s=seq, tma=tm_conv_a, tmc=tm_conv_c, is_ctx=False)
        m = _merge(h, ys, w_gate, w_br, b_gate[l], layer=l, tm=tm, tn=512)
        xs = _resid(m, w_o, xs, g_m, layer=l, seq=seq, tm=tm_rows, tk=D_MODEL)

        if not last:
            ys_c = mixers(pf_c, q_c, k_c, vt_c, None, None, l, s=n_ctx, tma=n_ctx, tmc=n_ctx, is_ctx=True)
            m_c = _merge(h_c, ys_c, w_gate, w_br, b_gate[l], layer=l, tm=n_ctx, tn=512)
            xc = _resid(m_c, w_o, xc, cg_m, layer=l, seq=n_ctx, tm=n_ctx, tk=D_MODEL)
            hid_c = _ffn_in(xc, row(norm_ffn[l]), csc_f, csh_f, w_f1, layer=l, seq=n_ctx, tm=n_ctx, tn=512)
            xc = _resid(hid_c, w_f2, xc, cg_f, layer=l, seq=n_ctx, tm=n_ctx, tk=FFN_HIDDEN // 2)

        hid = _ffn_in(xs, row(norm_ffn[l]), sc_f, sh_f, w_f1, layer=l, seq=seq, tm=tm, tn=512)
        if last:
            xs = _resid_norm(hid, w_f2, xs, g_f, row(norm_final), layer=l, seq=seq, tm=tm_rows,
                             tk=FFN_HIDDEN // 2)
        else:
            xs = _resid(hid, w_f2, xs, g_f, layer=l, seq=seq, tm=tm_rows, tk=FFN_HIDDEN // 2)

    return xs.reshape(batch, seq, D_MODEL)


def kernel(x, c, ctx, c_ctx, w_ada, b_ada, norm_mix, norm_ffn, w_in, b_gate, conv_a_w, sink_b, qk_norm_q,
           qk_norm_k, conv_c_w, conv_c_b, ln_c_g, ln_c_b, w_branch, w_out, w_ffn_in, w_ffn_out, norm_final):
    return _forward(x, c, ctx, c_ctx, w_ada, b_ada, norm_mix, norm_ffn, w_in, b_gate, conv_a_w, sink_b,
                    qk_norm_q, qk_norm_k, conv_c_w, conv_c_b, ln_c_g, ln_c_b, w_branch, w_out,
                    w_ffn_in, w_ffn_out, norm_final)
```

```python
import functools

import jax
import jax.numpy as jnp
from jax import lax
from jax.experimental import pallas as pl
from jax.experimental.pallas import tpu as pltpu

F32 = jnp.float32
BF16 = jnp.bfloat16

D_MODEL = 2048
N_BRANCH = 4
W_BR = D_MODEL // 4
HEAD_DIM = 128
N_Q = W_BR // HEAD_DIM
N_KV = N_Q // 2
GROUP = N_Q // N_KV
Q_W = N_Q * HEAD_DIM
KV_W = N_KV * HEAD_DIM
CONV_A = 3
CONV_C = 31
WINDOW = 128
GRID_W = 64
ROT_AXIS = HEAD_DIM // 2
ROPE_THETA = 10000.0
FFN_HIDDEN = 5632
EPS = 1e-6
ATTN_SCALE = HEAD_DIM ** -0.5
LOG2E = 1.4426950408889634
Q_SCALE = ATTN_SCALE * LOG2E
NEG_BIG = -1e30

OFF_A = 0
OFF_BQ = OFF_A + 3 * W_BR
OFF_BK = OFF_BQ + Q_W
OFF_BV = OFF_BK + KV_W
OFF_C = OFF_BV + KV_W
OFF_DQ = OFF_C + 2 * W_BR
OFF_DK = OFF_DQ + Q_W
OFF_DV = OFF_DK + KV_W
OFF_G = OFF_DV + KV_W

PF_W = 3 * W_BR + 2 * W_BR
PQ_W = 2 * (Q_W + 2 * KV_W)
TN = 512
N_PF_TILES = PF_W // TN
N_PROJ_TILES = (PF_W + PQ_W) // TN

VMEM_LIMIT = 56 * 1024 * 1024


def _cparams(sem):
    return pltpu.CompilerParams(dimension_semantics=sem, vmem_limit_bytes=VMEM_LIMIT)


def _sigmoid(z):
    return 1.0 / (1.0 + jnp.exp(-z))


def _ada_kernel(c_ref, w_ref, b_ref, o_ref):
    c = c_ref[...]
    s = (c * _sigmoid(c)).astype(BF16)
    o_ref[...] = jnp.dot(s, w_ref[...].astype(BF16), preferred_element_type=F32) + b_ref[...]


ADA_ROWS = 16


def _ada(cvec, w, b, *, layer):
    n = w.shape[2]
    tn = 1024
    return pl.pallas_call(
        _ada_kernel,
        out_shape=jax.ShapeDtypeStruct((ADA_ROWS, n), F32),
        grid=(n // tn,),
        in_specs=[pl.BlockSpec((ADA_ROWS, D_MODEL), lambda j: (0, 0)),
                  pl.BlockSpec((None, D_MODEL, tn), lambda j: (layer, 0, j)),
                  pl.BlockSpec((1, tn), lambda j: (0, j))],
        out_specs=pl.BlockSpec((ADA_ROWS, tn), lambda j: (0, j)),
        compiler_params=_cparams(("arbitrary",)),
        name="ada",
    )(cvec, w, b)


def _rope(xh, cos, sin_signed):
    return xh * cos + pltpu.roll(xh, HEAD_DIM // 2, 1) * sin_signed


def _head_norm(xh, g):
    return xh * lax.rsqrt(jnp.mean(xh * xh, axis=-1, keepdims=True) + EPS) * g


NORM_CHUNK = 16
NORM_UNROLL = 8


def _norm_mod_rows(x_ref, g_ref, sc_ref, sh_ref, h_ref):
    gs = g_ref[...] * (1.0 + sc_ref[0])
    sh = sh_ref[0]

    def body(i, carry):
        rows = pl.ds(pl.multiple_of(i * NORM_CHUNK, NORM_CHUNK), NORM_CHUNK)
        x = x_ref[rows, :]
        r = lax.rsqrt(jnp.mean(x * x, axis=-1, keepdims=True) + EPS)
        h_ref[rows, :] = (x * r * gs + sh).astype(BF16)
        return carry

    lax.fori_loop(0, x_ref.shape[0] // NORM_CHUNK, body, 0, unroll=NORM_UNROLL)


def _inproj_kernel(x_ref, g_ref, sc_ref, sh_ref, w_ref, cos_ref, sin_ref, gq_ref, gk_ref,
                   pf_ref, q_ref, h_ref, k_ref, vt_ref):
    j = pl.program_id(1)

    @pl.when(j == 0)
    def _():
        _norm_mod_rows(x_ref, g_ref, sc_ref, sh_ref, h_ref)

    acc = jnp.dot(h_ref[...], w_ref[...], preferred_element_type=F32)

    @pl.when(j < N_PF_TILES)
    def _():
        pf_ref[...] = acc.astype(BF16)

    def rope(xh):
        return _rope(xh, cos_ref[...], sin_ref[...])

    def head(k):
        return acc[:, k * HEAD_DIM:(k + 1) * HEAD_DIM]

    def put_q(k, v):
        q_ref[:, k * HEAD_DIM:(k + 1) * HEAD_DIM] = v.astype(BF16)

    @pl.when(j == N_PF_TILES)
    def _():
        for k in range(N_Q):
            put_q(k, rope(head(k)) * Q_SCALE)

    @pl.when(j == N_PF_TILES + 1)
    def _():
        for k in range(N_KV):
            k_ref[k] = rope(head(k)).astype(BF16)
        vt_ref[0:KV_W, :] = acc[:, KV_W:].T.astype(BF16)

    @pl.when(j == N_PF_TILES + 2)
    def _():
        for k in range(N_Q):
            put_q(k, rope(_head_norm(head(k), gq_ref[...])) * Q_SCALE)

    @pl.when(j == N_PF_TILES + 3)
    def _():
        for k in range(N_KV):
            k_ref[N_KV + k] = rope(_head_norm(head(k), gk_ref[...])).astype(BF16)
        vt_ref[KV_W:, :] = acc[:, KV_W:].T.astype(BF16)


def _proj_block(j):
    n_a, n_b, n_c = 3 * W_BR // TN, (Q_W + 2 * KV_W) // TN, 2 * W_BR // TN
    return jnp.where(j < n_a, j, jnp.where(j < n_a + n_c, j + n_b, jnp.where(j < n_a + n_c + n_b, j - n_c, j)))


def _inproj(x, g, sc, sh, w, cos, sin_signed, gq, gk, *, layer, seq, tm):
    t = x.shape[0]
    nbt = seq // tm
    return pl.pallas_call(
        _inproj_kernel,
        out_shape=(jax.ShapeDtypeStruct((t, PF_W), BF16),
                   jax.ShapeDtypeStruct((t, 2 * Q_W), BF16),
                   jax.ShapeDtypeStruct((t, D_MODEL), BF16),
                   jax.ShapeDtypeStruct((2 * N_KV, t, HEAD_DIM), BF16),
                   jax.ShapeDtypeStruct((2 * KV_W, t), BF16)),
        grid=(t // tm, N_PROJ_TILES),
        in_specs=[pl.BlockSpec((tm, D_MODEL), lambda i, j: (i, 0)),
                  pl.BlockSpec((1, D_MODEL), lambda i, j: (0, 0)),
                  pl.BlockSpec((1, 1, D_MODEL), lambda i, j: (i // nbt, 0, 0)),
                  pl.BlockSpec((1, 1, D_MODEL), lambda i, j: (i // nbt, 0, 0)),
                  pl.BlockSpec((None, D_MODEL, TN), lambda i, j: (layer, 0, _proj_block(j))),
                  pl.BlockSpec((tm, HEAD_DIM), lambda i, j: (i % nbt, 0)),
                  pl.BlockSpec((tm, HEAD_DIM), lambda i, j: (i % nbt, 0)),
                  pl.BlockSpec((1, HEAD_DIM), lambda i, j: (0, 0)),
                  pl.BlockSpec((1, HEAD_DIM), lambda i, j: (0, 0))],
        out_specs=(pl.BlockSpec((tm, TN), lambda i, j: (i, jnp.minimum(j, N_PF_TILES - 1))),
                   pl.BlockSpec((tm, Q_W), lambda i, j: (i, jnp.where(j >= N_PF_TILES + 2, 1, 0))),
                   pl.BlockSpec((tm, D_MODEL), lambda i, j: (i, 0)),
                   pl.BlockSpec((2 * N_KV, tm, HEAD_DIM), lambda i, j: (0, i, 0)),
                   pl.BlockSpec((2 * KV_W, tm), lambda i, j: (0, i))),
        compiler_params=_cparams(("parallel", "arbitrary")),
        name="inproj",
    )(x, g, sc, sh, w, cos, sin_signed, gq, gk)


HALO_A = 16


def _conv_a_kernel(bg_ref, cg_ref, h_ref, cgp_ref, hp_ref, cgn_ref, hn_ref, w_ref, o_ref, u_ref, *, nbt):
    it = pl.program_id(0) % nbt
    tm = bg_ref.shape[0]
    def gated(gate_ref, val_ref):
        return gate_ref[...].astype(F32) * val_ref[...].astype(F32)

    u_ref[HALO_A:HALO_A + tm, :] = gated(cg_ref, h_ref)
    u_ref[0:HALO_A, :] = jnp.where(it > 0, gated(cgp_ref, hp_ref), 0.0)
    u_ref[HALO_A + tm:, :] = jnp.where(it < nbt - 1, gated(cgn_ref, hn_ref), 0.0)
    y = (w_ref[0:1, :] * u_ref[HALO_A - 1:HALO_A - 1 + tm, :]
         + w_ref[1:2, :] * u_ref[HALO_A:HALO_A + tm, :]
         + w_ref[2:3, :] * u_ref[HALO_A + 1:HALO_A + 1 + tm, :])
    o_ref[...] = (bg_ref[...].astype(F32) * y).astype(BF16)


def _conv_a(pf, w, *, seq, tm):
    t = pf.shape[0]
    nbt = seq // tm
    r = tm // HALO_A
    last = t // HALO_A - 1

    def cur(c):
        return pl.BlockSpec((tm, W_BR), lambda i: (i, c))

    def prev(c):
        return pl.BlockSpec((HALO_A, W_BR), lambda i: (jnp.maximum(i * r - 1, 0), c))

    def nxt(c):
        return pl.BlockSpec((HALO_A, W_BR), lambda i: (jnp.minimum((i + 1) * r, last), c))

    return pl.pallas_call(
        functools.partial(_conv_a_kernel, nbt=nbt),
        out_shape=jax.ShapeDtypeStruct((t, W_BR), BF16),
        grid=(t // tm,),
        in_specs=[cur(0), cur(1), cur(2), prev(1), prev(2), nxt(1), nxt(2),
                  pl.BlockSpec((CONV_A, W_BR), lambda i: (0, 0))],
        out_specs=pl.BlockSpec((tm, W_BR), lambda i: (i, 0)),
        scratch_shapes=[pltpu.VMEM((tm + 2 * HALO_A, W_BR), F32)],
        compiler_params=_cparams(("parallel",)),
        name="conv_a",
    )(pf, pf, pf, pf, pf, pf, pf, w)


HALO_C = 16
ROWS_C = 64
SUBLANES = 8


def _conv_c_kernel(v_ref, g_ref, vp_ref, gp_ref, vn_ref, gn_ref, w_ref, b_ref, lg_ref, lb_ref,
                   o_ref, u_ref, us_ref, *, nbt):
    it = pl.program_id(0) % nbt
    tm = v_ref.shape[0]

    def glu(val_ref, gate_ref):
        return val_ref[...].astype(F32) * _sigmoid(gate_ref[...].astype(F32))

    u_ref[HALO_C:HALO_C + tm, :] = glu(v_ref, g_ref)
    u_ref[0:HALO_C, :] = jnp.where(it > 0, glu(vp_ref, gp_ref), 0.0)
    u_ref[HALO_C + tm:, :] = jnp.where(it < nbt - 1, glu(vn_ref, gn_ref), 0.0)
    n_sh = us_ref.shape[1]
    for r in range(1, SUBLANES):
        us_ref[r - 1] = u_ref[r:r + n_sh, :]
    pad = CONV_C // 2

    def tap(row):
        a, r = divmod(row, SUBLANES)
        if r == 0:
            return u_ref[a * SUBLANES:a * SUBLANES + ROWS_C, :]
        return us_ref[r - 1, a * SUBLANES:a * SUBLANES + ROWS_C, :]

    for c in range(tm // ROWS_C):
        base = HALO_C + c * ROWS_C - pad
        acc = w_ref[0:1, :] * tap(base)
        for k in range(1, CONV_C):
            acc = acc + w_ref[k:k + 1, :] * tap(base + k)
        acc = acc + b_ref[...]
        mu = jnp.mean(acc, axis=-1, keepdims=True)
        xc = acc - mu
        var = jnp.mean(xc * xc, axis=-1, keepdims=True)
        y = xc * lax.rsqrt(var + EPS) * lg_ref[...] + lb_ref[...]
        o_ref[c * ROWS_C:(c + 1) * ROWS_C, :] = (y * _sigmoid(y)).astype(BF16)


def _conv_c(pf, w, b, lg, lb, *, seq, tm):
    t = pf.shape[0]
    nbt = seq // tm
    r = tm // HALO_C
    last = t // HALO_C - 1
    c0 = 3

    def cur(c):
        return pl.BlockSpec((tm, W_BR), lambda i: (i, c))

    def prev(c):
        return pl.BlockSpec((HALO_C, W_BR), lambda i: (jnp.maximum(i * r - 1, 0), c))

    def nxt(c):
        return pl.BlockSpec((HALO_C, W_BR), lambda i: (jnp.minimum((i + 1) * r, last), c))

    def vec():
        return pl.BlockSpec((1, W_BR), lambda i: (0, 0))

    return pl.pallas_call(
        functools.partial(_conv_c_kernel, nbt=nbt),
        out_shape=jax.ShapeDtypeStruct((t, W_BR), BF16),
        grid=(t // tm,),
        in_specs=[cur(c0), cur(c0 + 1), prev(c0), prev(c0 + 1), nxt(c0), nxt(c0 + 1),
                  pl.BlockSpec((CONV_C, W_BR), lambda i: (0, 0)), vec(), vec(), vec()],
        out_specs=pl.BlockSpec((tm, W_BR), lambda i: (i, 0)),
        scratch_shapes=[pltpu.VMEM((tm + 2 * HALO_C, W_BR), F32),
                        pltpu.VMEM((SUBLANES - 1, tm + 2 * HALO_C - SUBLANES, W_BR), F32)],
        compiler_params=_cparams(("parallel",)),
        name="conv_c",
    )(pf, pf, pf, pf, pf, pf, w, b, lg, lb)


def _dot_nt(a, b):
    return lax.dot_general(a, b, (((1,), (1,)), ((), ())), preferred_element_type=F32)


WIN_UNIT = 2 * WINDOW
WIN_AHEAD = 5


def _win_kernel(sink_ref, q_ref, kp_ref, kc_ref, kn_ref, kx_ref, vp_ref, vc_ref, vn_ref, vx_ref, o_ref,
                kk_ref, vext_ref, *, nq):
    hk = pl.program_id(1)
    i = pl.program_id(2)
    tq = q_ref.shape[0]
    n_loc = tq + 2 * WINDOW
    n_all = kk_ref.shape[0]
    u = WIN_UNIT
    kk_ref[0:WINDOW, :] = kp_ref[...]
    kk_ref[WINDOW:WINDOW + tq, :] = kc_ref[...]
    kk_ref[WINDOW + tq:n_loc, :] = kn_ref[...]
    kk_ref[n_loc:, :] = kx_ref[...]
    vext_ref[0:HEAD_DIM, 0:WINDOW] = vp_ref[...]
    vext_ref[0:HEAD_DIM, WINDOW:WINDOW + tq] = vc_ref[...]
    vext_ref[0:HEAD_DIM, WINDOW + tq:n_loc] = vn_ref[...]
    vext_ref[0:HEAD_DIM, n_loc:] = vx_ref[...]
    vext_ref[HEAD_DIM:, :] = jnp.ones((ONES_ROWS, n_all), BF16)

    key = lax.broadcasted_iota(jnp.int32, (u, u), 0)
    qry = lax.broadcasted_iota(jnp.int32, (u, u), 1)
    n_loc_blocks = n_loc // u
    n_chunks = tq // u

    def mask(c, kb):
        ok = jnp.abs((kb - c) * u - WINDOW + key - qry) <= WINDOW
        if kb == 0:
            ok = ok & (key >= jnp.where(i > 0, 0, WINDOW))
        if kb == n_loc_blocks - 1:
            ok = ok & (key < jnp.where(i < nq - 1, u, u - WINDOW))
        return ok

    units = []
    for g in range(GROUP):
        for c in range(n_chunks):
            units += [(g, c, kb) for kb in (c, c + 1)] + [(g, c, kb) for kb in range(n_loc_blocks, n_all // u)]

    def scores(unit):
        g, c, kb = unit
        return _dot_nt(kk_ref[kb * u:(kb + 1) * u, :], q_ref[c * u:(c + 1) * u, g * HEAD_DIM:(g + 1) * HEAD_DIM])

    m, acc = {}, {}
    for g in range(GROUP):
        sink = sink_ref[hk * GROUP + g] * LOG2E
        for c in range(n_chunks):
            m[g, c] = jnp.full((1, u), sink, F32)
            acc[g, c] = jnp.concatenate([jnp.zeros((HEAD_DIM, u), F32), jnp.ones((ONES_ROWS, u), F32)], axis=0)

    ahead = [scores(unit) for unit in units[:WIN_AHEAD]]
    for idx, (g, c, kb) in enumerate(units):
        s = ahead.pop(0)
        if idx + WIN_AHEAD < len(units):
            ahead.append(scores(units[idx + WIN_AHEAD]))
        if kb < n_loc_blocks:
            s = jnp.where(mask(c, kb), s, NEG_BIG)
        m_next = jnp.maximum(m[g, c], jnp.max(s, axis=0, keepdims=True))
        alpha = jnp.exp2(m[g, c] - m_next)
        p = jnp.exp2(s - m_next).astype(BF16)
        acc[g, c] = alpha * acc[g, c] + jnp.dot(vext_ref[:, kb * u:(kb + 1) * u], p, preferred_element_type=F32)
        m[g, c] = m_next

    outs = []
    for g in range(GROUP):
        o_t = jnp.concatenate([acc[g, c][0:HEAD_DIM, :] / acc[g, c][HEAD_DIM:HEAD_DIM + 1, :]
                               for c in range(n_chunks)], axis=1)
        outs.append(o_t.T.astype(BF16))
    o_ref[...] = jnp.concatenate(outs, axis=1)


def _win_attn(q, k, vt, k_c, vt_c, sink, *, batch, seq, ctx, tq):
    nq = seq // tq
    r = tq // WINDOW
    nblk = seq // WINDOW

    def prev_blk(b, i):
        return b * nblk + jnp.maximum(i * r - 1, 0)

    def next_blk(b, i):
        return b * nblk + jnp.minimum((i + 1) * r, nblk - 1)

    return pl.pallas_call(
        functools.partial(_win_kernel, nq=nq),
        out_shape=jax.ShapeDtypeStruct((batch * seq, Q_W), BF16),
        grid=(batch, N_KV, nq),
        in_specs=[pl.BlockSpec(memory_space=pltpu.SMEM),
                  pl.BlockSpec((tq, GROUP * HEAD_DIM), lambda b, h, i: (b * nq + i, h)),
                  pl.BlockSpec((None, WINDOW, HEAD_DIM), lambda b, h, i: (h, prev_blk(b, i), 0)),
                  pl.BlockSpec((None, tq, HEAD_DIM), lambda b, h, i: (h, b * nq + i, 0)),
                  pl.BlockSpec((None, WINDOW, HEAD_DIM), lambda b, h, i: (h, next_blk(b, i), 0)),
                  pl.BlockSpec((None, ctx, HEAD_DIM), lambda b, h, i: (h, b, 0)),
                  pl.BlockSpec((HEAD_DIM, WINDOW), lambda b, h, i: (h, prev_blk(b, i))),
                  pl.BlockSpec((HEAD_DIM, tq), lambda b, h, i: (h, b * nq + i)),
                  pl.BlockSpec((HEAD_DIM, WINDOW), lambda b, h, i: (h, next_blk(b, i))),
                  pl.BlockSpec((HEAD_DIM, ctx), lambda b, h, i: (h, b))],
        out_specs=pl.BlockSpec((tq, GROUP * HEAD_DIM), lambda b, h, i: (b * nq + i, h)),
        scratch_shapes=[pltpu.VMEM((tq + 2 * WINDOW + ctx, HEAD_DIM), BF16),
                        pltpu.VMEM((HEAD_DIM + ONES_ROWS, tq + 2 * WINDOW + ctx), BF16)],
        compiler_params=_cparams(("parallel", "parallel", "parallel")),
        name="win_attn",
    )(sink, q, k, k, k, k_c, vt, vt, vt, vt_c)


ONES_ROWS = 16
Q_CHUNK = 256
K_SUB = 256
QK_AHEAD = 5


def _flash_kernel(sink_ref, q_ref, k_ref, vt_ref, *rest, n_main, has_extra, use_sink):
    if has_extra:
        kx_ref, vtx_ref, o_ref, q2_ref, m_ref, acc_ref, vext_ref, vextx_ref = rest
    else:
        o_ref, q2_ref, m_ref, acc_ref, vext_ref = rest
    hk = pl.program_id(1)
    j = pl.program_id(3)
    tq = q_ref.shape[0]

    @pl.when(j == 0)
    def _():
        for g in range(GROUP):
            q2_ref[g * tq:(g + 1) * tq, :] = q_ref[:, g * HEAD_DIM:(g + 1) * HEAD_DIM]
            if use_sink:
                m_ref[:, g * tq:(g + 1) * tq] = jnp.full((1, tq), sink_ref[hk * GROUP + g] * LOG2E, F32)
        if not use_sink:
            m_ref[...] = jnp.full(m_ref.shape, NEG_BIG, F32)
        acc_ref[0:HEAD_DIM, :] = jnp.zeros((HEAD_DIM, GROUP * tq), F32)
        acc_ref[HEAD_DIM:, :] = jnp.full((ONES_ROWS, GROUP * tq), 1.0 if use_sink else 0.0, F32)
        vext_ref[HEAD_DIM:, :] = jnp.ones((ONES_ROWS, vext_ref.shape[1]), BF16)
        if has_extra:
            vextx_ref[HEAD_DIM:, :] = jnp.ones((ONES_ROWS, vextx_ref.shape[1]), BF16)

    n_chunks = GROUP * tq // Q_CHUNK

    def step(kk_ref, vv_ref, vext):
        tkk = kk_ref.shape[0]
        kb = min(tkk, K_SUB)
        units = [(b, c) for b in range(tkk // kb) for c in range(n_chunks)]
        vext[0:HEAD_DIM, :] = vv_ref[...]

        def scores(u):
            b, c = u
            return _dot_nt(kk_ref[b * kb:(b + 1) * kb, :], q2_ref[c * Q_CHUNK:(c + 1) * Q_CHUNK, :])

        m = [m_ref[:, c * Q_CHUNK:(c + 1) * Q_CHUNK] for c in range(n_chunks)]
        acc = [acc_ref[:, c * Q_CHUNK:(c + 1) * Q_CHUNK] for c in range(n_chunks)]
        ahead = [scores(u) for u in units[:QK_AHEAD]]
        for idx, (b, c) in enumerate(units):
            s = ahead.pop(0)
            if idx + QK_AHEAD < len(units):
                ahead.append(scores(units[idx + QK_AHEAD]))
            m_next = jnp.maximum(m[c], jnp.max(s, axis=0, keepdims=True))
            alpha = jnp.exp2(m[c] - m_next)
            p = jnp.exp2(s - m_next).astype(BF16)
            acc[c] = alpha * acc[c] + jnp.dot(vext[:, b * kb:(b + 1) * kb], p, preferred_element_type=F32)
            m[c] = m_next
        acc_ref[...] = jnp.concatenate(acc, axis=1)
        m_ref[...] = jnp.concatenate(m, axis=1)

    @pl.when(j < n_main)
    def _():
        step(k_ref, vt_ref, vext_ref)

    if has_extra:
        @pl.when(j == n_main)
        def _():
            step(kx_ref, vtx_ref, vextx_ref)

    @pl.when(j == n_main + (1 if has_extra else 0) - 1)
    def _():
        o = (acc_ref[0:HEAD_DIM, :] / acc_ref[HEAD_DIM:HEAD_DIM + 1, :]).T
        for g in range(GROUP):
            o_ref[:, g * HEAD_DIM:(g + 1) * HEAD_DIM] = o[g * tq:(g + 1) * tq, :].astype(BF16)


def _flash(q_arr, k_arr, vt_arr, kx_arr, vtx_arr, sink, *, batch, seq_q, seq_k, seq_x, tq, tk, branch, use_sink):
    nq = seq_q // tq
    n_main = seq_k // tk
    has_extra = kx_arr is not None
    n_steps = n_main + (1 if has_extra else 0)
    qc = branch * N_KV
    kr = branch * N_KV

    def jj(j):
        return jnp.minimum(j, n_main - 1)

    in_specs = [pl.BlockSpec(memory_space=pltpu.SMEM),
                pl.BlockSpec((tq, GROUP * HEAD_DIM), lambda b, h, i, j: (b * nq + i, qc + h)),
                pl.BlockSpec((None, tk, HEAD_DIM), lambda b, h, i, j: (kr + h, b * n_main + jj(j), 0)),
                pl.BlockSpec((HEAD_DIM, tk), lambda b, h, i, j: (kr + h, b * n_main + jj(j)))]
    args = [sink, q_arr, k_arr, vt_arr]
    rows = GROUP * tq
    scratch = [pltpu.VMEM((rows, HEAD_DIM), BF16),
               pltpu.VMEM((1, rows), F32),
               pltpu.VMEM((HEAD_DIM + ONES_ROWS, rows), F32),
               pltpu.VMEM((HEAD_DIM + ONES_ROWS, tk), BF16)]
    if has_extra:
        in_specs += [pl.BlockSpec((None, seq_x, HEAD_DIM), lambda b, h, i, j: (kr + h, b, 0)),
                     pl.BlockSpec((HEAD_DIM, seq_x), lambda b, h, i, j: (kr + h, b))]
        args += [kx_arr, vtx_arr]
        scratch += [pltpu.VMEM((HEAD_DIM + ONES_ROWS, seq_x), BF16)]
    return pl.pallas_call(
        functools.partial(_flash_kernel, n_main=n_main, has_extra=has_extra, use_sink=use_sink),
        out_shape=jax.ShapeDtypeStruct((batch * seq_q, Q_W), BF16),
        grid=(batch, N_KV, nq, n_steps),
        in_specs=in_specs,
        out_specs=pl.BlockSpec((tq, GROUP * HEAD_DIM), lambda b, h, i, j: (b * nq + i, h)),
        scratch_shapes=scratch,
        compiler_params=_cparams(("parallel", "parallel", "parallel", "arbitrary")),
        name="flash",
    )(*args)


def _merge_kernel(h_ref, ya_ref, yb_ref, yc_ref, yd_ref, wga_ref, wgb_ref, wgc_ref, wgd_ref,
                  wb_ref, bg_ref, o_ref):
    h = h_ref[...]
    m = None
    for k, (y_ref, wg_ref) in enumerate(((ya_ref, wga_ref), (yb_ref, wgb_ref), (yc_ref, wgc_ref), (yd_ref, wgd_ref))):
        logits = jnp.dot(h, wg_ref[...], preferred_element_type=F32) + bg_ref[k:k + 1, :]
        t = _sigmoid(logits) * jnp.dot(y_ref[...], wb_ref[k], preferred_element_type=F32)
        m = t if m is None else m + t
    o_ref[...] = m.astype(BF16)


def _merge(h, ys, wg, wb, bg, *, layer, tm, tn):
    t = h.shape[0]
    nct = D_MODEL // tn

    def y():
        return pl.BlockSpec((tm, W_BR), lambda i, j: (i, 0))

    def g(k):
        return pl.BlockSpec((None, D_MODEL, tn), lambda i, j: (layer, 0, OFF_G // tn + k * nct + j))

    return pl.pallas_call(
        _merge_kernel,
        out_shape=jax.ShapeDtypeStruct((t, D_MODEL), BF16),
        grid=(t // tm, nct),
        in_specs=[pl.BlockSpec((tm, D_MODEL), lambda i, j: (i, 0)), y(), y(), y(), y(),
                  g(0), g(1), g(2), g(3),
                  pl.BlockSpec((None, N_BRANCH, W_BR, tn), lambda i, j: (layer, 0, 0, j)),
                  pl.BlockSpec((N_BRANCH, tn), lambda i, j: (0, j))],
        out_specs=pl.BlockSpec((tm, tn), lambda i, j: (i, j)),
        compiler_params=_cparams(("parallel", "arbitrary")),
        name="merge",
    )(h, *ys, wg, wg, wg, wg, wb, bg)


def _resid_kernel(a_ref, w_ref, x_ref, gate_ref, o_ref, *scratch, nk):
    part = jnp.dot(a_ref[...], w_ref[...], preferred_element_type=F32)
    if nk == 1:
        o_ref[...] = x_ref[...] + gate_ref[0] * part
        return
    (acc_ref,) = scratch
    k = pl.program_id(1)

    @pl.when(k == 0)
    def _():
        acc_ref[...] = part

    if nk > 2:
        @pl.when((k > 0) & (k < nk - 1))
        def _():
            acc_ref[...] += part

    @pl.when(k == nk - 1)
    def _():
        o_ref[...] = x_ref[...] + gate_ref[0] * (acc_ref[...] + part)


def _resid(a, w, x, gate, *, layer, seq, tm, tk):
    t, k = a.shape
    nbt = seq // tm
    nk = k // tk
    return pl.pallas_call(
        functools.partial(_resid_kernel, nk=nk),
        out_shape=jax.ShapeDtypeStruct((t, D_MODEL), F32),
        grid=(t // tm, nk),
        in_specs=[pl.BlockSpec((tm, tk), lambda i, j: (i, j)),
                  pl.BlockSpec((None, tk, D_MODEL), lambda i, j: (layer, j, 0)),
                  pl.BlockSpec((tm, D_MODEL), lambda i, j: (i, 0)),
                  pl.BlockSpec((1, 1, D_MODEL), lambda i, j: (i // nbt, 0, 0))],
        out_specs=pl.BlockSpec((tm, D_MODEL), lambda i, j: (i, 0)),
        scratch_shapes=[pltpu.VMEM((tm, D_MODEL), F32)] if nk > 1 else [],
        compiler_params=_cparams(("parallel", "arbitrary")),
        name="resid",
    )(a, w, x, gate)


def _resid_norm_kernel(a_ref, w_ref, x_ref, gate_ref, gn_ref, o_ref, acc_ref, *, nk):
    k = pl.program_id(1)
    part = jnp.dot(a_ref[...], w_ref[...], preferred_element_type=F32)

    @pl.when(k == 0)
    def _():
        acc_ref[...] = part

    @pl.when(k > 0)
    def _():
        acc_ref[...] += part

    @pl.when(k == nk - 1)
    def _():
        gate = gate_ref[0]
        gn = gn_ref[...]

        def body(i, carry):
            rows = pl.ds(pl.multiple_of(i * NORM_CHUNK, NORM_CHUNK), NORM_CHUNK)
            y = x_ref[rows, :] + gate * acc_ref[rows, :]
            o_ref[rows, :] = y * lax.rsqrt(jnp.mean(y * y, axis=-1, keepdims=True) + EPS) * gn
            return carry

        lax.fori_loop(0, o_ref.shape[0] // NORM_CHUNK, body, 0, unroll=NORM_UNROLL)


def _resid_norm(a, w, x, gate, gn, *, layer, seq, tm, tk):
    t, k = a.shape
    nbt = seq // tm
    nk = k // tk
    return pl.pallas_call(
        functools.partial(_resid_norm_kernel, nk=nk),
        out_shape=jax.ShapeDtypeStruct((t, D_MODEL), F32),
        grid=(t // tm, nk),
        in_specs=[pl.BlockSpec((tm, tk), lambda i, j: (i, j)),
                  pl.BlockSpec((None, tk, D_MODEL), lambda i, j: (layer, j, 0)),
                  pl.BlockSpec((tm, D_MODEL), lambda i, j: (i, 0)),
                  pl.BlockSpec((1, 1, D_MODEL), lambda i, j: (i // nbt, 0, 0)),
                  pl.BlockSpec((1, D_MODEL), lambda i, j: (0, 0))],
        out_specs=pl.BlockSpec((tm, D_MODEL), lambda i, j: (i, 0)),
        scratch_shapes=[pltpu.VMEM((tm, D_MODEL), F32)],
        compiler_params=_cparams(("parallel", "arbitrary")),
        name="resid_norm",
    )(a, w, x, gate, gn)


def _ffn_in_kernel(x_ref, g_ref, sc_ref, sh_ref, wa_ref, wb_ref, o_ref, h_ref):
    @pl.when(pl.program_id(1) == 0)
    def _():
        _norm_mod_rows(x_ref, g_ref, sc_ref, sh_ref, h_ref)

    h = h_ref[...]
    a = jnp.dot(h, wa_ref[...], preferred_element_type=F32)
    b = jnp.dot(h, wb_ref[...], preferred_element_type=F32)
    o_ref[...] = (a * _sigmoid(a) * b).astype(BF16)


def _ffn_in(x, g, sc, sh, w, *, layer, seq, tm, tn):
    t = x.shape[0]
    nbt = seq // tm
    nct = FFN_HIDDEN // tn
    return pl.pallas_call(
        _ffn_in_kernel,
        out_shape=jax.ShapeDtypeStruct((t, FFN_HIDDEN), BF16),
        grid=(t // tm, nct),
        in_specs=[pl.BlockSpec((tm, D_MODEL), lambda i, j: (i, 0)),
                  pl.BlockSpec((1, D_MODEL), lambda i, j: (0, 0)),
                  pl.BlockSpec((1, 1, D_MODEL), lambda i, j: (i // nbt, 0, 0)),
                  pl.BlockSpec((1, 1, D_MODEL), lambda i, j: (i // nbt, 0, 0)),
                  pl.BlockSpec((None, D_MODEL, tn), lambda i, j: (layer, 0, j)),
                  pl.BlockSpec((None, D_MODEL, tn), lambda i, j: (layer, 0, nct + j))],
        out_specs=pl.BlockSpec((tm, tn), lambda i, j: (i, j)),
        scratch_shapes=[pltpu.VMEM((tm, D_MODEL), BF16)],
        compiler_params=_cparams(("parallel", "arbitrary")),
        name="ffn_in",
    )(x, g, sc, sh, w, w)


def _rope_tables(seq):
    n_rows = seq // GRID_W
    inv = ROPE_THETA ** (-jnp.arange(0, ROT_AXIS, 2, dtype=F32) / ROT_AXIS)
    ar = jnp.arange(n_rows, dtype=F32)[:, None] * inv
    ac = jnp.arange(GRID_W, dtype=F32)[:, None] * inv
    shape = (n_rows, GRID_W, ROT_AXIS // 2)

    def grid(fr, fc):
        r = jnp.broadcast_to(fr[:, None, :], shape)
        c = jnp.broadcast_to(fc[None, :, :], shape)
        return r, c

    cr, cc = grid(jnp.cos(ar), jnp.cos(ac))
    sr, sc = grid(jnp.sin(ar), jnp.sin(ac))
    cos = jnp.concatenate([cr, cc, cr, cc], axis=-1).reshape(seq, HEAD_DIM)
    sin_signed = jnp.concatenate([-sr, -sc, sr, sc], axis=-1).reshape(seq, HEAD_DIM)
    return cos, sin_signed


def _pair_split(v):
    quarter = HEAD_DIM // 4
    q = v.reshape(v.shape[:-1] + (v.shape[-1] // HEAD_DIM, 4, quarter))
    q = jnp.concatenate([q[..., 0:1, :], q[..., 2:3, :], q[..., 1:2, :], q[..., 3:4, :]], axis=-2)
    return q.reshape(v.shape)


def _pair_split_heads(w):
    return jnp.concatenate([w[..., :OFF_BQ], _pair_split(w[..., OFF_BQ:OFF_BV]), w[..., OFF_BV:OFF_DQ],
                            _pair_split(w[..., OFF_DQ:OFF_DV]), w[..., OFF_DV:]], axis=-1)


def _tile(n, pref):
    return pref if n % pref == 0 else n


def _forward(x, c, ctx, c_ctx, w_ada, b_ada, norm_mix, norm_ffn, w_in, b_gate, conv_a_w, sink_b,
             qk_norm_q, qk_norm_k, conv_c_w, conv_c_b, ln_c_g, ln_c_b, w_branch, w_out,
             w_ffn_in, w_ffn_out, norm_final):
    batch, seq, _ = x.shape
    n_ctx = ctx.shape[1]
    depth = w_in.shape[0]
    xs = x.reshape(batch * seq, D_MODEL)
    xc = ctx.reshape(batch * n_ctx, D_MODEL)

    tm = _tile(seq, 1024)
    tm_rows = _tile(seq, 512)
    tm_conv_a = _tile(seq, 512)
    tm_conv_c = _tile(seq, 512)
    tq_win = _tile(seq, 512)
    tq = _tile(seq, 1024)
    tk = _tile(seq, 4096)

    rope_x = _rope_tables(seq)
    rope_c = (jnp.ones((n_ctx, HEAD_DIM), F32), jnp.zeros((n_ctx, HEAD_DIM), F32))
    cvec = jnp.zeros((ADA_ROWS, D_MODEL), F32).at[:batch].set(c).at[batch].set(c_ctx)

    def row(v):
        return v.reshape(1, -1)

    w_proj = w_gate = _pair_split_heads(w_in.astype(BF16))
    w_br = w_branch.astype(BF16)
    w_o = w_out.astype(BF16)
    w_f1 = w_ffn_in.astype(BF16)
    w_f2 = w_ffn_out.astype(BF16)

    def mixers(pf, q, k, vt, k_c, vt_c, l, *, s, tma, tmc, is_ctx):
        y_a = _conv_a(pf, conv_a_w[l], seq=s, tm=tma)
        y_c = _conv_c(pf, conv_c_w[l], row(conv_c_b[l]), row(ln_c_g[l]), row(ln_c_b[l]), seq=s, tm=tmc)
        if is_ctx:
            y_b = _flash(q, k, vt, None, None, sink_b[l], batch=batch, seq_q=s, seq_k=s, seq_x=0, tq=s, tk=s,
                         branch=0, use_sink=True)
            y_d = _flash(q, k, vt, None, None, sink_b[l], batch=batch, seq_q=s, seq_k=s, seq_x=0, tq=s, tk=s,
                         branch=1, use_sink=False)
        else:
            y_b = _win_attn(q, k, vt, k_c, vt_c, sink_b[l], batch=batch, seq=s, ctx=n_ctx, tq=tq_win)
            y_d = _flash(q, k, vt, k_c, vt_c, sink_b[l], batch=batch, seq_q=s, seq_k=s, seq_x=n_ctx,
                         tq=tq, tk=tk, branch=1, use_sink=False)
        return [y_a, y_b, y_c, y_d]

    for l in range(depth):
        last = l == depth - 1
        mod = _ada(cvec, w_ada, row(b_ada[l]), layer=l)
        mx = mod[:batch].reshape(batch, 1, 6, D_MODEL)
        mcx = jnp.broadcast_to(mod[batch].reshape(1, 1, 6, D_MODEL), (batch, 1, 6, D_MODEL))
        sh_m, sc_m, g_m, sh_f, sc_f, g_f = (mx[:, :, k] for k in range(6))
        csh_m, csc_m, cg_m, csh_f, csc_f, cg_f = (mcx[:, :, k] for k in range(6))
        gq, gk = row(_pair_split(qk_norm_q[l])), row(_pair_split(qk_norm_k[l]))

        pf_c, q_c, h_c, k_c, vt_c = _inproj(xc, row(norm_mix[l]), csc_m, csh_m, w_proj, *rope_c, gq, gk,
                                            layer=l, seq=n_ctx, tm=n_ctx)
        pf, q, h, k, vt = _inproj(xs, row(norm_mix[l]), sc_m, sh_m, w_proj, *rope_x, gq, gk,
                                  layer=l, seq=seq, tm=tm)

        ys = mixers(pf, q, k, vt, k_c, vt_c, l, s=seq, tma=tm_conv_a, tmc=tm_conv_c, is_ctx=False)
        m = _merge(h, ys, w_gate, w_br, b_gate[l], layer=l, tm=tm, tn=512)
        xs = _resid(m, w_o, xs, g_m, layer=l, seq=seq, tm=tm_rows, tk=D_MODEL)

        if not last:
            ys_c = mixers(pf_c, q_c, k_c, vt_c, None, None, l, s=n_ctx, tma=n_ctx, tmc=n_ctx, is_ctx=True)
            t_c = batch * n_ctx
            m_c = _merge(h_c, ys_c, w_gate, w_br, b_gate[l], layer=l, tm=t_c, tn=512)
            xc = _resid(m_c, w_o, xc, cg_m, layer=l, seq=t_c, tm=t_c, tk=D_MODEL)
            hid_c = _ffn_in(xc, row(norm_ffn[l]), csc_f, csh_f, w_f1, layer=l, seq=t_c, tm=t_c, tn=512)
            xc = _resid(hid_c, w_f2, xc, cg_f, layer=l, seq=t_c, tm=t_c, tk=FFN_HIDDEN // 2)

        hid = _ffn_in(xs, row(norm_ffn[l]), sc_f, sh_f, w_f1, layer=l, seq=seq, tm=tm, tn=512)
        if last:
            xs = _resid_norm(hid, w_f2, xs, g_f, row(norm_final), layer=l, seq=seq, tm=tm_rows,
                             tk=FFN_HIDDEN // 2)
        else:
            xs = _resid(hid, w_f2, xs, g_f, layer=l, seq=seq, tm=tm_rows, tk=FFN_HIDDEN // 2)

    return xs.reshape(batch, seq, D_MODEL)


def kernel(x, c, ctx, c_ctx, w_ada, b_ada, norm_mix, norm_ffn, w_in, b_gate, conv_a_w, sink_b, qk_norm_q,
           qk_norm_k, conv_c_w, conv_c_b, ln_c_g, ln_c_b, w_branch, w_out, w_ffn_in, w_ffn_out, norm_final):
    return _forward(x, c, ctx, c_ctx, w_ada, b_ada, norm_mix, norm_ffn, w_in, b_gate, conv_a_w, sink_b,
                    qk_norm_q, qk_norm_k, conv_c_w, conv_c_b, ln_c_g, ln_c_b, w_branch, w_out,
                    w_ffn_in, w_ffn_out, norm_final)
```

```python
import functools
from typing import NamedTuple

import jax
import jax.numpy as jnp
from jax import lax
from jax.experimental import pallas as pl
from jax.experimental.pallas import tpu as pltpu

F32 = jnp.float32
BF16 = jnp.bfloat16

D_MODEL = 2048
N_BRANCH = 4
W_BR = D_MODEL // 4
HEAD_DIM = 128
N_Q = W_BR // HEAD_DIM
N_KV = N_Q // 2
GROUP = N_Q // N_KV
Q_W = N_Q * HEAD_DIM
KV_W = N_KV * HEAD_DIM
CONV_A = 3
CONV_C = 31
WINDOW = 128
GRID_W = 64
ROT_AXIS = HEAD_DIM // 2
ROPE_THETA = 10000.0
FFN_HIDDEN = 5632
EPS = 1e-6
ATTN_SCALE = HEAD_DIM ** -0.5
LOG2E = 1.4426950408889634
Q_SCALE = ATTN_SCALE * LOG2E
NEG_BIG = -1e30

OFF_A = 0
OFF_BQ = OFF_A + 3 * W_BR
OFF_BK = OFF_BQ + Q_W
OFF_BV = OFF_BK + KV_W
OFF_C = OFF_BV + KV_W
OFF_DQ = OFF_C + 2 * W_BR
OFF_DK = OFF_DQ + Q_W
OFF_DV = OFF_DK + KV_W
OFF_G = OFF_DV + KV_W

PF_W = 3 * W_BR + 2 * W_BR
PQ_W = 2 * (Q_W + 2 * KV_W)
TN = 512
N_PF_TILES = PF_W // TN
N_PROJ_TILES = (PF_W + PQ_W) // TN

VMEM_LIMIT = 56 * 1024 * 1024


def _cparams(sem):
    return pltpu.CompilerParams(dimension_semantics=sem, vmem_limit_bytes=VMEM_LIMIT)


def _sigmoid(z):
    return 1.0 / (1.0 + jnp.exp(-z))


def _ada_kernel(c_ref, w_ref, b_ref, o_ref):
    c = c_ref[...]
    s = (c * _sigmoid(c)).astype(BF16)
    o_ref[...] = jnp.dot(s, w_ref[...].astype(BF16), preferred_element_type=F32) + b_ref[...]


ADA_ROWS = 16


def _ada(cvec, w, b, *, layer):
    n = w.shape[2]
    tn = 1024
    return pl.pallas_call(
        _ada_kernel,
        out_shape=jax.ShapeDtypeStruct((ADA_ROWS, n), F32),
        grid=(n // tn,),
        in_specs=[pl.BlockSpec((ADA_ROWS, D_MODEL), lambda j: (0, 0)),
                  pl.BlockSpec((None, D_MODEL, tn), lambda j: (layer, 0, j)),
                  pl.BlockSpec((1, tn), lambda j: (0, j))],
        out_specs=pl.BlockSpec((ADA_ROWS, tn), lambda j: (0, j)),
        compiler_params=_cparams(("arbitrary",)),
        name="ada",
    )(cvec, w, b)


def _rope(xh, cos, sin_signed):
    return xh * cos + pltpu.roll(xh, HEAD_DIM // 2, 1) * sin_signed


def _head_norm(xh, g):
    return xh * lax.rsqrt(jnp.mean(xh * xh, axis=-1, keepdims=True) + EPS) * g


NORM_CHUNK = 16
NORM_UNROLL = 8


def _norm_mod_rows(x_ref, g_ref, sc_ref, sh_ref, h_ref):
    gs = g_ref[...] * (1.0 + sc_ref[0])
    sh = sh_ref[0]

    def body(i, carry):
        rows = pl.ds(pl.multiple_of(i * NORM_CHUNK, NORM_CHUNK), NORM_CHUNK)
        x = x_ref[rows, :]
        r = lax.rsqrt(jnp.mean(x * x, axis=-1, keepdims=True) + EPS)
        h_ref[rows, :] = (x * r * gs + sh).astype(BF16)
        return carry

    lax.fori_loop(0, x_ref.shape[0] // NORM_CHUNK, body, 0, unroll=NORM_UNROLL)


def _inproj_kernel(x_ref, g_ref, sc_ref, sh_ref, w_ref, cos_ref, sin_ref, gq_ref, gk_ref,
                   pf_ref, q_ref, h_ref, k_ref, vt_ref):
    j = pl.program_id(1)

    @pl.when(j == 0)
    def _():
        _norm_mod_rows(x_ref, g_ref, sc_ref, sh_ref, h_ref)

    acc = jnp.dot(h_ref[...], w_ref[...], preferred_element_type=F32)

    @pl.when(j < N_PF_TILES)
    def _():
        pf_ref[...] = acc.astype(BF16)

    def rope(xh):
        return _rope(xh, cos_ref[...], sin_ref[...])

    def head(k):
        return acc[:, k * HEAD_DIM:(k + 1) * HEAD_DIM]

    def put_q(k, v):
        q_ref[:, k * HEAD_DIM:(k + 1) * HEAD_DIM] = v.astype(BF16)

    @pl.when(j == N_PF_TILES)
    def _():
        for k in range(N_Q):
            put_q(k, rope(head(k)) * Q_SCALE)

    @pl.when(j == N_PF_TILES + 1)
    def _():
        for k in range(N_KV):
            k_ref[k] = rope(head(k)).astype(BF16)
        vt_ref[0:KV_W, :] = acc[:, KV_W:].T.astype(BF16)

    @pl.when(j == N_PF_TILES + 2)
    def _():
        for k in range(N_Q):
            put_q(k, rope(_head_norm(head(k), gq_ref[...])) * Q_SCALE)

    @pl.when(j == N_PF_TILES + 3)
    def _():
        for k in range(N_KV):
            k_ref[N_KV + k] = rope(_head_norm(head(k), gk_ref[...])).astype(BF16)
        vt_ref[KV_W:, :] = acc[:, KV_W:].T.astype(BF16)


def _proj_block(j):
    n_a, n_b, n_c = 3 * W_BR // TN, (Q_W + 2 * KV_W) // TN, 2 * W_BR // TN
    return jnp.where(j < n_a, j, jnp.where(j < n_a + n_c, j + n_b, jnp.where(j < n_a + n_c + n_b, j - n_c, j)))


def _inproj(x, g, sc, sh, w, cos, sin_signed, gq, gk, *, layer, seq, tm):
    t = x.shape[0]
    nbt = seq // tm
    return pl.pallas_call(
        _inproj_kernel,
        out_shape=(jax.ShapeDtypeStruct((t, PF_W), BF16),
                   jax.ShapeDtypeStruct((t, 2 * Q_W), BF16),
                   jax.ShapeDtypeStruct((t, D_MODEL), BF16),
                   jax.ShapeDtypeStruct((2 * N_KV, t, HEAD_DIM), BF16),
                   jax.ShapeDtypeStruct((2 * KV_W, t), BF16)),
        grid=(t // tm, N_PROJ_TILES),
        in_specs=[pl.BlockSpec((tm, D_MODEL), lambda i, j: (i, 0)),
                  pl.BlockSpec((1, D_MODEL), lambda i, j: (0, 0)),
                  pl.BlockSpec((1, 1, D_MODEL), lambda i, j: (i // nbt, 0, 0)),
                  pl.BlockSpec((1, 1, D_MODEL), lambda i, j: (i // nbt, 0, 0)),
                  pl.BlockSpec((None, D_MODEL, TN), lambda i, j: (layer, 0, _proj_block(j))),
                  pl.BlockSpec((tm, HEAD_DIM), lambda i, j: (i % nbt, 0)),
                  pl.BlockSpec((tm, HEAD_DIM), lambda i, j: (i % nbt, 0)),
                  pl.BlockSpec((1, HEAD_DIM), lambda i, j: (0, 0)),
                  pl.BlockSpec((1, HEAD_DIM), lambda i, j: (0, 0))],
        out_specs=(pl.BlockSpec((tm, TN), lambda i, j: (i, jnp.minimum(j, N_PF_TILES - 1))),
                   pl.BlockSpec((tm, Q_W), lambda i, j: (i, jnp.where(j >= N_PF_TILES + 2, 1, 0))),
                   pl.BlockSpec((tm, D_MODEL), lambda i, j: (i, 0)),
                   pl.BlockSpec((2 * N_KV, tm, HEAD_DIM), lambda i, j: (0, i, 0)),
                   pl.BlockSpec((2 * KV_W, tm), lambda i, j: (0, i))),
        compiler_params=_cparams(("parallel", "arbitrary")),
        name="inproj",
    )(x, g, sc, sh, w, cos, sin_signed, gq, gk)


HALO_A = 16


def _conv_a_kernel(bg_ref, cg_ref, h_ref, cgp_ref, hp_ref, cgn_ref, hn_ref, w_ref, o_ref, u_ref, *, nbt):
    it = pl.program_id(0) % nbt
    tm = bg_ref.shape[0]
    def gated(gate_ref, val_ref):
        return gate_ref[...].astype(F32) * val_ref[...].astype(F32)

    u_ref[HALO_A:HALO_A + tm, :] = gated(cg_ref, h_ref)
    u_ref[0:HALO_A, :] = jnp.where(it > 0, gated(cgp_ref, hp_ref), 0.0)
    u_ref[HALO_A + tm:, :] = jnp.where(it < nbt - 1, gated(cgn_ref, hn_ref), 0.0)
    y = (w_ref[0:1, :] * u_ref[HALO_A - 1:HALO_A - 1 + tm, :]
         + w_ref[1:2, :] * u_ref[HALO_A:HALO_A + tm, :]
         + w_ref[2:3, :] * u_ref[HALO_A + 1:HALO_A + 1 + tm, :])
    o_ref[...] = (bg_ref[...].astype(F32) * y).astype(BF16)


def _conv_a(pf, w, *, seq, tm):
    t = pf.shape[0]
    nbt = seq // tm
    r = tm // HALO_A
    last = t // HALO_A - 1

    def cur(c):
        return pl.BlockSpec((tm, W_BR), lambda i: (i, c))

    def prev(c):
        return pl.BlockSpec((HALO_A, W_BR), lambda i: (jnp.maximum(i * r - 1, 0), c))

    def nxt(c):
        return pl.BlockSpec((HALO_A, W_BR), lambda i: (jnp.minimum((i + 1) * r, last), c))

    return pl.pallas_call(
        functools.partial(_conv_a_kernel, nbt=nbt),
        out_shape=jax.ShapeDtypeStruct((t, W_BR), BF16),
        grid=(t // tm,),
        in_specs=[cur(0), cur(1), cur(2), prev(1), prev(2), nxt(1), nxt(2),
                  pl.BlockSpec((CONV_A, W_BR), lambda i: (0, 0))],
        out_specs=pl.BlockSpec((tm, W_BR), lambda i: (i, 0)),
        scratch_shapes=[pltpu.VMEM((tm + 2 * HALO_A, W_BR), F32)],
        compiler_params=_cparams(("parallel",)),
        name="conv_a",
    )(pf, pf, pf, pf, pf, pf, pf, w)


HALO_C = 16
ROWS_C = 64
SUBLANES = 8


def _conv_c_kernel(v_ref, g_ref, vp_ref, gp_ref, vn_ref, gn_ref, w_ref, b_ref, lg_ref, lb_ref,
                   o_ref, u_ref, us_ref, *, nbt):
    it = pl.program_id(0) % nbt
    tm = v_ref.shape[0]

    def glu(val_ref, gate_ref):
        return val_ref[...].astype(F32) * _sigmoid(gate_ref[...].astype(F32))

    u_ref[HALO_C:HALO_C + tm, :] = glu(v_ref, g_ref)
    u_ref[0:HALO_C, :] = jnp.where(it > 0, glu(vp_ref, gp_ref), 0.0)
    u_ref[HALO_C + tm:, :] = jnp.where(it < nbt - 1, glu(vn_ref, gn_ref), 0.0)
    n_sh = us_ref.shape[1]
    for r in range(1, SUBLANES):
        us_ref[r - 1] = u_ref[r:r + n_sh, :]
    pad = CONV_C // 2

    def tap(row):
        a, r = divmod(row, SUBLANES)
        if r == 0:
            return u_ref[a * SUBLANES:a * SUBLANES + ROWS_C, :]
        return us_ref[r - 1, a * SUBLANES:a * SUBLANES + ROWS_C, :]

    for c in range(tm // ROWS_C):
        base = HALO_C + c * ROWS_C - pad
        acc = w_ref[0:1, :] * tap(base)
        for k in range(1, CONV_C):
            acc = acc + w_ref[k:k + 1, :] * tap(base + k)
        acc = acc + b_ref[...]
        mu = jnp.mean(acc, axis=-1, keepdims=True)
        xc = acc - mu
        var = jnp.mean(xc * xc, axis=-1, keepdims=True)
        y = xc * lax.rsqrt(var + EPS) * lg_ref[...] + lb_ref[...]
        o_ref[c * ROWS_C:(c + 1) * ROWS_C, :] = (y * _sigmoid(y)).astype(BF16)


def _conv_c(pf, w, b, lg, lb, *, seq, tm):
    t = pf.shape[0]
    nbt = seq // tm
    r = tm // HALO_C
    last = t // HALO_C - 1
    c0 = 3

    def cur(c):
        return pl.BlockSpec((tm, W_BR), lambda i: (i, c))

    def prev(c):
        return pl.BlockSpec((HALO_C, W_BR), lambda i: (jnp.maximum(i * r - 1, 0), c))

    def nxt(c):
        return pl.BlockSpec((HALO_C, W_BR), lambda i: (jnp.minimum((i + 1) * r, last), c))

    def vec():
        return pl.BlockSpec((1, W_BR), lambda i: (0, 0))

    return pl.pallas_call(
        functools.partial(_conv_c_kernel, nbt=nbt),
        out_shape=jax.ShapeDtypeStruct((t, W_BR), BF16),
        grid=(t // tm,),
        in_specs=[cur(c0), cur(c0 + 1), prev(c0), prev(c0 + 1), nxt(c0), nxt(c0 + 1),
                  pl.BlockSpec((CONV_C, W_BR), lambda i: (0, 0)), vec(), vec(), vec()],
        out_specs=pl.BlockSpec((tm, W_BR), lambda i: (i, 0)),
        scratch_shapes=[pltpu.VMEM((tm + 2 * HALO_C, W_BR), F32),
                        pltpu.VMEM((SUBLANES - 1, tm + 2 * HALO_C - SUBLANES, W_BR), F32)],
        compiler_params=_cparams(("parallel",)),
        name="conv_c",
    )(pf, pf, pf, pf, pf, pf, w, b, lg, lb)


def _dot_nt(a, b):
    return lax.dot_general(a, b, (((1,), (1,)), ((), ())), preferred_element_type=F32)


WIN_UNIT = 2 * WINDOW
WIN_AHEAD = 5


def _win_kernel(sink_ref, q_ref, kp_ref, kc_ref, kn_ref, kx_ref, vp_ref, vc_ref, vn_ref, vx_ref, o_ref,
                kk_ref, vext_ref, *, nq):
    hk = pl.program_id(1)
    i = pl.program_id(2)
    tq = q_ref.shape[0]
    n_loc = tq + 2 * WINDOW
    n_all = kk_ref.shape[0]
    u = WIN_UNIT
    kk_ref[0:WINDOW, :] = kp_ref[...]
    kk_ref[WINDOW:WINDOW + tq, :] = kc_ref[...]
    kk_ref[WINDOW + tq:n_loc, :] = kn_ref[...]
    kk_ref[n_loc:, :] = kx_ref[...]
    vext_ref[0:HEAD_DIM, 0:WINDOW] = vp_ref[...]
    vext_ref[0:HEAD_DIM, WINDOW:WINDOW + tq] = vc_ref[...]
    vext_ref[0:HEAD_DIM, WINDOW + tq:n_loc] = vn_ref[...]
    vext_ref[0:HEAD_DIM, n_loc:] = vx_ref[...]
    vext_ref[HEAD_DIM:, :] = jnp.ones((ONES_ROWS, n_all), BF16)

    key = lax.broadcasted_iota(jnp.int32, (u, u), 0)
    qry = lax.broadcasted_iota(jnp.int32, (u, u), 1)
    n_loc_blocks = n_loc // u
    n_chunks = tq // u

    def mask(c, kb):
        ok = jnp.abs((kb - c) * u - WINDOW + key - qry) <= WINDOW
        if kb == 0:
            ok = ok & (key >= jnp.where(i > 0, 0, WINDOW))
        if kb == n_loc_blocks - 1:
            ok = ok & (key < jnp.where(i < nq - 1, u, u - WINDOW))
        return ok

    units = []
    for g in range(GROUP):
        for c in range(n_chunks):
            units += [(g, c, kb) for kb in (c, c + 1)] + [(g, c, kb) for kb in range(n_loc_blocks, n_all // u)]

    def scores(unit):
        g, c, kb = unit
        return _dot_nt(kk_ref[kb * u:(kb + 1) * u, :], q_ref[c * u:(c + 1) * u, g * HEAD_DIM:(g + 1) * HEAD_DIM])

    m, acc = {}, {}
    for g in range(GROUP):
        sink = sink_ref[hk * GROUP + g] * LOG2E
        for c in range(n_chunks):
            m[g, c] = jnp.full((1, u), sink, F32)
            acc[g, c] = jnp.concatenate([jnp.zeros((HEAD_DIM, u), F32), jnp.ones((ONES_ROWS, u), F32)], axis=0)

    ahead = [scores(unit) for unit in units[:WIN_AHEAD]]
    for idx, (g, c, kb) in enumerate(units):
        s = ahead.pop(0)
        if idx + WIN_AHEAD < len(units):
            ahead.append(scores(units[idx + WIN_AHEAD]))
        if kb < n_loc_blocks:
            s = jnp.where(mask(c, kb), s, NEG_BIG)
        m_next = jnp.maximum(m[g, c], jnp.max(s, axis=0, keepdims=True))
        alpha = jnp.exp2(m[g, c] - m_next)
        p = jnp.exp2(s - m_next).astype(BF16)
        acc[g, c] = alpha * acc[g, c] + jnp.dot(vext_ref[:, kb * u:(kb + 1) * u], p, preferred_element_type=F32)
        m[g, c] = m_next

    outs = []
    for g in range(GROUP):
        o_t = jnp.concatenate([acc[g, c][0:HEAD_DIM, :] / acc[g, c][HEAD_DIM:HEAD_DIM + 1, :]
                               for c in range(n_chunks)], axis=1)
        outs.append(o_t.T.astype(BF16))
    o_ref[...] = jnp.concatenate(outs, axis=1)


def _win_attn(q, k, vt, k_c, vt_c, sink, *, batch, seq, ctx, tq):
    nq = seq // tq
    r = tq // WINDOW
    nblk = seq // WINDOW

    def prev_blk(b, i):
        return b * nblk + jnp.maximum(i * r - 1, 0)

    def next_blk(b, i):
        return b * nblk + jnp.minimum((i + 1) * r, nblk - 1)

    return pl.pallas_call(
        functools.partial(_win_kernel, nq=nq),
        out_shape=jax.ShapeDtypeStruct((batch * seq, Q_W), BF16),
        grid=(batch, N_KV, nq),
        in_specs=[pl.BlockSpec(memory_space=pltpu.SMEM),
                  pl.BlockSpec((tq, GROUP * HEAD_DIM), lambda b, h, i: (b * nq + i, h)),
                  pl.BlockSpec((None, WINDOW, HEAD_DIM), lambda b, h, i: (h, prev_blk(b, i), 0)),
                  pl.BlockSpec((None, tq, HEAD_DIM), lambda b, h, i: (h, b * nq + i, 0)),
                  pl.BlockSpec((None, WINDOW, HEAD_DIM), lambda b, h, i: (h, next_blk(b, i), 0)),
                  pl.BlockSpec((None, ctx, HEAD_DIM), lambda b, h, i: (h, b, 0)),
                  pl.BlockSpec((HEAD_DIM, WINDOW), lambda b, h, i: (h, prev_blk(b, i))),
                  pl.BlockSpec((HEAD_DIM, tq), lambda b, h, i: (h, b * nq + i)),
                  pl.BlockSpec((HEAD_DIM, WINDOW), lambda b, h, i: (h, next_blk(b, i))),
                  pl.BlockSpec((HEAD_DIM, ctx), lambda b, h, i: (h, b))],
        out_specs=pl.BlockSpec((tq, GROUP * HEAD_DIM), lambda b, h, i: (b * nq + i, h)),
        scratch_shapes=[pltpu.VMEM((tq + 2 * WINDOW + ctx, HEAD_DIM), BF16),
                        pltpu.VMEM((HEAD_DIM + ONES_ROWS, tq + 2 * WINDOW + ctx), BF16)],
        compiler_params=_cparams(("parallel", "parallel", "parallel")),
        name="win_attn",
    )(sink, q, k, k, k, k_c, vt, vt, vt, vt_c)


ONES_ROWS = 16
Q_CHUNK = 256
K_SUB = 256
QK_AHEAD = 5


def _flash_kernel(sink_ref, q_ref, k_ref, vt_ref, *rest, n_main, has_extra, use_sink):
    if has_extra:
        kx_ref, vtx_ref, o_ref, q2_ref, m_ref, acc_ref, vext_ref, vextx_ref = rest
    else:
        o_ref, q2_ref, m_ref, acc_ref, vext_ref = rest
    hk = pl.program_id(1)
    j = pl.program_id(3)
    tq = q_ref.shape[0]

    @pl.when(j == 0)
    def _():
        for g in range(GROUP):
            q2_ref[g * tq:(g + 1) * tq, :] = q_ref[:, g * HEAD_DIM:(g + 1) * HEAD_DIM]
            if use_sink:
                m_ref[:, g * tq:(g + 1) * tq] = jnp.full((1, tq), sink_ref[hk * GROUP + g] * LOG2E, F32)
        if not use_sink:
            m_ref[...] = jnp.full(m_ref.shape, NEG_BIG, F32)
        acc_ref[0:HEAD_DIM, :] = jnp.zeros((HEAD_DIM, GROUP * tq), F32)
        acc_ref[HEAD_DIM:, :] = jnp.full((ONES_ROWS, GROUP * tq), 1.0 if use_sink else 0.0, F32)
        vext_ref[HEAD_DIM:, :] = jnp.ones((ONES_ROWS, vext_ref.shape[1]), BF16)
        if has_extra:
            vextx_ref[HEAD_DIM:, :] = jnp.ones((ONES_ROWS, vextx_ref.shape[1]), BF16)

    n_chunks = GROUP * tq // Q_CHUNK

    def step(kk_ref, vv_ref, vext):
        tkk = kk_ref.shape[0]
        kb = min(tkk, K_SUB)
        units = [(b, c) for b in range(tkk // kb) for c in range(n_chunks)]
        vext[0:HEAD_DIM, :] = vv_ref[...]

        def scores(u):
            b, c = u
            return _dot_nt(kk_ref[b * kb:(b + 1) * kb, :], q2_ref[c * Q_CHUNK:(c + 1) * Q_CHUNK, :])

        m = [m_ref[:, c * Q_CHUNK:(c + 1) * Q_CHUNK] for c in range(n_chunks)]
        acc = [acc_ref[:, c * Q_CHUNK:(c + 1) * Q_CHUNK] for c in range(n_chunks)]
        ahead = [scores(u) for u in units[:QK_AHEAD]]
        for idx, (b, c) in enumerate(units):
            s = ahead.pop(0)
            if idx + QK_AHEAD < len(units):
                ahead.append(scores(units[idx + QK_AHEAD]))
            m_next = jnp.maximum(m[c], jnp.max(s, axis=0, keepdims=True))
            alpha = jnp.exp2(m[c] - m_next)
            p = jnp.exp2(s - m_next).astype(BF16)
            acc[c] = alpha * acc[c] + jnp.dot(vext[:, b * kb:(b + 1) * kb], p, preferred_element_type=F32)
            m[c] = m_next
        acc_ref[...] = jnp.concatenate(acc, axis=1)
        m_ref[...] = jnp.concatenate(m, axis=1)

    @pl.when(j < n_main)
    def _():
        step(k_ref, vt_ref, vext_ref)

    if has_extra:
        @pl.when(j == n_main)
        def _():
            step(kx_ref, vtx_ref, vextx_ref)

    @pl.when(j == n_main + (1 if has_extra else 0) - 1)
    def _():
        o = (acc_ref[0:HEAD_DIM, :] / acc_ref[HEAD_DIM:HEAD_DIM + 1, :]).T
        for g in range(GROUP):
            o_ref[:, g * HEAD_DIM:(g + 1) * HEAD_DIM] = o[g * tq:(g + 1) * tq, :].astype(BF16)


def _flash(q_arr, k_arr, vt_arr, kx_arr, vtx_arr, sink, *, batch, seq_q, seq_k, seq_x, tq, tk, branch, use_sink):
    nq = seq_q // tq
    n_main = seq_k // tk
    has_extra = kx_arr is not None
    n_steps = n_main + (1 if has_extra else 0)
    qc = branch * N_KV
    kr = branch * N_KV

    def jj(j):
        return jnp.minimum(j, n_main - 1)

    in_specs = [pl.BlockSpec(memory_space=pltpu.SMEM),
                pl.BlockSpec((tq, GROUP * HEAD_DIM), lambda b, h, i, j: (b * nq + i, qc + h)),
                pl.BlockSpec((None, tk, HEAD_DIM), lambda b, h, i, j: (kr + h, b * n_main + jj(j), 0)),
                pl.BlockSpec((HEAD_DIM, tk), lambda b, h, i, j: (kr + h, b * n_main + jj(j)))]
    args = [sink, q_arr, k_arr, vt_arr]
    rows = GROUP * tq
    scratch = [pltpu.VMEM((rows, HEAD_DIM), BF16),
               pltpu.VMEM((1, rows), F32),
               pltpu.VMEM((HEAD_DIM + ONES_ROWS, rows), F32),
               pltpu.VMEM((HEAD_DIM + ONES_ROWS, tk), BF16)]
    if has_extra:
        in_specs += [pl.BlockSpec((None, seq_x, HEAD_DIM), lambda b, h, i, j: (kr + h, b, 0)),
                     pl.BlockSpec((HEAD_DIM, seq_x), lambda b, h, i, j: (kr + h, b))]
        args += [kx_arr, vtx_arr]
        scratch += [pltpu.VMEM((HEAD_DIM + ONES_ROWS, seq_x), BF16)]
    return pl.pallas_call(
        functools.partial(_flash_kernel, n_main=n_main, has_extra=has_extra, use_sink=use_sink),
        out_shape=jax.ShapeDtypeStruct((batch * seq_q, Q_W), BF16),
        grid=(batch, N_KV, nq, n_steps),
        in_specs=in_specs,
        out_specs=pl.BlockSpec((tq, GROUP * HEAD_DIM), lambda b, h, i, j: (b * nq + i, h)),
        scratch_shapes=scratch,
        compiler_params=_cparams(("parallel", "parallel", "parallel", "arbitrary")),
        name="flash",
    )(*args)


def _merge_kernel(h_ref, ya_ref, yb_ref, yc_ref, yd_ref, wga_ref, wgb_ref, wgc_ref, wgd_ref,
                  wb_ref, bg_ref, o_ref):
    h = h_ref[...]
    m = None
    for k, (y_ref, wg_ref) in enumerate(((ya_ref, wga_ref), (yb_ref, wgb_ref), (yc_ref, wgc_ref), (yd_ref, wgd_ref))):
        logits = jnp.dot(h, wg_ref[...], preferred_element_type=F32) + bg_ref[k:k + 1, :]
        t = _sigmoid(logits) * jnp.dot(y_ref[...], wb_ref[k], preferred_element_type=F32)
        m = t if m is None else m + t
    o_ref[...] = m.astype(BF16)


def _merge(h, ys, wg, wb, bg, *, layer, tm, tn):
    t = h.shape[0]
    nct = D_MODEL // tn

    def y():
        return pl.BlockSpec((tm, W_BR), lambda i, j: (i, 0))

    def g(k):
        return pl.BlockSpec((None, D_MODEL, tn), lambda i, j: (layer, 0, OFF_G // tn + k * nct + j))

    return pl.pallas_call(
        _merge_kernel,
        out_shape=jax.ShapeDtypeStruct((t, D_MODEL), BF16),
        grid=(t // tm, nct),
        in_specs=[pl.BlockSpec((tm, D_MODEL), lambda i, j: (i, 0)), y(), y(), y(), y(),
                  g(0), g(1), g(2), g(3),
                  pl.BlockSpec((None, N_BRANCH, W_BR, tn), lambda i, j: (layer, 0, 0, j)),
                  pl.BlockSpec((N_BRANCH, tn), lambda i, j: (0, j))],
        out_specs=pl.BlockSpec((tm, tn), lambda i, j: (i, j)),
        compiler_params=_cparams(("parallel", "arbitrary")),
        name="merge",
    )(h, *ys, wg, wg, wg, wg, wb, bg)


def _resid_kernel(a_ref, w_ref, x_ref, gate_ref, o_ref, *scratch, nk):
    part = jnp.dot(a_ref[...], w_ref[...], preferred_element_type=F32)
    if nk == 1:
        o_ref[...] = x_ref[...] + gate_ref[0] * part
        return
    (acc_ref,) = scratch
    k = pl.program_id(1)

    @pl.when(k == 0)
    def _():
        acc_ref[...] = part

    if nk > 2:
        @pl.when((k > 0) & (k < nk - 1))
        def _():
            acc_ref[...] += part

    @pl.when(k == nk - 1)
    def _():
        o_ref[...] = x_ref[...] + gate_ref[0] * (acc_ref[...] + part)


def _resid(a, w, x, gate, *, layer, seq, tm, tk):
    t, k = a.shape
    nbt = seq // tm
    nk = k // tk
    return pl.pallas_call(
        functools.partial(_resid_kernel, nk=nk),
        out_shape=jax.ShapeDtypeStruct((t, D_MODEL), F32),
        grid=(t // tm, nk),
        in_specs=[pl.BlockSpec((tm, tk), lambda i, j: (i, j)),
                  pl.BlockSpec((None, tk, D_MODEL), lambda i, j: (layer, j, 0)),
                  pl.BlockSpec((tm, D_MODEL), lambda i, j: (i, 0)),
                  pl.BlockSpec((1, 1, D_MODEL), lambda i, j: (i // nbt, 0, 0))],
        out_specs=pl.BlockSpec((tm, D_MODEL), lambda i, j: (i, 0)),
        scratch_shapes=[pltpu.VMEM((tm, D_MODEL), F32)] if nk > 1 else [],
        compiler_params=_cparams(("parallel", "arbitrary")),
        name="resid",
    )(a, w, x, gate)


def _resid_norm_kernel(a_ref, w_ref, x_ref, gate_ref, gn_ref, o_ref, acc_ref, *, nk):
    k = pl.program_id(1)
    part = jnp.dot(a_ref[...], w_ref[...], preferred_element_type=F32)

    @pl.when(k == 0)
    def _():
        acc_ref[...] = part

    @pl.when(k > 0)
    def _():
        acc_ref[...] += part

    @pl.when(k == nk - 1)
    def _():
        gate = gate_ref[0]
        gn = gn_ref[...]

        def body(i, carry):
            rows = pl.ds(pl.multiple_of(i * NORM_CHUNK, NORM_CHUNK), NORM_CHUNK)
            y = x_ref[rows, :] + gate * acc_ref[rows, :]
            o_ref[rows, :] = y * lax.rsqrt(jnp.mean(y * y, axis=-1, keepdims=True) + EPS) * gn
            return carry

        lax.fori_loop(0, o_ref.shape[0] // NORM_CHUNK, body, 0, unroll=NORM_UNROLL)


def _resid_norm(a, w, x, gate, gn, *, layer, seq, tm, tk):
    t, k = a.shape
    nbt = seq // tm
    nk = k // tk
    return pl.pallas_call(
        functools.partial(_resid_norm_kernel, nk=nk),
        out_shape=jax.ShapeDtypeStruct((t, D_MODEL), F32),
        grid=(t // tm, nk),
        in_specs=[pl.BlockSpec((tm, tk), lambda i, j: (i, j)),
                  pl.BlockSpec((None, tk, D_MODEL), lambda i, j: (layer, j, 0)),
                  pl.BlockSpec((tm, D_MODEL), lambda i, j: (i, 0)),
                  pl.BlockSpec((1, 1, D_MODEL), lambda i, j: (i // nbt, 0, 0)),
                  pl.BlockSpec((1, D_MODEL), lambda i, j: (0, 0))],
        out_specs=pl.BlockSpec((tm, D_MODEL), lambda i, j: (i, 0)),
        scratch_shapes=[pltpu.VMEM((tm, D_MODEL), F32)],
        compiler_params=_cparams(("parallel", "arbitrary")),
        name="resid_norm",
    )(a, w, x, gate, gn)


def _ffn_in_kernel(x_ref, g_ref, sc_ref, sh_ref, wa_ref, wb_ref, o_ref, h_ref):
    @pl.when(pl.program_id(1) == 0)
    def _():
        _norm_mod_rows(x_ref, g_ref, sc_ref, sh_ref, h_ref)

    h = h_ref[...]
    a = jnp.dot(h, wa_ref[...], preferred_element_type=F32)
    b = jnp.dot(h, wb_ref[...], preferred_element_type=F32)
    o_ref[...] = (a * _sigmoid(a) * b).astype(BF16)


def _ffn_in(x, g, sc, sh, w, *, layer, seq, tm, tn):
    t = x.shape[0]
    nbt = seq // tm
    nct = FFN_HIDDEN // tn
    return pl.pallas_call(
        _ffn_in_kernel,
        out_shape=jax.ShapeDtypeStruct((t, FFN_HIDDEN), BF16),
        grid=(t // tm, nct),
        in_specs=[pl.BlockSpec((tm, D_MODEL), lambda i, j: (i, 0)),
                  pl.BlockSpec((1, D_MODEL), lambda i, j: (0, 0)),
                  pl.BlockSpec((1, 1, D_MODEL), lambda i, j: (i // nbt, 0, 0)),
                  pl.BlockSpec((1, 1, D_MODEL), lambda i, j: (i // nbt, 0, 0)),
                  pl.BlockSpec((None, D_MODEL, tn), lambda i, j: (layer, 0, j)),
                  pl.BlockSpec((None, D_MODEL, tn), lambda i, j: (layer, 0, nct + j))],
        out_specs=pl.BlockSpec((tm, tn), lambda i, j: (i, j)),
        scratch_shapes=[pltpu.VMEM((tm, D_MODEL), BF16)],
        compiler_params=_cparams(("parallel", "arbitrary")),
        name="ffn_in",
    )(x, g, sc, sh, w, w)


def _rope_tables(seq):
    n_rows = seq // GRID_W
    inv = ROPE_THETA ** (-jnp.arange(0, ROT_AXIS, 2, dtype=F32) / ROT_AXIS)
    ar = jnp.arange(n_rows, dtype=F32)[:, None] * inv
    ac = jnp.arange(GRID_W, dtype=F32)[:, None] * inv
    shape = (n_rows, GRID_W, ROT_AXIS // 2)

    def grid(fr, fc):
        r = jnp.broadcast_to(fr[:, None, :], shape)
        c = jnp.broadcast_to(fc[None, :, :], shape)
        return r, c

    cr, cc = grid(jnp.cos(ar), jnp.cos(ac))
    sr, sc = grid(jnp.sin(ar), jnp.sin(ac))
    cos = jnp.concatenate([cr, cc, cr, cc], axis=-1).reshape(seq, HEAD_DIM)
    sin_signed = jnp.concatenate([-sr, -sc, sr, sc], axis=-1).reshape(seq, HEAD_DIM)
    return cos, sin_signed


def _pair_split(v):
    quarter = HEAD_DIM // 4
    q = v.reshape(v.shape[:-1] + (v.shape[-1] // HEAD_DIM, 4, quarter))
    q = jnp.concatenate([q[..., 0:1, :], q[..., 2:3, :], q[..., 1:2, :], q[..., 3:4, :]], axis=-2)
    return q.reshape(v.shape)


CAST_ROWS = 128
ROTARY_SPANS = ((OFF_BQ, OFF_BV), (OFF_DQ, OFF_DV))


def _cast_w_in_kernel(w_ref, o_ref):
    quarter = HEAD_DIM // 4
    lane = lax.broadcasted_iota(jnp.int32, (w_ref.shape[0], HEAD_DIM), 1)
    second = (lane >= quarter) & (lane < 2 * quarter)
    third = (lane >= 2 * quarter) & (lane < 3 * quarter)
    edges = sorted({0, w_ref.shape[1]} | {e for span in ROTARY_SPANS for e in span})
    for lo, hi in zip(edges[:-1], edges[1:]):
        if (lo, hi) in ROTARY_SPANS:
            for c in range(lo, hi, HEAD_DIM):
                xh = w_ref[:, c:c + HEAD_DIM]
                xh = jnp.where(second, pltpu.roll(xh, HEAD_DIM - quarter, 1),
                               jnp.where(third, pltpu.roll(xh, quarter, 1), xh))
                o_ref[:, c:c + HEAD_DIM] = xh.astype(BF16)
        else:
            o_ref[:, lo:hi] = w_ref[:, lo:hi].astype(BF16)


def _cast_w_in(w):
    depth, d, p = w.shape
    return pl.pallas_call(
        _cast_w_in_kernel,
        out_shape=jax.ShapeDtypeStruct(w.shape, BF16),
        grid=(depth, d // CAST_ROWS),
        in_specs=[pl.BlockSpec((None, CAST_ROWS, p), lambda l, i: (l, i, 0))],
        out_specs=pl.BlockSpec((None, CAST_ROWS, p), lambda l, i: (l, i, 0)),
        compiler_params=_cparams(("parallel", "parallel")),
        name="cast_w_in",
    )(w)


def _tile(n, pref):
    return pref if n % pref == 0 else n


class _Tiles(NamedTuple):
    proj: int
    rows: int
    conv_a: int
    conv_c: int
    win_q: int
    flash_q: int
    flash_k: int


def _tiles(seq):
    return _Tiles(proj=_tile(seq, 1024), rows=_tile(seq, 512), conv_a=_tile(seq, 512), conv_c=_tile(seq, 512),
                  win_q=_tile(seq, 512), flash_q=_tile(seq, 1024), flash_k=_tile(seq, 4096))


def _forward(x, c, ctx, c_ctx, w_ada, b_ada, norm_mix, norm_ffn, w_in, b_gate, conv_a_w, sink_b,
             qk_norm_q, qk_norm_k, conv_c_w, conv_c_b, ln_c_g, ln_c_b, w_branch, w_out,
             w_ffn_in, w_ffn_out, norm_final):
    batch, seq, _ = x.shape
    n_ctx = ctx.shape[1]
    depth = w_in.shape[0]
    xs = x.reshape(batch * seq, D_MODEL)
    xc = ctx.reshape(batch * n_ctx, D_MODEL)

    tiles = _tiles(seq)
    tm = tiles.proj

    rope_x = _rope_tables(seq)
    rope_c = (jnp.ones((n_ctx, HEAD_DIM), F32), jnp.zeros((n_ctx, HEAD_DIM), F32))
    cvec = jnp.zeros((ADA_ROWS, D_MODEL), F32).at[:batch].set(c).at[batch].set(c_ctx)

    def row(v):
        return v.reshape(1, -1)

    w_proj = w_gate = _cast_w_in(w_in)
    w_br = w_branch.astype(BF16)
    w_o = w_out.astype(BF16)
    w_f1 = w_ffn_in.astype(BF16)
    w_f2 = w_ffn_out.astype(BF16)

    def mixers(pf, q, k, vt, k_c, vt_c, l, *, s, tma, tmc, is_ctx):
        y_a = _conv_a(pf, conv_a_w[l], seq=s, tm=tma)
        y_c = _conv_c(pf, conv_c_w[l], row(conv_c_b[l]), row(ln_c_g[l]), row(ln_c_b[l]), seq=s, tm=tmc)
        if is_ctx:
            y_b = _flash(q, k, vt, None, None, sink_b[l], batch=batch, seq_q=s, seq_k=s, seq_x=0, tq=s, tk=s,
                         branch=0, use_sink=True)
            y_d = _flash(q, k, vt, None, None, sink_b[l], batch=batch, seq_q=s, seq_k=s, seq_x=0, tq=s, tk=s,
                         branch=1, use_sink=False)
        else:
            y_b = _win_attn(q, k, vt, k_c, vt_c, sink_b[l], batch=batch, seq=s, ctx=n_ctx, tq=tiles.win_q)
            y_d = _flash(q, k, vt, k_c, vt_c, sink_b[l], batch=batch, seq_q=s, seq_k=s, seq_x=n_ctx,
                         tq=tiles.flash_q, tk=tiles.flash_k, branch=1, use_sink=False)
        return [y_a, y_b, y_c, y_d]

    for l in range(depth):
        last = l == depth - 1
        mod = _ada(cvec, w_ada, row(b_ada[l]), layer=l)
        mx = mod[:batch].reshape(batch, 1, 6, D_MODEL)
        mcx = jnp.broadcast_to(mod[batch].reshape(1, 1, 6, D_MODEL), (batch, 1, 6, D_MODEL))
        sh_m, sc_m, g_m, sh_f, sc_f, g_f = (mx[:, :, k] for k in range(6))
        csh_m, csc_m, cg_m, csh_f, csc_f, cg_f = (mcx[:, :, k] for k in range(6))
        gq, gk = row(_pair_split(qk_norm_q[l])), row(_pair_split(qk_norm_k[l]))

        pf_c, q_c, h_c, k_c, vt_c = _inproj(xc, row(norm_mix[l]), csc_m, csh_m, w_proj, *rope_c, gq, gk,
                                            layer=l, seq=n_ctx, tm=n_ctx)
        pf, q, h, k, vt = _inproj(xs, row(norm_mix[l]), sc_m, sh_m, w_proj, *rope_x, gq, gk,
                                  layer=l, seq=seq, tm=tm)

        ys = mixers(pf, q, k, vt, k_c, vt_c, l, s=seq, tma=tiles.conv_a, tmc=tiles.conv_c, is_ctx=False)
        m = _merge(h, ys, w_gate, w_br, b_gate[l], layer=l, tm=tm, tn=512)
        xs = _resid(m, w_o, xs, g_m, layer=l, seq=seq, tm=tiles.rows, tk=D_MODEL)

        if not last:
            ys_c = mixers(pf_c, q_c, k_c, vt_c, None, None, l, s=n_ctx, tma=n_ctx, tmc=n_ctx, is_ctx=True)
            t_c = batch * n_ctx
            m_c = _merge(h_c, ys_c, w_gate, w_br, b_gate[l], layer=l, tm=t_c, tn=512)
            xc = _resid(m_c, w_o, xc, cg_m, layer=l, seq=t_c, tm=t_c, tk=D_MODEL)
            hid_c = _ffn_in(xc, row(norm_ffn[l]), csc_f, csh_f, w_f1, layer=l, seq=t_c, tm=t_c, tn=512)
            xc = _resid(hid_c, w_f2, xc, cg_f, layer=l, seq=t_c, tm=t_c, tk=FFN_HIDDEN // 2)

        hid = _ffn_in(xs, row(norm_ffn[l]), sc_f, sh_f, w_f1, layer=l, seq=seq, tm=tm, tn=512)
        if last:
            xs = _resid_norm(hid, w_f2, xs, g_f, row(norm_final), layer=l, seq=seq, tm=tiles.rows,
                             tk=FFN_HIDDEN // 2)
        else:
            xs = _resid(hid, w_f2, xs, g_f, layer=l, seq=seq, tm=tiles.rows, tk=FFN_HIDDEN // 2)

    return xs.reshape(batch, seq, D_MODEL)


def kernel(x, c, ctx, c_ctx, w_ada, b_ada, norm_mix, norm_ffn, w_in, b_gate, conv_a_w, sink_b, qk_norm_q,
           qk_norm_k, conv_c_w, conv_c_b, ln_c_g, ln_c_b, w_branch, w_out, w_ffn_in, w_ffn_out, norm_final):
    return _forward(x, c, ctx, c_ctx, w_ada, b_ada, norm_mix, norm_ffn, w_in, b_gate, conv_a_w, sink_b,
                    qk_norm_q, qk_norm_k, conv_c_w, conv_c_b, ln_c_g, ln_c_b, w_branch, w_out,
                    w_ffn_in, w_ffn_out, norm_final)
```

```python
import functools
from typing import NamedTuple

import jax
import jax.numpy as jnp
from jax import lax
from jax.experimental import pallas as pl
from jax.experimental.pallas import tpu as pltpu

F32 = jnp.float32
BF16 = jnp.bfloat16

D_MODEL = 2048
N_BRANCH = 4
W_BR = D_MODEL // 4
HEAD_DIM = 128
N_Q = W_BR // HEAD_DIM
N_KV = N_Q // 2
GROUP = N_Q // N_KV
Q_W = N_Q * HEAD_DIM
KV_W = N_KV * HEAD_DIM
CONV_A = 3
CONV_C = 31
WINDOW = 128
GRID_W = 64
ROT_AXIS = HEAD_DIM // 2
ROPE_THETA = 10000.0
FFN_HIDDEN = 5632
EPS = 1e-6
ATTN_SCALE = HEAD_DIM ** -0.5
LOG2E = 1.4426950408889634
Q_SCALE = ATTN_SCALE * LOG2E
NEG_BIG = -1e30

OFF_A = 0
OFF_BQ = OFF_A + 3 * W_BR
OFF_BK = OFF_BQ + Q_W
OFF_BV = OFF_BK + KV_W
OFF_C = OFF_BV + KV_W
OFF_DQ = OFF_C + 2 * W_BR
OFF_DK = OFF_DQ + Q_W
OFF_DV = OFF_DK + KV_W
OFF_G = OFF_DV + KV_W

PF_W = 3 * W_BR + 2 * W_BR
PQ_W = 2 * (Q_W + 2 * KV_W)
TN = 512
N_PF_TILES = PF_W // TN
N_PROJ_TILES = (PF_W + PQ_W) // TN

VMEM_LIMIT = 56 * 1024 * 1024


def _cparams(sem):
    return pltpu.CompilerParams(dimension_semantics=sem, vmem_limit_bytes=VMEM_LIMIT)


def _sigmoid(z):
    return 1.0 / (1.0 + jnp.exp(-z))


def _ada_kernel(c_ref, w_ref, b_ref, o_ref):
    c = c_ref[...]
    s = (c * _sigmoid(c)).astype(BF16)
    o_ref[...] = jnp.dot(s, w_ref[...].astype(BF16), preferred_element_type=F32) + b_ref[...]


ADA_ROWS = 16


def _ada(cvec, w, b, *, layer):
    n = w.shape[2]
    tn = 1024
    return pl.pallas_call(
        _ada_kernel,
        out_shape=jax.ShapeDtypeStruct((ADA_ROWS, n), F32),
        grid=(n // tn,),
        in_specs=[pl.BlockSpec((ADA_ROWS, D_MODEL), lambda j: (0, 0)),
                  pl.BlockSpec((None, D_MODEL, tn), lambda j: (layer, 0, j)),
                  pl.BlockSpec((1, tn), lambda j: (0, j))],
        out_specs=pl.BlockSpec((ADA_ROWS, tn), lambda j: (0, j)),
        compiler_params=_cparams(("arbitrary",)),
        name="ada",
    )(cvec, w, b)


def _rope(xh, cos, sin_signed):
    return xh * cos + pltpu.roll(xh, HEAD_DIM // 2, 1) * sin_signed


def _head_norm(xh, g):
    return xh * lax.rsqrt(jnp.mean(xh * xh, axis=-1, keepdims=True) + EPS) * g


NORM_CHUNK = 16
NORM_UNROLL = 8


def _norm_mod_rows(x_ref, g_ref, sc_ref, sh_ref, h_ref):
    gs = g_ref[...] * (1.0 + sc_ref[0])
    sh = sh_ref[0]

    def body(i, carry):
        rows = pl.ds(pl.multiple_of(i * NORM_CHUNK, NORM_CHUNK), NORM_CHUNK)
        x = x_ref[rows, :]
        r = lax.rsqrt(jnp.mean(x * x, axis=-1, keepdims=True) + EPS)
        h_ref[rows, :] = (x * r * gs + sh).astype(BF16)
        return carry

    lax.fori_loop(0, x_ref.shape[0] // NORM_CHUNK, body, 0, unroll=NORM_UNROLL)


def _inproj_kernel(x_ref, g_ref, sc_ref, sh_ref, w_ref, cos_ref, sin_ref, gq_ref, gk_ref,
                   pf_ref, q_ref, h_ref, k_ref, vt_ref):
    j = pl.program_id(1)

    @pl.when(j == 0)
    def _():
        _norm_mod_rows(x_ref, g_ref, sc_ref, sh_ref, h_ref)

    acc = jnp.dot(h_ref[...], w_ref[...], preferred_element_type=F32)

    @pl.when(j < N_PF_TILES)
    def _():
        pf_ref[...] = acc.astype(BF16)

    def rope(xh):
        return _rope(xh, cos_ref[...], sin_ref[...])

    def head(k):
        return acc[:, k * HEAD_DIM:(k + 1) * HEAD_DIM]

    def put_q(k, v):
        q_ref[:, k * HEAD_DIM:(k + 1) * HEAD_DIM] = v.astype(BF16)

    @pl.when(j == N_PF_TILES)
    def _():
        for k in range(N_Q):
            put_q(k, rope(head(k)) * Q_SCALE)

    @pl.when(j == N_PF_TILES + 1)
    def _():
        for k in range(N_KV):
            k_ref[k] = rope(head(k)).astype(BF16)
        vt_ref[0:KV_W, :] = acc[:, KV_W:].T.astype(BF16)

    @pl.when(j == N_PF_TILES + 2)
    def _():
        for k in range(N_Q):
            put_q(k, rope(_head_norm(head(k), gq_ref[...])) * Q_SCALE)

    @pl.when(j == N_PF_TILES + 3)
    def _():
        for k in range(N_KV):
            k_ref[N_KV + k] = rope(_head_norm(head(k), gk_ref[...])).astype(BF16)
        vt_ref[KV_W:, :] = acc[:, KV_W:].T.astype(BF16)


def _proj_block(j):
    n_a, n_b, n_c = 3 * W_BR // TN, (Q_W + 2 * KV_W) // TN, 2 * W_BR // TN
    return jnp.where(j < n_a, j, jnp.where(j < n_a + n_c, j + n_b, jnp.where(j < n_a + n_c + n_b, j - n_c, j)))


def _inproj(x, g, sc, sh, w, cos, sin_signed, gq, gk, *, layer, seq, tm):
    t = x.shape[0]
    nbt = seq // tm
    return pl.pallas_call(
        _inproj_kernel,
        out_shape=(jax.ShapeDtypeStruct((t, PF_W), BF16),
                   jax.ShapeDtypeStruct((t, 2 * Q_W), BF16),
                   jax.ShapeDtypeStruct((t, D_MODEL), BF16),
                   jax.ShapeDtypeStruct((2 * N_KV, t, HEAD_DIM), BF16),
                   jax.ShapeDtypeStruct((2 * KV_W, t), BF16)),
        grid=(t // tm, N_PROJ_TILES),
        in_specs=[pl.BlockSpec((tm, D_MODEL), lambda i, j: (i, 0)),
                  pl.BlockSpec((1, D_MODEL), lambda i, j: (0, 0)),
                  pl.BlockSpec((1, 1, D_MODEL), lambda i, j: (i // nbt, 0, 0)),
                  pl.BlockSpec((1, 1, D_MODEL), lambda i, j: (i // nbt, 0, 0)),
                  pl.BlockSpec((None, D_MODEL, TN), lambda i, j: (layer, 0, _proj_block(j))),
                  pl.BlockSpec((tm, HEAD_DIM), lambda i, j: (i % nbt, 0)),
                  pl.BlockSpec((tm, HEAD_DIM), lambda i, j: (i % nbt, 0)),
                  pl.BlockSpec((1, HEAD_DIM), lambda i, j: (0, 0)),
                  pl.BlockSpec((1, HEAD_DIM), lambda i, j: (0, 0))],
        out_specs=(pl.BlockSpec((tm, TN), lambda i, j: (i, jnp.minimum(j, N_PF_TILES - 1))),
                   pl.BlockSpec((tm, Q_W), lambda i, j: (i, jnp.where(j >= N_PF_TILES + 2, 1, 0))),
                   pl.BlockSpec((tm, D_MODEL), lambda i, j: (i, 0)),
                   pl.BlockSpec((2 * N_KV, tm, HEAD_DIM), lambda i, j: (0, i, 0)),
                   pl.BlockSpec((2 * KV_W, tm), lambda i, j: (0, i))),
        compiler_params=_cparams(("parallel", "arbitrary")),
        name="inproj",
    )(x, g, sc, sh, w, cos, sin_signed, gq, gk)


HALO_A = 16


def _conv_a_kernel(bg_ref, cg_ref, h_ref, cgp_ref, hp_ref, cgn_ref, hn_ref, w_ref, o_ref, u_ref, *, nbt):
    it = pl.program_id(0) % nbt
    tm = bg_ref.shape[0]
    def gated(gate_ref, val_ref):
        return gate_ref[...].astype(F32) * val_ref[...].astype(F32)

    u_ref[HALO_A:HALO_A + tm, :] = gated(cg_ref, h_ref)
    u_ref[0:HALO_A, :] = jnp.where(it > 0, gated(cgp_ref, hp_ref), 0.0)
    u_ref[HALO_A + tm:, :] = jnp.where(it < nbt - 1, gated(cgn_ref, hn_ref), 0.0)
    y = (w_ref[0:1, :] * u_ref[HALO_A - 1:HALO_A - 1 + tm, :]
         + w_ref[1:2, :] * u_ref[HALO_A:HALO_A + tm, :]
         + w_ref[2:3, :] * u_ref[HALO_A + 1:HALO_A + 1 + tm, :])
    o_ref[...] = (bg_ref[...].astype(F32) * y).astype(BF16)


def _conv_a(pf, w, *, seq, tm):
    t = pf.shape[0]
    nbt = seq // tm
    r = tm // HALO_A
    last = t // HALO_A - 1

    def cur(c):
        return pl.BlockSpec((tm, W_BR), lambda i: (i, c))

    def prev(c):
        return pl.BlockSpec((HALO_A, W_BR), lambda i: (jnp.maximum(i * r - 1, 0), c))

    def nxt(c):
        return pl.BlockSpec((HALO_A, W_BR), lambda i: (jnp.minimum((i + 1) * r, last), c))

    return pl.pallas_call(
        functools.partial(_conv_a_kernel, nbt=nbt),
        out_shape=jax.ShapeDtypeStruct((t, W_BR), BF16),
        grid=(t // tm,),
        in_specs=[cur(0), cur(1), cur(2), prev(1), prev(2), nxt(1), nxt(2),
                  pl.BlockSpec((CONV_A, W_BR), lambda i: (0, 0))],
        out_specs=pl.BlockSpec((tm, W_BR), lambda i: (i, 0)),
        scratch_shapes=[pltpu.VMEM((tm + 2 * HALO_A, W_BR), F32)],
        compiler_params=_cparams(("parallel",)),
        name="conv_a",
    )(pf, pf, pf, pf, pf, pf, pf, w)


HALO_C = 16
ROWS_C = 64
SUBLANES = 8


def _conv_c_kernel(v_ref, g_ref, vp_ref, gp_ref, vn_ref, gn_ref, w_ref, b_ref, lg_ref, lb_ref,
                   o_ref, u_ref, us_ref, *, nbt):
    it = pl.program_id(0) % nbt
    tm = v_ref.shape[0]

    def glu(val_ref, gate_ref):
        return val_ref[...].astype(F32) * _sigmoid(gate_ref[...].astype(F32))

    u_ref[HALO_C:HALO_C + tm, :] = glu(v_ref, g_ref)
    u_ref[0:HALO_C, :] = jnp.where(it > 0, glu(vp_ref, gp_ref), 0.0)
    u_ref[HALO_C + tm:, :] = jnp.where(it < nbt - 1, glu(vn_ref, gn_ref), 0.0)
    n_sh = us_ref.shape[1]
    for r in range(1, SUBLANES):
        us_ref[r - 1] = u_ref[r:r + n_sh, :]
    pad = CONV_C // 2

    def tap(row):
        a, r = divmod(row, SUBLANES)
        if r == 0:
            return u_ref[a * SUBLANES:a * SUBLANES + ROWS_C, :]
        return us_ref[r - 1, a * SUBLANES:a * SUBLANES + ROWS_C, :]

    for c in range(tm // ROWS_C):
        base = HALO_C + c * ROWS_C - pad
        acc = w_ref[0:1, :] * tap(base)
        for k in range(1, CONV_C):
            acc = acc + w_ref[k:k + 1, :] * tap(base + k)
        acc = acc + b_ref[...]
        mu = jnp.mean(acc, axis=-1, keepdims=True)
        xc = acc - mu
        var = jnp.mean(xc * xc, axis=-1, keepdims=True)
        y = xc * lax.rsqrt(var + EPS) * lg_ref[...] + lb_ref[...]
        o_ref[c * ROWS_C:(c + 1) * ROWS_C, :] = (y * _sigmoid(y)).astype(BF16)


def _conv_c(pf, w, b, lg, lb, *, seq, tm):
    t = pf.shape[0]
    nbt = seq // tm
    r = tm // HALO_C
    last = t // HALO_C - 1
    c0 = 3

    def cur(c):
        return pl.BlockSpec((tm, W_BR), lambda i: (i, c))

    def prev(c):
        return pl.BlockSpec((HALO_C, W_BR), lambda i: (jnp.maximum(i * r - 1, 0), c))

    def nxt(c):
        return pl.BlockSpec((HALO_C, W_BR), lambda i: (jnp.minimum((i + 1) * r, last), c))

    def vec():
        return pl.BlockSpec((1, W_BR), lambda i: (0, 0))

    return pl.pallas_call(
        functools.partial(_conv_c_kernel, nbt=nbt),
        out_shape=jax.ShapeDtypeStruct((t, W_BR), BF16),
        grid=(t // tm,),
        in_specs=[cur(c0), cur(c0 + 1), prev(c0), prev(c0 + 1), nxt(c0), nxt(c0 + 1),
                  pl.BlockSpec((CONV_C, W_BR), lambda i: (0, 0)), vec(), vec(), vec()],
        out_specs=pl.BlockSpec((tm, W_BR), lambda i: (i, 0)),
        scratch_shapes=[pltpu.VMEM((tm + 2 * HALO_C, W_BR), F32),
                        pltpu.VMEM((SUBLANES - 1, tm + 2 * HALO_C - SUBLANES, W_BR), F32)],
        compiler_params=_cparams(("parallel",)),
        name="conv_c",
    )(pf, pf, pf, pf, pf, pf, w, b, lg, lb)


def _dot_nt(a, b):
    return lax.dot_general(a, b, (((1,), (1,)), ((), ())), preferred_element_type=F32)


WIN_UNIT = 2 * WINDOW
WIN_AHEAD = 5


def _win_kernel(sink_ref, q_ref, kp_ref, kc_ref, kn_ref, kx_ref, vp_ref, vc_ref, vn_ref, vx_ref, o_ref,
                kk_ref, vext_ref, *, nq):
    hk = pl.program_id(1)
    i = pl.program_id(2)
    tq = q_ref.shape[0]
    n_loc = tq + 2 * WINDOW
    n_all = kk_ref.shape[0]
    u = WIN_UNIT
    kk_ref[0:WINDOW, :] = kp_ref[...]
    kk_ref[WINDOW:WINDOW + tq, :] = kc_ref[...]
    kk_ref[WINDOW + tq:n_loc, :] = kn_ref[...]
    kk_ref[n_loc:, :] = kx_ref[...]
    vext_ref[0:HEAD_DIM, 0:WINDOW] = vp_ref[...]
    vext_ref[0:HEAD_DIM, WINDOW:WINDOW + tq] = vc_ref[...]
    vext_ref[0:HEAD_DIM, WINDOW + tq:n_loc] = vn_ref[...]
    vext_ref[0:HEAD_DIM, n_loc:] = vx_ref[...]
    vext_ref[HEAD_DIM:, :] = jnp.ones((ONES_ROWS, n_all), BF16)

    key = lax.broadcasted_iota(jnp.int32, (u, u), 0)
    qry = lax.broadcasted_iota(jnp.int32, (u, u), 1)
    n_loc_blocks = n_loc // u
    n_chunks = tq // u

    def mask(c, kb):
        ok = jnp.abs((kb - c) * u - WINDOW + key - qry) <= WINDOW
        if kb == 0:
            ok = ok & (key >= jnp.where(i > 0, 0, WINDOW))
        if kb == n_loc_blocks - 1:
            ok = ok & (key < jnp.where(i < nq - 1, u, u - WINDOW))
        return ok

    units = []
    for g in range(GROUP):
        for c in range(n_chunks):
            units += [(g, c, kb) for kb in (c, c + 1)] + [(g, c, kb) for kb in range(n_loc_blocks, n_all // u)]

    def scores(unit):
        g, c, kb = unit
        return _dot_nt(kk_ref[kb * u:(kb + 1) * u, :], q_ref[c * u:(c + 1) * u, g * HEAD_DIM:(g + 1) * HEAD_DIM])

    m, acc = {}, {}
    for g in range(GROUP):
        sink = sink_ref[hk * GROUP + g] * LOG2E
        for c in range(n_chunks):
            m[g, c] = jnp.full((1, u), sink, F32)
            acc[g, c] = jnp.concatenate([jnp.zeros((HEAD_DIM, u), F32), jnp.ones((ONES_ROWS, u), F32)], axis=0)

    ahead = [scores(unit) for unit in units[:WIN_AHEAD]]
    for idx, (g, c, kb) in enumerate(units):
        s = ahead.pop(0)
        if idx + WIN_AHEAD < len(units):
            ahead.append(scores(units[idx + WIN_AHEAD]))
        if kb < n_loc_blocks:
            s = jnp.where(mask(c, kb), s, NEG_BIG)
        m_next = jnp.maximum(m[g, c], jnp.max(s, axis=0, keepdims=True))
        alpha = jnp.exp2(m[g, c] - m_next)
        p = jnp.exp2(s - m_next).astype(BF16)
        acc[g, c] = alpha * acc[g, c] + jnp.dot(vext_ref[:, kb * u:(kb + 1) * u], p, preferred_element_type=F32)
        m[g, c] = m_next

    outs = []
    for g in range(GROUP):
        o_t = jnp.concatenate([acc[g, c][0:HEAD_DIM, :] / acc[g, c][HEAD_DIM:HEAD_DIM + 1, :]
                               for c in range(n_chunks)], axis=1)
        outs.append(o_t.T.astype(BF16))
    o_ref[...] = jnp.concatenate(outs, axis=1)


def _win_attn(q, k, vt, k_c, vt_c, sink, *, batch, seq, ctx, tq):
    nq = seq // tq
    r = tq // WINDOW
    nblk = seq // WINDOW

    def prev_blk(b, i):
        return b * nblk + jnp.maximum(i * r - 1, 0)

    def next_blk(b, i):
        return b * nblk + jnp.minimum((i + 1) * r, nblk - 1)

    return pl.pallas_call(
        functools.partial(_win_kernel, nq=nq),
        out_shape=jax.ShapeDtypeStruct((batch * seq, Q_W), BF16),
        grid=(batch, N_KV, nq),
        in_specs=[pl.BlockSpec(memory_space=pltpu.SMEM),
                  pl.BlockSpec((tq, GROUP * HEAD_DIM), lambda b, h, i: (b * nq + i, h)),
                  pl.BlockSpec((None, WINDOW, HEAD_DIM), lambda b, h, i: (h, prev_blk(b, i), 0)),
                  pl.BlockSpec((None, tq, HEAD_DIM), lambda b, h, i: (h, b * nq + i, 0)),
                  pl.BlockSpec((None, WINDOW, HEAD_DIM), lambda b, h, i: (h, next_blk(b, i), 0)),
                  pl.BlockSpec((None, ctx, HEAD_DIM), lambda b, h, i: (h, b, 0)),
                  pl.BlockSpec((HEAD_DIM, WINDOW), lambda b, h, i: (h, prev_blk(b, i))),
                  pl.BlockSpec((HEAD_DIM, tq), lambda b, h, i: (h, b * nq + i)),
                  pl.BlockSpec((HEAD_DIM, WINDOW), lambda b, h, i: (h, next_blk(b, i))),
                  pl.BlockSpec((HEAD_DIM, ctx), lambda b, h, i: (h, b))],
        out_specs=pl.BlockSpec((tq, GROUP * HEAD_DIM), lambda b, h, i: (b * nq + i, h)),
        scratch_shapes=[pltpu.VMEM((tq + 2 * WINDOW + ctx, HEAD_DIM), BF16),
                        pltpu.VMEM((HEAD_DIM + ONES_ROWS, tq + 2 * WINDOW + ctx), BF16)],
        compiler_params=_cparams(("parallel", "parallel", "parallel")),
        name="win_attn",
    )(sink, q, k, k, k, k_c, vt, vt, vt, vt_c)


ONES_ROWS = 16
Q_CHUNK = 256
K_SUB = 256
QK_AHEAD = 5


def _flash_kernel(sink_ref, q_ref, k_ref, vt_ref, *rest, n_main, has_extra, use_sink):
    if has_extra:
        kx_ref, vtx_ref, o_ref, q2_ref, m_ref, acc_ref, vext_ref, vextx_ref = rest
    else:
        o_ref, q2_ref, m_ref, acc_ref, vext_ref = rest
    hk = pl.program_id(1)
    j = pl.program_id(3)
    tq = q_ref.shape[0]

    @pl.when(j == 0)
    def _():
        for g in range(GROUP):
            q2_ref[g * tq:(g + 1) * tq, :] = q_ref[:, g * HEAD_DIM:(g + 1) * HEAD_DIM]
            if use_sink:
                m_ref[:, g * tq:(g + 1) * tq] = jnp.full((1, tq), sink_ref[hk * GROUP + g] * LOG2E, F32)
        if not use_sink:
            m_ref[...] = jnp.full(m_ref.shape, NEG_BIG, F32)
        acc_ref[0:HEAD_DIM, :] = jnp.zeros((HEAD_DIM, GROUP * tq), F32)
        acc_ref[HEAD_DIM:, :] = jnp.full((ONES_ROWS, GROUP * tq), 1.0 if use_sink else 0.0, F32)
        vext_ref[HEAD_DIM:, :] = jnp.ones((ONES_ROWS, vext_ref.shape[1]), BF16)
        if has_extra:
            vextx_ref[HEAD_DIM:, :] = jnp.ones((ONES_ROWS, vextx_ref.shape[1]), BF16)

    n_chunks = GROUP * tq // Q_CHUNK

    def step(kk_ref, vv_ref, vext):
        tkk = kk_ref.shape[0]
        kb = min(tkk, K_SUB)
        units = [(b, c) for b in range(tkk // kb) for c in range(n_chunks)]
        vext[0:HEAD_DIM, :] = vv_ref[...]

        def scores(u):
            b, c = u
            return _dot_nt(kk_ref[b * kb:(b + 1) * kb, :], q2_ref[c * Q_CHUNK:(c + 1) * Q_CHUNK, :])

        m = [m_ref[:, c * Q_CHUNK:(c + 1) * Q_CHUNK] for c in range(n_chunks)]
        acc = [acc_ref[:, c * Q_CHUNK:(c + 1) * Q_CHUNK] for c in range(n_chunks)]
        ahead = [scores(u) for u in units[:QK_AHEAD]]
        for idx, (b, c) in enumerate(units):
            s = ahead.pop(0)
            if idx + QK_AHEAD < len(units):
                ahead.append(scores(units[idx + QK_AHEAD]))
            m_next = jnp.maximum(m[c], jnp.max(s, axis=0, keepdims=True))
            alpha = jnp.exp2(m[c] - m_next)
            p = jnp.exp2(s - m_next).astype(BF16)
            acc[c] = alpha * acc[c] + jnp.dot(vext[:, b * kb:(b + 1) * kb], p, preferred_element_type=F32)
            m[c] = m_next
        acc_ref[...] = jnp.concatenate(acc, axis=1)
        m_ref[...] = jnp.concatenate(m, axis=1)

    @pl.when(j < n_main)
    def _():
        step(k_ref, vt_ref, vext_ref)

    if has_extra:
        @pl.when(j == n_main)
        def _():
            step(kx_ref, vtx_ref, vextx_ref)

    @pl.when(j == n_main + (1 if has_extra else 0) - 1)
    def _():
        o = (acc_ref[0:HEAD_DIM, :] / acc_ref[HEAD_DIM:HEAD_DIM + 1, :]).T
        for g in range(GROUP):
            o_ref[:, g * HEAD_DIM:(g + 1) * HEAD_DIM] = o[g * tq:(g + 1) * tq, :].astype(BF16)


def _flash(q_arr, k_arr, vt_arr, kx_arr, vtx_arr, sink, *, batch, seq_q, seq_k, seq_x, tq, tk, branch, use_sink):
    nq = seq_q // tq
    n_main = seq_k // tk
    has_extra = kx_arr is not None
    n_steps = n_main + (1 if has_extra else 0)
    qc = branch * N_KV
    kr = branch * N_KV

    def jj(j):
        return jnp.minimum(j, n_main - 1)

    in_specs = [pl.BlockSpec(memory_space=pltpu.SMEM),
                pl.BlockSpec((tq, GROUP * HEAD_DIM), lambda b, h, i, j: (b * nq + i, qc + h)),
                pl.BlockSpec((None, tk, HEAD_DIM), lambda b, h, i, j: (kr + h, b * n_main + jj(j), 0)),
                pl.BlockSpec((HEAD_DIM, tk), lambda b, h, i, j: (kr + h, b * n_main + jj(j)))]
    args = [sink, q_arr, k_arr, vt_arr]
    rows = GROUP * tq
    scratch = [pltpu.VMEM((rows, HEAD_DIM), BF16),
               pltpu.VMEM((1, rows), F32),
               pltpu.VMEM((HEAD_DIM + ONES_ROWS, rows), F32),
               pltpu.VMEM((HEAD_DIM + ONES_ROWS, tk), BF16)]
    if has_extra:
        in_specs += [pl.BlockSpec((None, seq_x, HEAD_DIM), lambda b, h, i, j: (kr + h, b, 0)),
                     pl.BlockSpec((HEAD_DIM, seq_x), lambda b, h, i, j: (kr + h, b))]
        args += [kx_arr, vtx_arr]
        scratch += [pltpu.VMEM((HEAD_DIM + ONES_ROWS, seq_x), BF16)]
    return pl.pallas_call(
        functools.partial(_flash_kernel, n_main=n_main, has_extra=has_extra, use_sink=use_sink),
        out_shape=jax.ShapeDtypeStruct((batch * seq_q, Q_W), BF16),
        grid=(batch, N_KV, nq, n_steps),
        in_specs=in_specs,
        out_specs=pl.BlockSpec((tq, GROUP * HEAD_DIM), lambda b, h, i, j: (b * nq + i, h)),
        scratch_shapes=scratch,
        compiler_params=_cparams(("parallel", "parallel", "parallel", "arbitrary")),
        name="flash",
    )(*args)


def _merge_kernel(h_ref, ya_ref, yb_ref, yc_ref, yd_ref, wga_ref, wgb_ref, wgc_ref, wgd_ref,
                  wb_ref, bg_ref, o_ref):
    h = h_ref[...]
    m = None
    for k, (y_ref, wg_ref) in enumerate(((ya_ref, wga_ref), (yb_ref, wgb_ref), (yc_ref, wgc_ref), (yd_ref, wgd_ref))):
        logits = jnp.dot(h, wg_ref[...], preferred_element_type=F32) + bg_ref[k:k + 1, :]
        t = _sigmoid(logits) * jnp.dot(y_ref[...], wb_ref[k], preferred_element_type=F32)
        m = t if m is None else m + t
    o_ref[...] = m.astype(BF16)


def _merge(h, ys, wg, wb, bg, *, layer, tm, tn):
    t = h.shape[0]
    nct = D_MODEL // tn

    def y():
        return pl.BlockSpec((tm, W_BR), lambda i, j: (i, 0))

    def g(k):
        return pl.BlockSpec((None, D_MODEL, tn), lambda i, j: (layer, 0, OFF_G // tn + k * nct + j))

    return pl.pallas_call(
        _merge_kernel,
        out_shape=jax.ShapeDtypeStruct((t, D_MODEL), BF16),
        grid=(t // tm, nct),
        in_specs=[pl.BlockSpec((tm, D_MODEL), lambda i, j: (i, 0)), y(), y(), y(), y(),
                  g(0), g(1), g(2), g(3),
                  pl.BlockSpec((None, N_BRANCH, W_BR, tn), lambda i, j: (layer, 0, 0, j)),
                  pl.BlockSpec((N_BRANCH, tn), lambda i, j: (0, j))],
        out_specs=pl.BlockSpec((tm, tn), lambda i, j: (i, j)),
        compiler_params=_cparams(("parallel", "arbitrary")),
        name="merge",
    )(h, *ys, wg, wg, wg, wg, wb, bg)


def _resid_kernel(a_ref, w_ref, x_ref, gate_ref, o_ref, *scratch, nk):
    part = jnp.dot(a_ref[...], w_ref[...], preferred_element_type=F32)
    if nk == 1:
        o_ref[...] = x_ref[...] + gate_ref[0] * part
        return
    (acc_ref,) = scratch
    k = pl.program_id(1)

    @pl.when(k == 0)
    def _():
        acc_ref[...] = part

    if nk > 2:
        @pl.when((k > 0) & (k < nk - 1))
        def _():
            acc_ref[...] += part

    @pl.when(k == nk - 1)
    def _():
        o_ref[...] = x_ref[...] + gate_ref[0] * (acc_ref[...] + part)


def _resid(a, w, x, gate, *, layer, seq, tm, tk):
    t, k = a.shape
    nbt = seq // tm
    nk = k // tk
    return pl.pallas_call(
        functools.partial(_resid_kernel, nk=nk),
        out_shape=jax.ShapeDtypeStruct((t, D_MODEL), F32),
        grid=(t // tm, nk),
        in_specs=[pl.BlockSpec((tm, tk), lambda i, j: (i, j)),
                  pl.BlockSpec((None, tk, D_MODEL), lambda i, j: (layer, j, 0)),
                  pl.BlockSpec((tm, D_MODEL), lambda i, j: (i, 0)),
                  pl.BlockSpec((1, 1, D_MODEL), lambda i, j: (i // nbt, 0, 0))],
        out_specs=pl.BlockSpec((tm, D_MODEL), lambda i, j: (i, 0)),
        scratch_shapes=[pltpu.VMEM((tm, D_MODEL), F32)] if nk > 1 else [],
        compiler_params=_cparams(("parallel", "arbitrary")),
        name="resid",
    )(a, w, x, gate)


def _resid_norm_kernel(a_ref, w_ref, x_ref, gate_ref, gn_ref, o_ref, acc_ref, *, nk):
    k = pl.program_id(1)
    part = jnp.dot(a_ref[...], w_ref[...], preferred_element_type=F32)

    @pl.when(k == 0)
    def _():
        acc_ref[...] = part

    @pl.when(k > 0)
    def _():
        acc_ref[...] += part

    @pl.when(k == nk - 1)
    def _():
        gate = gate_ref[0]
        gn = gn_ref[...]

        def body(i, carry):
            rows = pl.ds(pl.multiple_of(i * NORM_CHUNK, NORM_CHUNK), NORM_CHUNK)
            y = x_ref[rows, :] + gate * acc_ref[rows, :]
            o_ref[rows, :] = y * lax.rsqrt(jnp.mean(y * y, axis=-1, keepdims=True) + EPS) * gn
            return carry

        lax.fori_loop(0, o_ref.shape[0] // NORM_CHUNK, body, 0, unroll=NORM_UNROLL)


def _resid_norm(a, w, x, gate, gn, *, layer, seq, tm, tk):
    t, k = a.shape
    nbt = seq // tm
    nk = k // tk
    return pl.pallas_call(
        functools.partial(_resid_norm_kernel, nk=nk),
        out_shape=jax.ShapeDtypeStruct((t, D_MODEL), F32),
        grid=(t // tm, nk),
        in_specs=[pl.BlockSpec((tm, tk), lambda i, j: (i, j)),
                  pl.BlockSpec((None, tk, D_MODEL), lambda i, j: (layer, j, 0)),
                  pl.BlockSpec((tm, D_MODEL), lambda i, j: (i, 0)),
                  pl.BlockSpec((1, 1, D_MODEL), lambda i, j: (i // nbt, 0, 0)),
                  pl.BlockSpec((1, D_MODEL), lambda i, j: (0, 0))],
        out_specs=pl.BlockSpec((tm, D_MODEL), lambda i, j: (i, 0)),
        scratch_shapes=[pltpu.VMEM((tm, D_MODEL), F32)],
        compiler_params=_cparams(("parallel", "arbitrary")),
        name="resid_norm",
    )(a, w, x, gate, gn)


def _ffn_in_kernel(x_ref, g_ref, sc_ref, sh_ref, wa_ref, wb_ref, o_ref, h_ref):
    @pl.when(pl.program_id(1) == 0)
    def _():
        _norm_mod_rows(x_ref, g_ref, sc_ref, sh_ref, h_ref)

    h = h_ref[...]
    a = jnp.dot(h, wa_ref[...], preferred_element_type=F32)
    b = jnp.dot(h, wb_ref[...], preferred_element_type=F32)
    o_ref[...] = (a * _sigmoid(a) * b).astype(BF16)


def _ffn_in(x, g, sc, sh, w, *, layer, seq, tm, tn):
    t = x.shape[0]
    nbt = seq // tm
    nct = FFN_HIDDEN // tn
    return pl.pallas_call(
        _ffn_in_kernel,
        out_shape=jax.ShapeDtypeStruct((t, FFN_HIDDEN), BF16),
        grid=(t // tm, nct),
        in_specs=[pl.BlockSpec((tm, D_MODEL), lambda i, j: (i, 0)),
                  pl.BlockSpec((1, D_MODEL), lambda i, j: (0, 0)),
                  pl.BlockSpec((1, 1, D_MODEL), lambda i, j: (i // nbt, 0, 0)),
                  pl.BlockSpec((1, 1, D_MODEL), lambda i, j: (i // nbt, 0, 0)),
                  pl.BlockSpec((None, D_MODEL, tn), lambda i, j: (layer, 0, j)),
                  pl.BlockSpec((None, D_MODEL, tn), lambda i, j: (layer, 0, nct + j))],
        out_specs=pl.BlockSpec((tm, tn), lambda i, j: (i, j)),
        scratch_shapes=[pltpu.VMEM((tm, D_MODEL), BF16)],
        compiler_params=_cparams(("parallel", "arbitrary")),
        name="ffn_in",
    )(x, g, sc, sh, w, w)


def _rope_tables(seq):
    n_rows = seq // GRID_W
    inv = ROPE_THETA ** (-jnp.arange(0, ROT_AXIS, 2, dtype=F32) / ROT_AXIS)
    ar = jnp.arange(n_rows, dtype=F32)[:, None] * inv
    ac = jnp.arange(GRID_W, dtype=F32)[:, None] * inv
    shape = (n_rows, GRID_W, ROT_AXIS // 2)

    def grid(fr, fc):
        r = jnp.broadcast_to(fr[:, None, :], shape)
        c = jnp.broadcast_to(fc[None, :, :], shape)
        return r, c

    cr, cc = grid(jnp.cos(ar), jnp.cos(ac))
    sr, sc = grid(jnp.sin(ar), jnp.sin(ac))
    cos = jnp.concatenate([cr, cc, cr, cc], axis=-1).reshape(seq, HEAD_DIM)
    sin_signed = jnp.concatenate([-sr, -sc, sr, sc], axis=-1).reshape(seq, HEAD_DIM)
    return cos, sin_signed


def _pair_split(v):
    quarter = HEAD_DIM // 4
    q = v.reshape(v.shape[:-1] + (v.shape[-1] // HEAD_DIM, 4, quarter))
    q = jnp.concatenate([q[..., 0:1, :], q[..., 2:3, :], q[..., 1:2, :], q[..., 3:4, :]], axis=-2)
    return q.reshape(v.shape)


CAST_ROWS = 128
ROTARY_SPANS = ((OFF_BQ, OFF_BV), (OFF_DQ, OFF_DV))


def _cast_w_in_kernel(w_ref, o_ref):
    quarter = HEAD_DIM // 4
    lane = lax.broadcasted_iota(jnp.int32, (w_ref.shape[0], HEAD_DIM), 1)
    second = (lane >= quarter) & (lane < 2 * quarter)
    third = (lane >= 2 * quarter) & (lane < 3 * quarter)
    edges = sorted({0, w_ref.shape[1]} | {e for span in ROTARY_SPANS for e in span})
    for lo, hi in zip(edges[:-1], edges[1:]):
        if (lo, hi) in ROTARY_SPANS:
            for c in range(lo, hi, HEAD_DIM):
                xh = w_ref[:, c:c + HEAD_DIM]
                xh = jnp.where(second, pltpu.roll(xh, HEAD_DIM - quarter, 1),
                               jnp.where(third, pltpu.roll(xh, quarter, 1), xh))
                o_ref[:, c:c + HEAD_DIM] = xh.astype(BF16)
        else:
            o_ref[:, lo:hi] = w_ref[:, lo:hi].astype(BF16)


def _cast_w_in(w):
    depth, d, p = w.shape
    return pl.pallas_call(
        _cast_w_in_kernel,
        out_shape=jax.ShapeDtypeStruct(w.shape, BF16),
        grid=(depth, d // CAST_ROWS),
        in_specs=[pl.BlockSpec((None, CAST_ROWS, p), lambda l, i: (l, i, 0))],
        out_specs=pl.BlockSpec((None, CAST_ROWS, p), lambda l, i: (l, i, 0)),
        compiler_params=_cparams(("parallel", "parallel")),
        name="cast_w_in",
    )(w)


def _tile(n, pref):
    return pref if n % pref == 0 else n


class _Tiles(NamedTuple):
    proj: int
    rows: int
    conv_a: int
    conv_c: int
    win_q: int
    flash_q: int
    flash_k: int


def _tiles(seq):
    return _Tiles(proj=_tile(seq, 1024), rows=_tile(seq, 512), conv_a=_tile(seq, 512), conv_c=_tile(seq, 512),
                  win_q=_tile(seq, 512), flash_q=_tile(seq, 1024), flash_k=_tile(seq, 8192))


def _forward(x, c, ctx, c_ctx, w_ada, b_ada, norm_mix, norm_ffn, w_in, b_gate, conv_a_w, sink_b,
             qk_norm_q, qk_norm_k, conv_c_w, conv_c_b, ln_c_g, ln_c_b, w_branch, w_out,
             w_ffn_in, w_ffn_out, norm_final):
    batch, seq, _ = x.shape
    n_ctx = ctx.shape[1]
    depth = w_in.shape[0]
    xs = x.reshape(batch * seq, D_MODEL)
    xc = ctx.reshape(batch * n_ctx, D_MODEL)

    tiles = _tiles(seq)
    tm = tiles.proj

    rope_x = _rope_tables(seq)
    rope_c = (jnp.ones((n_ctx, HEAD_DIM), F32), jnp.zeros((n_ctx, HEAD_DIM), F32))
    cvec = jnp.zeros((ADA_ROWS, D_MODEL), F32).at[:batch].set(c).at[batch].set(c_ctx)

    def row(v):
        return v.reshape(1, -1)

    w_proj = w_gate = _cast_w_in(w_in)
    w_br = w_branch.astype(BF16)
    w_o = w_out.astype(BF16)
    w_f1 = w_ffn_in.astype(BF16)
    w_f2 = w_ffn_out.astype(BF16)

    def mixers(pf, q, k, vt, k_c, vt_c, l, *, s, tma, tmc, is_ctx):
        y_a = _conv_a(pf, conv_a_w[l], seq=s, tm=tma)
        y_c = _conv_c(pf, conv_c_w[l], row(conv_c_b[l]), row(ln_c_g[l]), row(ln_c_b[l]), seq=s, tm=tmc)
        if is_ctx:
            y_b = _flash(q, k, vt, None, None, sink_b[l], batch=batch, seq_q=s, seq_k=s, seq_x=0, tq=s, tk=s,
                         branch=0, use_sink=True)
            y_d = _flash(q, k, vt, None, None, sink_b[l], batch=batch, seq_q=s, seq_k=s, seq_x=0, tq=s, tk=s,
                         branch=1, use_sink=False)
        else:
            y_b = _win_attn(q, k, vt, k_c, vt_c, sink_b[l], batch=batch, seq=s, ctx=n_ctx, tq=tiles.win_q)
            y_d = _flash(q, k, vt, k_c, vt_c, sink_b[l], batch=batch, seq_q=s, seq_k=s, seq_x=n_ctx,
                         tq=tiles.flash_q, tk=tiles.flash_k, branch=1, use_sink=False)
        return [y_a, y_b, y_c, y_d]

    for l in range(depth):
        last = l == depth - 1
        mod = _ada(cvec, w_ada, row(b_ada[l]), layer=l)
        mx = mod[:batch].reshape(batch, 1, 6, D_MODEL)
        mcx = jnp.broadcast_to(mod[batch].reshape(1, 1, 6, D_MODEL), (batch, 1, 6, D_MODEL))
        sh_m, sc_m, g_m, sh_f, sc_f, g_f = (mx[:, :, k] for k in range(6))
        csh_m, csc_m, cg_m, csh_f, csc_f, cg_f = (mcx[:, :, k] for k in range(6))
        gq, gk = row(_pair_split(qk_norm_q[l])), row(_pair_split(qk_norm_k[l]))

        pf_c, q_c, h_c, k_c, vt_c = _inproj(xc, row(norm_mix[l]), csc_m, csh_m, w_proj, *rope_c, gq, gk,
                                            layer=l, seq=n_ctx, tm=n_ctx)
        pf, q, h, k, vt = _inproj(xs, row(norm_mix[l]), sc_m, sh_m, w_proj, *rope_x, gq, gk,
                                  layer=l, seq=seq, tm=tm)

        ys = mixers(pf, q, k, vt, k_c, vt_c, l, s=seq, tma=tiles.conv_a, tmc=tiles.conv_c, is_ctx=False)
        m = _merge(h, ys, w_gate, w_br, b_gate[l], layer=l, tm=tm, tn=512)
        xs = _resid(m, w_o, xs, g_m, layer=l, seq=seq, tm=tiles.rows, tk=D_MODEL)

        if not last:
            ys_c = mixers(pf_c, q_c, k_c, vt_c, None, None, l, s=n_ctx, tma=n_ctx, tmc=n_ctx, is_ctx=True)
            t_c = batch * n_ctx
            m_c = _merge(h_c, ys_c, w_gate, w_br, b_gate[l], layer=l, tm=t_c, tn=512)
            xc = _resid(m_c, w_o, xc, cg_m, layer=l, seq=t_c, tm=t_c, tk=D_MODEL)
            hid_c = _ffn_in(xc, row(norm_ffn[l]), csc_f, csh_f, w_f1, layer=l, seq=t_c, tm=t_c, tn=512)
            xc = _resid(hid_c, w_f2, xc, cg_f, layer=l, seq=t_c, tm=t_c, tk=FFN_HIDDEN // 2)

        hid = _ffn_in(xs, row(norm_ffn[l]), sc_f, sh_f, w_f1, layer=l, seq=seq, tm=tm, tn=512)
        if last:
            xs = _resid_norm(hid, w_f2, xs, g_f, row(norm_final), layer=l, seq=seq, tm=tiles.rows,
                             tk=FFN_HIDDEN // 2)
        else:
            xs = _resid(hid, w_f2, xs, g_f, layer=l, seq=seq, tm=tiles.rows, tk=FFN_HIDDEN // 2)

    return xs.reshape(batch, seq, D_MODEL)


def kernel(x, c, ctx, c_ctx, w_ada, b_ada, norm_mix, norm_ffn, w_in, b_gate, conv_a_w, sink_b, qk_norm_q,
           qk_norm_k, conv_c_w, conv_c_b, ln_c_g, ln_c_b, w_branch, w_out, w_ffn_in, w_ffn_out, norm_final):
    return _forward(x, c, ctx, c_ctx, w_ada, b_ada, norm_mix, norm_ffn, w_in, b_gate, conv_a_w, sink_b,
                    qk_norm_q, qk_norm_k, conv_c_w, conv_c_b, ln_c_g, ln_c_b, w_branch, w_out,
                    w_ffn_in, w_ffn_out, norm_final)
```

```python
import functools
from typing import NamedTuple

import jax
import jax.numpy as jnp
from jax import lax
from jax.experimental import pallas as pl
from jax.experimental.pallas import tpu as pltpu

F32 = jnp.float32
BF16 = jnp.bfloat16

D_MODEL = 2048
N_BRANCH = 4
W_BR = D_MODEL // 4
HEAD_DIM = 128
N_Q = W_BR // HEAD_DIM
N_KV = N_Q // 2
GROUP = N_Q // N_KV
Q_W = N_Q * HEAD_DIM
KV_W = N_KV * HEAD_DIM
CONV_A = 3
CONV_C = 31
WINDOW = 128
GRID_W = 64
ROT_AXIS = HEAD_DIM // 2
ROPE_THETA = 10000.0
FFN_HIDDEN = 5632
EPS = 1e-6
ATTN_SCALE = HEAD_DIM ** -0.5
LOG2E = 1.4426950408889634
Q_SCALE = ATTN_SCALE * LOG2E
NEG_BIG = -1e30

OFF_A = 0
OFF_BQ = OFF_A + 3 * W_BR
OFF_BK = OFF_BQ + Q_W
OFF_BV = OFF_BK + KV_W
OFF_C = OFF_BV + KV_W
OFF_DQ = OFF_C + 2 * W_BR
OFF_DK = OFF_DQ + Q_W
OFF_DV = OFF_DK + KV_W
OFF_G = OFF_DV + KV_W

PF_W = 3 * W_BR + 2 * W_BR
PQ_W = 2 * (Q_W + 2 * KV_W)
TN = 512
N_PF_TILES = PF_W // TN
N_PROJ_TILES = (PF_W + PQ_W) // TN

VMEM_LIMIT = 56 * 1024 * 1024


def _cparams(sem):
    return pltpu.CompilerParams(dimension_semantics=sem, vmem_limit_bytes=VMEM_LIMIT)


def _sigmoid(z):
    return 1.0 / (1.0 + jnp.exp(-z))


def _ada_kernel(c_ref, w_ref, b_ref, o_ref):
    c = c_ref[...]
    s = (c * _sigmoid(c)).astype(BF16)
    o_ref[...] = jnp.dot(s, w_ref[...].astype(BF16), preferred_element_type=F32) + b_ref[...]


ADA_ROWS = 16


def _ada(cvec, w, b, *, layer):
    n = w.shape[2]
    tn = 1024
    return pl.pallas_call(
        _ada_kernel,
        out_shape=jax.ShapeDtypeStruct((ADA_ROWS, n), F32),
        grid=(n // tn,),
        in_specs=[pl.BlockSpec((ADA_ROWS, D_MODEL), lambda j: (0, 0)),
                  pl.BlockSpec((None, D_MODEL, tn), lambda j: (layer, 0, j)),
                  pl.BlockSpec((1, tn), lambda j: (0, j))],
        out_specs=pl.BlockSpec((ADA_ROWS, tn), lambda j: (0, j)),
        compiler_params=_cparams(("arbitrary",)),
        name="ada",
    )(cvec, w, b)


def _rope(xh, cos, sin_signed):
    return xh * cos + pltpu.roll(xh, HEAD_DIM // 2, 1) * sin_signed


def _head_norm(xh, g):
    return xh * lax.rsqrt(jnp.mean(xh * xh, axis=-1, keepdims=True) + EPS) * g


NORM_CHUNK = 16
NORM_UNROLL = 8


def _norm_mod_rows(x_ref, g_ref, sc_ref, sh_ref, h_ref):
    gs = g_ref[...] * (1.0 + sc_ref[0])
    sh = sh_ref[0]

    def body(i, carry):
        rows = pl.ds(pl.multiple_of(i * NORM_CHUNK, NORM_CHUNK), NORM_CHUNK)
        x = x_ref[rows, :]
        r = lax.rsqrt(jnp.mean(x * x, axis=-1, keepdims=True) + EPS)
        h_ref[rows, :] = (x * r * gs + sh).astype(BF16)
        return carry

    lax.fori_loop(0, x_ref.shape[0] // NORM_CHUNK, body, 0, unroll=NORM_UNROLL)


def _inproj_kernel(x_ref, g_ref, sc_ref, sh_ref, w_ref, cos_ref, sin_ref, gq_ref, gk_ref,
                   pf_ref, q_ref, h_ref, k_ref, vt_ref):
    j = pl.program_id(1)

    @pl.when(j == 0)
    def _():
        _norm_mod_rows(x_ref, g_ref, sc_ref, sh_ref, h_ref)

    acc = jnp.dot(h_ref[...], w_ref[...], preferred_element_type=F32)

    @pl.when(j < N_PF_TILES)
    def _():
        pf_ref[...] = acc.astype(BF16)

    def rope(xh):
        return _rope(xh, cos_ref[...], sin_ref[...])

    def head(k):
        return acc[:, k * HEAD_DIM:(k + 1) * HEAD_DIM]

    def put_q(k, v):
        q_ref[:, k * HEAD_DIM:(k + 1) * HEAD_DIM] = v.astype(BF16)

    @pl.when(j == N_PF_TILES)
    def _():
        for k in range(N_Q):
            put_q(k, rope(head(k)) * Q_SCALE)

    @pl.when(j == N_PF_TILES + 1)
    def _():
        for k in range(N_KV):
            k_ref[k] = rope(head(k)).astype(BF16)
        vt_ref[0:KV_W, :] = acc[:, KV_W:].T.astype(BF16)

    @pl.when(j == N_PF_TILES + 2)
    def _():
        for k in range(N_Q):
            put_q(k, rope(_head_norm(head(k), gq_ref[...])) * Q_SCALE)

    @pl.when(j == N_PF_TILES + 3)
    def _():
        for k in range(N_KV):
            k_ref[N_KV + k] = rope(_head_norm(head(k), gk_ref[...])).astype(BF16)
        vt_ref[KV_W:, :] = acc[:, KV_W:].T.astype(BF16)


def _proj_block(j):
    n_a, n_b, n_c = 3 * W_BR // TN, (Q_W + 2 * KV_W) // TN, 2 * W_BR // TN
    return jnp.where(j < n_a, j, jnp.where(j < n_a + n_c, j + n_b, jnp.where(j < n_a + n_c + n_b, j - n_c, j)))


def _inproj(x, g, sc, sh, w, cos, sin_signed, gq, gk, *, layer, seq, tm):
    t = x.shape[0]
    nbt = seq // tm
    return pl.pallas_call(
        _inproj_kernel,
        out_shape=(jax.ShapeDtypeStruct((t, PF_W), BF16),
                   jax.ShapeDtypeStruct((t, 2 * Q_W), BF16),
                   jax.ShapeDtypeStruct((t, D_MODEL), BF16),
                   jax.ShapeDtypeStruct((2 * N_KV, t, HEAD_DIM), BF16),
                   jax.ShapeDtypeStruct((2 * KV_W, t), BF16)),
        grid=(t // tm, N_PROJ_TILES),
        in_specs=[pl.BlockSpec((tm, D_MODEL), lambda i, j: (i, 0)),
                  pl.BlockSpec((1, D_MODEL), lambda i, j: (0, 0)),
                  pl.BlockSpec((1, 1, D_MODEL), lambda i, j: (i // nbt, 0, 0)),
                  pl.BlockSpec((1, 1, D_MODEL), lambda i, j: (i // nbt, 0, 0)),
                  pl.BlockSpec((None, D_MODEL, TN), lambda i, j: (layer, 0, _proj_block(j))),
                  pl.BlockSpec((tm, HEAD_DIM), lambda i, j: (i % nbt, 0)),
                  pl.BlockSpec((tm, HEAD_DIM), lambda i, j: (i % nbt, 0)),
                  pl.BlockSpec((1, HEAD_DIM), lambda i, j: (0, 0)),
                  pl.BlockSpec((1, HEAD_DIM), lambda i, j: (0, 0))],
        out_specs=(pl.BlockSpec((tm, TN), lambda i, j: (i, jnp.minimum(j, N_PF_TILES - 1))),
                   pl.BlockSpec((tm, Q_W), lambda i, j: (i, jnp.where(j >= N_PF_TILES + 2, 1, 0))),
                   pl.BlockSpec((tm, D_MODEL), lambda i, j: (i, 0)),
                   pl.BlockSpec((2 * N_KV, tm, HEAD_DIM), lambda i, j: (0, i, 0)),
                   pl.BlockSpec((2 * KV_W, tm), lambda i, j: (0, i))),
        compiler_params=_cparams(("parallel", "arbitrary")),
        name="inproj",
    )(x, g, sc, sh, w, cos, sin_signed, gq, gk)


HALO_A = 16


def _conv_a_kernel(bg_ref, cg_ref, h_ref, cgp_ref, hp_ref, cgn_ref, hn_ref, w_ref, o_ref, u_ref, *, nbt):
    it = pl.program_id(0) % nbt
    tm = bg_ref.shape[0]
    def gated(gate_ref, val_ref):
        return gate_ref[...].astype(F32) * val_ref[...].astype(F32)

    u_ref[HALO_A:HALO_A + tm, :] = gated(cg_ref, h_ref)
    u_ref[0:HALO_A, :] = jnp.where(it > 0, gated(cgp_ref, hp_ref), 0.0)
    u_ref[HALO_A + tm:, :] = jnp.where(it < nbt - 1, gated(cgn_ref, hn_ref), 0.0)
    y = (w_ref[0:1, :] * u_ref[HALO_A - 1:HALO_A - 1 + tm, :]
         + w_ref[1:2, :] * u_ref[HALO_A:HALO_A + tm, :]
         + w_ref[2:3, :] * u_ref[HALO_A + 1:HALO_A + 1 + tm, :])
    o_ref[...] = (bg_ref[...].astype(F32) * y).astype(BF16)


def _conv_a(pf, w, *, seq, tm):
    t = pf.shape[0]
    nbt = seq // tm
    r = tm // HALO_A
    last = t // HALO_A - 1

    def cur(c):
        return pl.BlockSpec((tm, W_BR), lambda i: (i, c))

    def prev(c):
        return pl.BlockSpec((HALO_A, W_BR), lambda i: (jnp.maximum(i * r - 1, 0), c))

    def nxt(c):
        return pl.BlockSpec((HALO_A, W_BR), lambda i: (jnp.minimum((i + 1) * r, last), c))

    return pl.pallas_call(
        functools.partial(_conv_a_kernel, nbt=nbt),
        out_shape=jax.ShapeDtypeStruct((t, W_BR), BF16),
        grid=(t // tm,),
        in_specs=[cur(0), cur(1), cur(2), prev(1), prev(2), nxt(1), nxt(2),
                  pl.BlockSpec((CONV_A, W_BR), lambda i: (0, 0))],
        out_specs=pl.BlockSpec((tm, W_BR), lambda i: (i, 0)),
        scratch_shapes=[pltpu.VMEM((tm + 2 * HALO_A, W_BR), F32)],
        compiler_params=_cparams(("parallel",)),
        name="conv_a",
    )(pf, pf, pf, pf, pf, pf, pf, w)


HALO_C = 16
ROWS_C = 64
SUBLANES = 8


def _conv_c_kernel(v_ref, g_ref, vp_ref, gp_ref, vn_ref, gn_ref, w_ref, b_ref, lg_ref, lb_ref,
                   o_ref, u_ref, us_ref, *, nbt):
    it = pl.program_id(0) % nbt
    tm = v_ref.shape[0]

    def glu(val_ref, gate_ref):
        return val_ref[...].astype(F32) * _sigmoid(gate_ref[...].astype(F32))

    u_ref[HALO_C:HALO_C + tm, :] = glu(v_ref, g_ref)
    u_ref[0:HALO_C, :] = jnp.where(it > 0, glu(vp_ref, gp_ref), 0.0)
    u_ref[HALO_C + tm:, :] = jnp.where(it < nbt - 1, glu(vn_ref, gn_ref), 0.0)
    n_sh = us_ref.shape[1]
    for r in range(1, SUBLANES):
        us_ref[r - 1] = u_ref[r:r + n_sh, :]
    pad = CONV_C // 2

    def tap(row):
        a, r = divmod(row, SUBLANES)
        if r == 0:
            return u_ref[a * SUBLANES:a * SUBLANES + ROWS_C, :]
        return us_ref[r - 1, a * SUBLANES:a * SUBLANES + ROWS_C, :]

    for c in range(tm // ROWS_C):
        base = HALO_C + c * ROWS_C - pad
        acc = w_ref[0:1, :] * tap(base)
        for k in range(1, CONV_C):
            acc = acc + w_ref[k:k + 1, :] * tap(base + k)
        acc = acc + b_ref[...]
        mu = jnp.mean(acc, axis=-1, keepdims=True)
        xc = acc - mu
        var = jnp.mean(xc * xc, axis=-1, keepdims=True)
        y = xc * lax.rsqrt(var + EPS) * lg_ref[...] + lb_ref[...]
        o_ref[c * ROWS_C:(c + 1) * ROWS_C, :] = (y * _sigmoid(y)).astype(BF16)


def _conv_c(pf, w, b, lg, lb, *, seq, tm):
    t = pf.shape[0]
    nbt = seq // tm
    r = tm // HALO_C
    last = t // HALO_C - 1
    c0 = 3

    def cur(c):
        return pl.BlockSpec((tm, W_BR), lambda i: (i, c))

    def prev(c):
        return pl.BlockSpec((HALO_C, W_BR), lambda i: (jnp.maximum(i * r - 1, 0), c))

    def nxt(c):
        return pl.BlockSpec((HALO_C, W_BR), lambda i: (jnp.minimum((i + 1) * r, last), c))

    def vec():
        return pl.BlockSpec((1, W_BR), lambda i: (0, 0))

    return pl.pallas_call(
        functools.partial(_conv_c_kernel, nbt=nbt),
        out_shape=jax.ShapeDtypeStruct((t, W_BR), BF16),
        grid=(t // tm,),
        in_specs=[cur(c0), cur(c0 + 1), prev(c0), prev(c0 + 1), nxt(c0), nxt(c0 + 1),
                  pl.BlockSpec((CONV_C, W_BR), lambda i: (0, 0)), vec(), vec(), vec()],
        out_specs=pl.BlockSpec((tm, W_BR), lambda i: (i, 0)),
        scratch_shapes=[pltpu.VMEM((tm + 2 * HALO_C, W_BR), F32),
                        pltpu.VMEM((SUBLANES - 1, tm + 2 * HALO_C - SUBLANES, W_BR), F32)],
        compiler_params=_cparams(("parallel",)),
        name="conv_c",
    )(pf, pf, pf, pf, pf, pf, w, b, lg, lb)


def _dot_nt(a, b):
    return lax.dot_general(a, b, (((1,), (1,)), ((), ())), preferred_element_type=F32)


WIN_UNIT = 2 * WINDOW
WIN_AHEAD = 5


def _win_kernel(sink_ref, q_ref, kp_ref, kc_ref, kn_ref, kx_ref, vp_ref, vc_ref, vn_ref, vx_ref, o_ref,
                kk_ref, vext_ref, *, nq):
    hk = pl.program_id(1)
    i = pl.program_id(2)
    tq = q_ref.shape[0]
    n_loc = tq + 2 * WINDOW
    n_all = kk_ref.shape[0]
    u = WIN_UNIT
    kk_ref[0:WINDOW, :] = kp_ref[...]
    kk_ref[WINDOW:WINDOW + tq, :] = kc_ref[...]
    kk_ref[WINDOW + tq:n_loc, :] = kn_ref[...]
    kk_ref[n_loc:, :] = kx_ref[...]
    vext_ref[0:HEAD_DIM, 0:WINDOW] = vp_ref[...]
    vext_ref[0:HEAD_DIM, WINDOW:WINDOW + tq] = vc_ref[...]
    vext_ref[0:HEAD_DIM, WINDOW + tq:n_loc] = vn_ref[...]
    vext_ref[0:HEAD_DIM, n_loc:] = vx_ref[...]
    vext_ref[HEAD_DIM:, :] = jnp.ones((ONES_ROWS, n_all), BF16)

    key = lax.broadcasted_iota(jnp.int32, (u, u), 0)
    qry = lax.broadcasted_iota(jnp.int32, (u, u), 1)
    n_loc_blocks = n_loc // u
    n_chunks = tq // u

    def mask(c, kb):
        ok = jnp.abs((kb - c) * u - WINDOW + key - qry) <= WINDOW
        if kb == 0:
            ok = ok & (key >= jnp.where(i > 0, 0, WINDOW))
        if kb == n_loc_blocks - 1:
            ok = ok & (key < jnp.where(i < nq - 1, u, u - WINDOW))
        return ok

    units = []
    for g in range(GROUP):
        for c in range(n_chunks):
            units += [(g, c, kb) for kb in (c, c + 1)] + [(g, c, kb) for kb in range(n_loc_blocks, n_all // u)]

    def scores(unit):
        g, c, kb = unit
        return _dot_nt(kk_ref[kb * u:(kb + 1) * u, :], q_ref[c * u:(c + 1) * u, g * HEAD_DIM:(g + 1) * HEAD_DIM])

    m, acc = {}, {}
    for g in range(GROUP):
        sink = sink_ref[hk * GROUP + g] * LOG2E
        for c in range(n_chunks):
            m[g, c] = jnp.full((1, u), sink, F32)
            acc[g, c] = jnp.concatenate([jnp.zeros((HEAD_DIM, u), F32), jnp.ones((ONES_ROWS, u), F32)], axis=0)

    ahead = [scores(unit) for unit in units[:WIN_AHEAD]]
    for idx, (g, c, kb) in enumerate(units):
        s = ahead.pop(0)
        if idx + WIN_AHEAD < len(units):
            ahead.append(scores(units[idx + WIN_AHEAD]))
        if kb < n_loc_blocks:
            s = jnp.where(mask(c, kb), s, NEG_BIG)
        m_next = jnp.maximum(m[g, c], jnp.max(s, axis=0, keepdims=True))
        alpha = jnp.exp2(m[g, c] - m_next)
        p = jnp.exp2(s - m_next).astype(BF16)
        acc[g, c] = alpha * acc[g, c] + jnp.dot(vext_ref[:, kb * u:(kb + 1) * u], p, preferred_element_type=F32)
        m[g, c] = m_next

    outs = []
    for g in range(GROUP):
        o_t = jnp.concatenate([acc[g, c][0:HEAD_DIM, :] / acc[g, c][HEAD_DIM:HEAD_DIM + 1, :]
                               for c in range(n_chunks)], axis=1)
        outs.append(o_t.T.astype(BF16))
    o_ref[...] = jnp.concatenate(outs, axis=1)


def _win_attn(q, k, vt, k_c, vt_c, sink, *, batch, seq, ctx, tq):
    nq = seq // tq
    r = tq // WINDOW
    nblk = seq // WINDOW

    def prev_blk(b, i):
        return b * nblk + jnp.maximum(i * r - 1, 0)

    def next_blk(b, i):
        return b * nblk + jnp.minimum((i + 1) * r, nblk - 1)

    return pl.pallas_call(
        functools.partial(_win_kernel, nq=nq),
        out_shape=jax.ShapeDtypeStruct((batch * seq, Q_W), BF16),
        grid=(batch, N_KV, nq),
        in_specs=[pl.BlockSpec(memory_space=pltpu.SMEM),
                  pl.BlockSpec((tq, GROUP * HEAD_DIM), lambda b, h, i: (b * nq + i, h)),
                  pl.BlockSpec((None, WINDOW, HEAD_DIM), lambda b, h, i: (h, prev_blk(b, i), 0)),
                  pl.BlockSpec((None, tq, HEAD_DIM), lambda b, h, i: (h, b * nq + i, 0)),
                  pl.BlockSpec((None, WINDOW, HEAD_DIM), lambda b, h, i: (h, next_blk(b, i), 0)),
                  pl.BlockSpec((None, ctx, HEAD_DIM), lambda b, h, i: (h, b, 0)),
                  pl.BlockSpec((HEAD_DIM, WINDOW), lambda b, h, i: (h, prev_blk(b, i))),
                  pl.BlockSpec((HEAD_DIM, tq), lambda b, h, i: (h, b * nq + i)),
                  pl.BlockSpec((HEAD_DIM, WINDOW), lambda b, h, i: (h, next_blk(b, i))),
                  pl.BlockSpec((HEAD_DIM, ctx), lambda b, h, i: (h, b))],
        out_specs=pl.BlockSpec((tq, GROUP * HEAD_DIM), lambda b, h, i: (b * nq + i, h)),
        scratch_shapes=[pltpu.VMEM((tq + 2 * WINDOW + ctx, HEAD_DIM), BF16),
                        pltpu.VMEM((HEAD_DIM + ONES_ROWS, tq + 2 * WINDOW + ctx), BF16)],
        compiler_params=_cparams(("parallel", "parallel", "parallel")),
        name="win_attn",
    )(sink, q, k, k, k, k_c, vt, vt, vt, vt_c)


ONES_ROWS = 16
Q_CHUNK = 256
K_SUB = 256
QK_AHEAD = 5


def _flash_kernel(sink_ref, q_ref, k_ref, vt_ref, *rest, n_main, has_extra, use_sink):
    if has_extra:
        kx_ref, vtx_ref, o_ref, q2_ref, m_ref, acc_ref, vext_ref, vextx_ref = rest
    else:
        o_ref, q2_ref, m_ref, acc_ref, vext_ref = rest
    hk = pl.program_id(1)
    j = pl.program_id(3)
    tq = q_ref.shape[0]

    @pl.when(j == 0)
    def _():
        for g in range(GROUP):
            q2_ref[g * tq:(g + 1) * tq, :] = q_ref[:, g * HEAD_DIM:(g + 1) * HEAD_DIM]
            if use_sink:
                m_ref[:, g * tq:(g + 1) * tq] = jnp.full((1, tq), sink_ref[hk * GROUP + g] * LOG2E, F32)
        if not use_sink:
            m_ref[...] = jnp.full(m_ref.shape, NEG_BIG, F32)
        acc_ref[0:HEAD_DIM, :] = jnp.zeros((HEAD_DIM, GROUP * tq), F32)
        acc_ref[HEAD_DIM:, :] = jnp.full((ONES_ROWS, GROUP * tq), 1.0 if use_sink else 0.0, F32)
        vext_ref[HEAD_DIM:, :] = jnp.ones((ONES_ROWS, vext_ref.shape[1]), BF16)
        if has_extra:
            vextx_ref[HEAD_DIM:, :] = jnp.ones((ONES_ROWS, vextx_ref.shape[1]), BF16)

    n_chunks = GROUP * tq // Q_CHUNK

    def step(kk_ref, vv_ref, vext):
        tkk = kk_ref.shape[0]
        kb = min(tkk, K_SUB)
        units = [(b, c) for b in range(tkk // kb) for c in range(n_chunks)]
        vext[0:HEAD_DIM, :] = vv_ref[...]

        def scores(u):
            b, c = u
            return _dot_nt(kk_ref[b * kb:(b + 1) * kb, :], q2_ref[c * Q_CHUNK:(c + 1) * Q_CHUNK, :])

        m = [m_ref[:, c * Q_CHUNK:(c + 1) * Q_CHUNK] for c in range(n_chunks)]
        acc = [acc_ref[:, c * Q_CHUNK:(c + 1) * Q_CHUNK] for c in range(n_chunks)]
        ahead = [scores(u) for u in units[:QK_AHEAD]]
        for idx, (b, c) in enumerate(units):
            s = ahead.pop(0)
            if idx + QK_AHEAD < len(units):
                ahead.append(scores(units[idx + QK_AHEAD]))
            m_next = jnp.maximum(m[c], jnp.max(s, axis=0, keepdims=True))
            alpha = jnp.exp2(m[c] - m_next)
            p = jnp.exp2(s - m_next).astype(BF16)
            acc[c] = alpha * acc[c] + jnp.dot(vext[:, b * kb:(b + 1) * kb], p, preferred_element_type=F32)
            m[c] = m_next
        acc_ref[...] = jnp.concatenate(acc, axis=1)
        m_ref[...] = jnp.concatenate(m, axis=1)

    @pl.when(j < n_main)
    def _():
        step(k_ref, vt_ref, vext_ref)

    if has_extra:
        @pl.when(j == n_main)
        def _():
            step(kx_ref, vtx_ref, vextx_ref)

    @pl.when(j == n_main + (1 if has_extra else 0) - 1)
    def _():
        o = (acc_ref[0:HEAD_DIM, :] / acc_ref[HEAD_DIM:HEAD_DIM + 1, :]).T
        for g in range(GROUP):
            o_ref[:, g * HEAD_DIM:(g + 1) * HEAD_DIM] = o[g * tq:(g + 1) * tq, :].astype(BF16)


def _flash(q_arr, k_arr, vt_arr, kx_arr, vtx_arr, sink, *, batch, seq_q, seq_k, seq_x, tq, tk, branch, use_sink):
    nq = seq_q // tq
    n_main = seq_k // tk
    has_extra = kx_arr is not None
    n_steps = n_main + (1 if has_extra else 0)
    qc = branch * N_KV
    kr = branch * N_KV

    def jj(j):
        return jnp.minimum(j, n_main - 1)

    in_specs = [pl.BlockSpec(memory_space=pltpu.SMEM),
                pl.BlockSpec((tq, GROUP * HEAD_DIM), lambda b, h, i, j: (b * nq + i, qc + h)),
                pl.BlockSpec((None, tk, HEAD_DIM), lambda b, h, i, j: (kr + h, b * n_main + jj(j), 0)),
                pl.BlockSpec((HEAD_DIM, tk), lambda b, h, i, j: (kr + h, b * n_main + jj(j)))]
    args = [sink, q_arr, k_arr, vt_arr]
    rows = GROUP * tq
    scratch = [pltpu.VMEM((rows, HEAD_DIM), BF16),
               pltpu.VMEM((1, rows), F32),
               pltpu.VMEM((HEAD_DIM + ONES_ROWS, rows), F32),
               pltpu.VMEM((HEAD_DIM + ONES_ROWS, tk), BF16)]
    if has_extra:
        in_specs += [pl.BlockSpec((None, seq_x, HEAD_DIM), lambda b, h, i, j: (kr + h, b, 0)),
                     pl.BlockSpec((HEAD_DIM, seq_x), lambda b, h, i, j: (kr + h, b))]
        args += [kx_arr, vtx_arr]
        scratch += [pltpu.VMEM((HEAD_DIM + ONES_ROWS, seq_x), BF16)]
    return pl.pallas_call(
        functools.partial(_flash_kernel, n_main=n_main, has_extra=has_extra, use_sink=use_sink),
        out_shape=jax.ShapeDtypeStruct((batch * seq_q, Q_W), BF16),
        grid=(batch, N_KV, nq, n_steps),
        in_specs=in_specs,
        out_specs=pl.BlockSpec((tq, GROUP * HEAD_DIM), lambda b, h, i, j: (b * nq + i, h)),
        scratch_shapes=scratch,
        compiler_params=_cparams(("parallel", "parallel", "parallel", "arbitrary")),
        name="flash",
    )(*args)


def _merge_kernel(h_ref, ya_ref, yb_ref, yc_ref, yd_ref, wga_ref, wgb_ref, wgc_ref, wgd_ref,
                  wb_ref, bg_ref, o_ref):
    h = h_ref[...]
    m = None
    for k, (y_ref, wg_ref) in enumerate(((ya_ref, wga_ref), (yb_ref, wgb_ref), (yc_ref, wgc_ref), (yd_ref, wgd_ref))):
        logits = jnp.dot(h, wg_ref[...], preferred_element_type=F32) + bg_ref[k:k + 1, :]
        t = _sigmoid(logits) * jnp.dot(y_ref[...], wb_ref[k], preferred_element_type=F32)
        m = t if m is None else m + t
    o_ref[...] = m.astype(BF16)


def _merge(h, ys, wg, wb, bg, *, layer, tm, tn):
    t = h.shape[0]
    nct = D_MODEL // tn

    def y():
        return pl.BlockSpec((tm, W_BR), lambda i, j: (i, 0))

    def g(k):
        return pl.BlockSpec((None, D_MODEL, tn), lambda i, j: (layer, 0, OFF_G // tn + k * nct + j))

    return pl.pallas_call(
        _merge_kernel,
        out_shape=jax.ShapeDtypeStruct((t, D_MODEL), BF16),
        grid=(t // tm, nct),
        in_specs=[pl.BlockSpec((tm, D_MODEL), lambda i, j: (i, 0)), y(), y(), y(), y(),
                  g(0), g(1), g(2), g(3),
                  pl.BlockSpec((None, N_BRANCH, W_BR, tn), lambda i, j: (layer, 0, 0, j)),
                  pl.BlockSpec((N_BRANCH, tn), lambda i, j: (0, j))],
        out_specs=pl.BlockSpec((tm, tn), lambda i, j: (i, j)),
        compiler_params=_cparams(("parallel", "arbitrary")),
        name="merge",
    )(h, *ys, wg, wg, wg, wg, wb, bg)


def _resid_kernel(a_ref, w_ref, x_ref, gate_ref, o_ref, *scratch, nk):
    part = jnp.dot(a_ref[...], w_ref[...], preferred_element_type=F32)
    if nk == 1:
        o_ref[...] = x_ref[...] + gate_ref[0] * part
        return
    (acc_ref,) = scratch
    k = pl.program_id(1)

    @pl.when(k == 0)
    def _():
        acc_ref[...] = part

    if nk > 2:
        @pl.when((k > 0) & (k < nk - 1))
        def _():
            acc_ref[...] += part

    @pl.when(k == nk - 1)
    def _():
        o_ref[...] = x_ref[...] + gate_ref[0] * (acc_ref[...] + part)


def _resid(a, w, x, gate, *, layer, seq, tm, tk):
    t, k = a.shape
    nbt = seq // tm
    nk = k // tk
    return pl.pallas_call(
        functools.partial(_resid_kernel, nk=nk),
        out_shape=jax.ShapeDtypeStruct((t, D_MODEL), F32),
        grid=(t // tm, nk),
        in_specs=[pl.BlockSpec((tm, tk), lambda i, j: (i, j)),
                  pl.BlockSpec((None, tk, D_MODEL), lambda i, j: (layer, j, 0)),
                  pl.BlockSpec((tm, D_MODEL), lambda i, j: (i, 0)),
                  pl.BlockSpec((1, 1, D_MODEL), lambda i, j: (i // nbt, 0, 0))],
        out_specs=pl.BlockSpec((tm, D_MODEL), lambda i, j: (i, 0)),
        scratch_shapes=[pltpu.VMEM((tm, D_MODEL), F32)] if nk > 1 else [],
        compiler_params=_cparams(("parallel", "arbitrary")),
        name="resid",
    )(a, w, x, gate)


def _resid_norm_kernel(a_ref, w_ref, x_ref, gate_ref, gn_ref, o_ref, acc_ref, *, nk):
    k = pl.program_id(1)
    part = jnp.dot(a_ref[...], w_ref[...], preferred_element_type=F32)

    @pl.when(k == 0)
    def _():
        acc_ref[...] = part

    @pl.when(k > 0)
    def _():
        acc_ref[...] += part

    @pl.when(k == nk - 1)
    def _():
        gate = gate_ref[0]
        gn = gn_ref[...]

        def body(i, carry):
            rows = pl.ds(pl.multiple_of(i * NORM_CHUNK, NORM_CHUNK), NORM_CHUNK)
            y = x_ref[rows, :] + gate * acc_ref[rows, :]
            o_ref[rows, :] = y * lax.rsqrt(jnp.mean(y * y, axis=-1, keepdims=True) + EPS) * gn
            return carry

        lax.fori_loop(0, o_ref.shape[0] // NORM_CHUNK, body, 0, unroll=NORM_UNROLL)


def _resid_norm(a, w, x, gate, gn, *, layer, seq, tm, tk):
    t, k = a.shape
    nbt = seq // tm
    nk = k // tk
    return pl.pallas_call(
        functools.partial(_resid_norm_kernel, nk=nk),
        out_shape=jax.ShapeDtypeStruct((t, D_MODEL), F32),
        grid=(t // tm, nk),
        in_specs=[pl.BlockSpec((tm, tk), lambda i, j: (i, j)),
                  pl.BlockSpec((None, tk, D_MODEL), lambda i, j: (layer, j, 0)),
                  pl.BlockSpec((tm, D_MODEL), lambda i, j: (i, 0)),
                  pl.BlockSpec((1, 1, D_MODEL), lambda i, j: (i // nbt, 0, 0)),
                  pl.BlockSpec((1, D_MODEL), lambda i, j: (0, 0))],
        out_specs=pl.BlockSpec((tm, D_MODEL), lambda i, j: (i, 0)),
        scratch_shapes=[pltpu.VMEM((tm, D_MODEL), F32)],
        compiler_params=_cparams(("parallel", "arbitrary")),
        name="resid_norm",
    )(a, w, x, gate, gn)


def _ffn_in_kernel(x_ref, g_ref, sc_ref, sh_ref, wa_ref, wb_ref, o_ref, h_ref):
    @pl.when(pl.program_id(1) == 0)
    def _():
        _norm_mod_rows(x_ref, g_ref, sc_ref, sh_ref, h_ref)

    h = h_ref[...]
    half = o_ref.shape[1] // 2
    parts = []
    for c in range(2):
        cols = slice(c * half, (c + 1) * half)
        parts.append((jnp.dot(h, wa_ref[:, cols], preferred_element_type=F32),
                      jnp.dot(h, wb_ref[:, cols], preferred_element_type=F32)))
    for c, (a, b) in enumerate(parts):
        o_ref[:, c * half:(c + 1) * half] = (a * _sigmoid(a) * b).astype(BF16)


def _ffn_in(x, g, sc, sh, w, *, layer, seq, tm, tn):
    t = x.shape[0]
    nbt = seq // tm
    nct = FFN_HIDDEN // tn
    return pl.pallas_call(
        _ffn_in_kernel,
        out_shape=jax.ShapeDtypeStruct((t, FFN_HIDDEN), BF16),
        grid=(t // tm, nct),
        in_specs=[pl.BlockSpec((tm, D_MODEL), lambda i, j: (i, 0)),
                  pl.BlockSpec((1, D_MODEL), lambda i, j: (0, 0)),
                  pl.BlockSpec((1, 1, D_MODEL), lambda i, j: (i // nbt, 0, 0)),
                  pl.BlockSpec((1, 1, D_MODEL), lambda i, j: (i // nbt, 0, 0)),
                  pl.BlockSpec((None, D_MODEL, tn), lambda i, j: (layer, 0, j)),
                  pl.BlockSpec((None, D_MODEL, tn), lambda i, j: (layer, 0, nct + j))],
        out_specs=pl.BlockSpec((tm, tn), lambda i, j: (i, j)),
        scratch_shapes=[pltpu.VMEM((tm, D_MODEL), BF16)],
        compiler_params=_cparams(("parallel", "arbitrary")),
        name="ffn_in",
    )(x, g, sc, sh, w, w)


def _rope_tables(seq):
    n_rows = seq // GRID_W
    inv = ROPE_THETA ** (-jnp.arange(0, ROT_AXIS, 2, dtype=F32) / ROT_AXIS)
    ar = jnp.arange(n_rows, dtype=F32)[:, None] * inv
    ac = jnp.arange(GRID_W, dtype=F32)[:, None] * inv
    shape = (n_rows, GRID_W, ROT_AXIS // 2)

    def grid(fr, fc):
        r = jnp.broadcast_to(fr[:, None, :], shape)
        c = jnp.broadcast_to(fc[None, :, :], shape)
        return r, c

    cr, cc = grid(jnp.cos(ar), jnp.cos(ac))
    sr, sc = grid(jnp.sin(ar), jnp.sin(ac))
    cos = jnp.concatenate([cr, cc, cr, cc], axis=-1).reshape(seq, HEAD_DIM)
    sin_signed = jnp.concatenate([-sr, -sc, sr, sc], axis=-1).reshape(seq, HEAD_DIM)
    return cos, sin_signed


def _pair_split(v):
    quarter = HEAD_DIM // 4
    q = v.reshape(v.shape[:-1] + (v.shape[-1] // HEAD_DIM, 4, quarter))
    q = jnp.concatenate([q[..., 0:1, :], q[..., 2:3, :], q[..., 1:2, :], q[..., 3:4, :]], axis=-2)
    return q.reshape(v.shape)


CAST_ROWS = 128
ROTARY_SPANS = ((OFF_BQ, OFF_BV), (OFF_DQ, OFF_DV))


def _cast_w_in_kernel(w_ref, o_ref):
    quarter = HEAD_DIM // 4
    lane = lax.broadcasted_iota(jnp.int32, (w_ref.shape[0], HEAD_DIM), 1)
    second = (lane >= quarter) & (lane < 2 * quarter)
    third = (lane >= 2 * quarter) & (lane < 3 * quarter)
    edges = sorted({0, w_ref.shape[1]} | {e for span in ROTARY_SPANS for e in span})
    for lo, hi in zip(edges[:-1], edges[1:]):
        if (lo, hi) in ROTARY_SPANS:
            for c in range(lo, hi, HEAD_DIM):
                xh = w_ref[:, c:c + HEAD_DIM]
                xh = jnp.where(second, pltpu.roll(xh, HEAD_DIM - quarter, 1),
                               jnp.where(third, pltpu.roll(xh, quarter, 1), xh))
                o_ref[:, c:c + HEAD_DIM] = xh.astype(BF16)
        else:
            o_ref[:, lo:hi] = w_ref[:, lo:hi].astype(BF16)


def _cast_w_in(w):
    depth, d, p = w.shape
    return pl.pallas_call(
        _cast_w_in_kernel,
        out_shape=jax.ShapeDtypeStruct(w.shape, BF16),
        grid=(depth, d // CAST_ROWS),
        in_specs=[pl.BlockSpec((None, CAST_ROWS, p), lambda l, i: (l, i, 0))],
        out_specs=pl.BlockSpec((None, CAST_ROWS, p), lambda l, i: (l, i, 0)),
        compiler_params=_cparams(("parallel", "parallel")),
        name="cast_w_in",
    )(w)


def _tile(n, pref):
    return pref if n % pref == 0 else n


class _Tiles(NamedTuple):
    proj: int
    rows: int
    conv_a: int
    conv_c: int
    win_q: int
    flash_q: int
    flash_k: int


def _tiles(seq):
    return _Tiles(proj=_tile(seq, 1024), rows=_tile(seq, 512), conv_a=_tile(seq, 512), conv_c=_tile(seq, 512),
                  win_q=_tile(seq, 512), flash_q=_tile(seq, 1024), flash_k=_tile(seq, 8192))


def _forward(x, c, ctx, c_ctx, w_ada, b_ada, norm_mix, norm_ffn, w_in, b_gate, conv_a_w, sink_b,
             qk_norm_q, qk_norm_k, conv_c_w, conv_c_b, ln_c_g, ln_c_b, w_branch, w_out,
             w_ffn_in, w_ffn_out, norm_final):
    batch, seq, _ = x.shape
    n_ctx = ctx.shape[1]
    depth = w_in.shape[0]
    xs = x.reshape(batch * seq, D_MODEL)
    xc = ctx.reshape(batch * n_ctx, D_MODEL)

    tiles = _tiles(seq)
    tm = tiles.proj

    rope_x = _rope_tables(seq)
    rope_c = (jnp.ones((n_ctx, HEAD_DIM), F32), jnp.zeros((n_ctx, HEAD_DIM), F32))
    cvec = jnp.zeros((ADA_ROWS, D_MODEL), F32).at[:batch].set(c).at[batch].set(c_ctx)

    def row(v):
        return v.reshape(1, -1)

    w_proj = w_gate = _cast_w_in(w_in)
    w_br = w_branch.astype(BF16)
    w_o = w_out.astype(BF16)
    w_f1 = w_ffn_in.astype(BF16)
    w_f2 = w_ffn_out.astype(BF16)

    def mixers(pf, q, k, vt, k_c, vt_c, l, *, s, tma, tmc, is_ctx):
        y_a = _conv_a(pf, conv_a_w[l], seq=s, tm=tma)
        y_c = _conv_c(pf, conv_c_w[l], row(conv_c_b[l]), row(ln_c_g[l]), row(ln_c_b[l]), seq=s, tm=tmc)
        if is_ctx:
            y_b = _flash(q, k, vt, None, None, sink_b[l], batch=batch, seq_q=s, seq_k=s, seq_x=0, tq=s, tk=s,
                         branch=0, use_sink=True)
            y_d = _flash(q, k, vt, None, None, sink_b[l], batch=batch, seq_q=s, seq_k=s, seq_x=0, tq=s, tk=s,
                         branch=1, use_sink=False)
        else:
            y_b = _win_attn(q, k, vt, k_c, vt_c, sink_b[l], batch=batch, seq=s, ctx=n_ctx, tq=tiles.win_q)
            y_d = _flash(q, k, vt, k_c, vt_c, sink_b[l], batch=batch, seq_q=s, seq_k=s, seq_x=n_ctx,
                         tq=tiles.flash_q, tk=tiles.flash_k, branch=1, use_sink=False)
        return [y_a, y_b, y_c, y_d]

    for l in range(depth):
        last = l == depth - 1
        mod = _ada(cvec, w_ada, row(b_ada[l]), layer=l)
        mx = mod[:batch].reshape(batch, 1, 6, D_MODEL)
        mcx = jnp.broadcast_to(mod[batch].reshape(1, 1, 6, D_MODEL), (batch, 1, 6, D_MODEL))
        sh_m, sc_m, g_m, sh_f, sc_f, g_f = (mx[:, :, k] for k in range(6))
        csh_m, csc_m, cg_m, csh_f, csc_f, cg_f = (mcx[:, :, k] for k in range(6))
        gq, gk = row(_pair_split(qk_norm_q[l])), row(_pair_split(qk_norm_k[l]))

        pf_c, q_c, h_c, k_c, vt_c = _inproj(xc, row(norm_mix[l]), csc_m, csh_m, w_proj, *rope_c, gq, gk,
                                            layer=l, seq=n_ctx, tm=n_ctx)
        pf, q, h, k, vt = _inproj(xs, row(norm_mix[l]), sc_m, sh_m, w_proj, *rope_x, gq, gk,
                                  layer=l, seq=seq, tm=tm)

        ys = mixers(pf, q, k, vt, k_c, vt_c, l, s=seq, tma=tiles.conv_a, tmc=tiles.conv_c, is_ctx=False)
        m = _merge(h, ys, w_gate, w_br, b_gate[l], layer=l, tm=tm, tn=512)
        xs = _resid(m, w_o, xs, g_m, layer=l, seq=seq, tm=tiles.rows, tk=D_MODEL)

        if not last:
            ys_c = mixers(pf_c, q_c, k_c, vt_c, None, None, l, s=n_ctx, tma=n_ctx, tmc=n_ctx, is_ctx=True)
            t_c = batch * n_ctx
            m_c = _merge(h_c, ys_c, w_gate, w_br, b_gate[l], layer=l, tm=t_c, tn=512)
            xc = _resid(m_c, w_o, xc, cg_m, layer=l, seq=t_c, tm=t_c, tk=D_MODEL)
            hid_c = _ffn_in(xc, row(norm_ffn[l]), csc_f, csh_f, w_f1, layer=l, seq=t_c, tm=t_c, tn=512)
            xc = _resid(hid_c, w_f2, xc, cg_f, layer=l, seq=t_c, tm=t_c, tk=FFN_HIDDEN // 2)

        hid = _ffn_in(xs, row(norm_ffn[l]), sc_f, sh_f, w_f1, layer=l, seq=seq, tm=tm, tn=512)
        if last:
            xs = _resid_norm(hid, w_f2, xs, g_f, row(norm_final), layer=l, seq=seq, tm=tiles.rows,
                             tk=FFN_HIDDEN // 2)
        else:
            xs = _resid(hid, w_f2, xs, g_f, layer=l, seq=seq, tm=tiles.rows, tk=FFN_HIDDEN // 2)

    return xs.reshape(batch, seq, D_MODEL)


def kernel(x, c, ctx, c_ctx, w_ada, b_ada, norm_mix, norm_ffn, w_in, b_gate, conv_a_w, sink_b, qk_norm_q,
           qk_norm_k, conv_c_w, conv_c_b, ln_c_g, ln_c_b, w_branch, w_out, w_ffn_in, w_ffn_out, norm_final):
    return _forward(x, c, ctx, c_ctx, w_ada, b_ada, norm_mix, norm_ffn, w_in, b_gate, conv_a_w, sink_b,
                    qk_norm_q, qk_norm_k, conv_c_w, conv_c_b, ln_c_g, ln_c_b, w_branch, w_out,
                    w_ffn_in, w_ffn_out, norm_final)
```

```python
import functools
from typing import NamedTuple

import jax
import jax.numpy as jnp
from jax import lax
from jax.experimental import pallas as pl
from jax.experimental.pallas import tpu as pltpu

F32 = jnp.float32
BF16 = jnp.bfloat16

D_MODEL = 2048
N_BRANCH = 4
W_BR = D_MODEL // 4
HEAD_DIM = 128
N_Q = W_BR // HEAD_DIM
N_KV = N_Q // 2
GROUP = N_Q // N_KV
Q_W = N_Q * HEAD_DIM
KV_W = N_KV * HEAD_DIM
CONV_A = 3
CONV_C = 31
WINDOW = 128
GRID_W = 64
ROT_AXIS = HEAD_DIM // 2
ROPE_THETA = 10000.0
FFN_HIDDEN = 5632
EPS = 1e-6
ATTN_SCALE = HEAD_DIM ** -0.5
LOG2E = 1.4426950408889634
Q_SCALE = ATTN_SCALE * LOG2E
NEG_BIG = -1e30

OFF_A = 0
OFF_BQ = OFF_A + 3 * W_BR
OFF_BK = OFF_BQ + Q_W
OFF_BV = OFF_BK + KV_W
OFF_C = OFF_BV + KV_W
OFF_DQ = OFF_C + 2 * W_BR
OFF_DK = OFF_DQ + Q_W
OFF_DV = OFF_DK + KV_W
OFF_G = OFF_DV + KV_W

PF_W = 3 * W_BR + 2 * W_BR
PQ_W = 2 * (Q_W + 2 * KV_W)
TN = 512
N_PF_TILES = PF_W // TN
N_PROJ_TILES = (PF_W + PQ_W) // TN

VMEM_LIMIT = 56 * 1024 * 1024


def _cparams(sem):
    return pltpu.CompilerParams(dimension_semantics=sem, vmem_limit_bytes=VMEM_LIMIT)


def _sigmoid(z):
    return 1.0 / (1.0 + jnp.exp(-z))


def _ada_kernel(c_ref, w_ref, b_ref, o_ref):
    c = c_ref[...]
    s = (c * _sigmoid(c)).astype(BF16)
    o_ref[...] = jnp.dot(s, w_ref[...].astype(BF16), preferred_element_type=F32) + b_ref[...]


ADA_ROWS = 16


def _ada(cvec, w, b, *, layer):
    n = w.shape[2]
    tn = 1024
    return pl.pallas_call(
        _ada_kernel,
        out_shape=jax.ShapeDtypeStruct((ADA_ROWS, n), F32),
        grid=(n // tn,),
        in_specs=[pl.BlockSpec((ADA_ROWS, D_MODEL), lambda j: (0, 0)),
                  pl.BlockSpec((None, D_MODEL, tn), lambda j: (layer, 0, j)),
                  pl.BlockSpec((1, tn), lambda j: (0, j))],
        out_specs=pl.BlockSpec((ADA_ROWS, tn), lambda j: (0, j)),
        compiler_params=_cparams(("arbitrary",)),
        name="ada",
    )(cvec, w, b)


def _rope(xh, cos, sin_signed):
    return xh * cos + pltpu.roll(xh, HEAD_DIM // 2, 1) * sin_signed


def _head_norm(xh, g):
    return xh * lax.rsqrt(jnp.mean(xh * xh, axis=-1, keepdims=True) + EPS) * g


NORM_CHUNK = 16
NORM_UNROLL = 8


def _norm_mod_rows(x_ref, g_ref, sc_ref, sh_ref, h_ref):
    gs = g_ref[...] * (1.0 + sc_ref[0])
    sh = sh_ref[0]

    def body(i, carry):
        rows = pl.ds(pl.multiple_of(i * NORM_CHUNK, NORM_CHUNK), NORM_CHUNK)
        x = x_ref[rows, :]
        r = lax.rsqrt(jnp.mean(x * x, axis=-1, keepdims=True) + EPS)
        h_ref[rows, :] = (x * r * gs + sh).astype(BF16)
        return carry

    lax.fori_loop(0, x_ref.shape[0] // NORM_CHUNK, body, 0, unroll=NORM_UNROLL)


def _inproj_kernel(x_ref, g_ref, sc_ref, sh_ref, w_ref, cos_ref, sin_ref, gq_ref, gk_ref,
                   pf_ref, q_ref, h_ref, k_ref, vt_ref):
    j = pl.program_id(1)

    @pl.when(j == 0)
    def _():
        _norm_mod_rows(x_ref, g_ref, sc_ref, sh_ref, h_ref)

    acc = jnp.dot(h_ref[...], w_ref[...], preferred_element_type=F32)

    @pl.when(j < N_PF_TILES)
    def _():
        pf_ref[...] = acc.astype(BF16)

    def rope(xh):
        return _rope(xh, cos_ref[...], sin_ref[...])

    def head(k):
        return acc[:, k * HEAD_DIM:(k + 1) * HEAD_DIM]

    def put_q(k, v):
        q_ref[:, k * HEAD_DIM:(k + 1) * HEAD_DIM] = v.astype(BF16)

    @pl.when(j == N_PF_TILES)
    def _():
        for k in range(N_Q):
            put_q(k, rope(head(k)) * Q_SCALE)

    @pl.when(j == N_PF_TILES + 1)
    def _():
        for k in range(N_KV):
            k_ref[k] = rope(head(k)).astype(BF16)
        vt_ref[0:KV_W, :] = acc[:, KV_W:].T.astype(BF16)

    @pl.when(j == N_PF_TILES + 2)
    def _():
        for k in range(N_Q):
            put_q(k, rope(_head_norm(head(k), gq_ref[...])) * Q_SCALE)

    @pl.when(j == N_PF_TILES + 3)
    def _():
        for k in range(N_KV):
            k_ref[N_KV + k] = rope(_head_norm(head(k), gk_ref[...])).astype(BF16)
        vt_ref[KV_W:, :] = acc[:, KV_W:].T.astype(BF16)


def _proj_block(j):
    n_a, n_b, n_c = 3 * W_BR // TN, (Q_W + 2 * KV_W) // TN, 2 * W_BR // TN
    return jnp.where(j < n_a, j, jnp.where(j < n_a + n_c, j + n_b, jnp.where(j < n_a + n_c + n_b, j - n_c, j)))


def _inproj(x, g, sc, sh, w, cos, sin_signed, gq, gk, *, layer, seq, tm):
    t = x.shape[0]
    nbt = seq // tm
    return pl.pallas_call(
        _inproj_kernel,
        out_shape=(jax.ShapeDtypeStruct((t, PF_W), BF16),
                   jax.ShapeDtypeStruct((t, 2 * Q_W), BF16),
                   jax.ShapeDtypeStruct((t, D_MODEL), BF16),
                   jax.ShapeDtypeStruct((2 * N_KV, t, HEAD_DIM), BF16),
                   jax.ShapeDtypeStruct((2 * KV_W, t), BF16)),
        grid=(t // tm, N_PROJ_TILES),
        in_specs=[pl.BlockSpec((tm, D_MODEL), lambda i, j: (i, 0)),
                  pl.BlockSpec((1, D_MODEL), lambda i, j: (0, 0)),
                  pl.BlockSpec((1, 1, D_MODEL), lambda i, j: (i // nbt, 0, 0)),
                  pl.BlockSpec((1, 1, D_MODEL), lambda i, j: (i // nbt, 0, 0)),
                  pl.BlockSpec((None, D_MODEL, TN), lambda i, j: (layer, 0, _proj_block(j))),
                  pl.BlockSpec((tm, HEAD_DIM), lambda i, j: (i % nbt, 0)),
                  pl.BlockSpec((tm, HEAD_DIM), lambda i, j: (i % nbt, 0)),
                  pl.BlockSpec((1, HEAD_DIM), lambda i, j: (0, 0)),
                  pl.BlockSpec((1, HEAD_DIM), lambda i, j: (0, 0))],
        out_specs=(pl.BlockSpec((tm, TN), lambda i, j: (i, jnp.minimum(j, N_PF_TILES - 1))),
                   pl.BlockSpec((tm, Q_W), lambda i, j: (i, jnp.where(j >= N_PF_TILES + 2, 1, 0))),
                   pl.BlockSpec((tm, D_MODEL), lambda i, j: (i, 0)),
                   pl.BlockSpec((2 * N_KV, tm, HEAD_DIM), lambda i, j: (0, i, 0)),
                   pl.BlockSpec((2 * KV_W, tm), lambda i, j: (0, i))),
        compiler_params=_cparams(("parallel", "arbitrary")),
        name="inproj",
    )(x, g, sc, sh, w, cos, sin_signed, gq, gk)


HALO_A = 16


def _conv_a_kernel(bg_ref, cg_ref, h_ref, cgp_ref, hp_ref, cgn_ref, hn_ref, w_ref, o_ref, u_ref, *, nbt):
    it = pl.program_id(0) % nbt
    tm = bg_ref.shape[0]
    def gated(gate_ref, val_ref):
        return gate_ref[...].astype(F32) * val_ref[...].astype(F32)

    u_ref[HALO_A:HALO_A + tm, :] = gated(cg_ref, h_ref)
    u_ref[0:HALO_A, :] = jnp.where(it > 0, gated(cgp_ref, hp_ref), 0.0)
    u_ref[HALO_A + tm:, :] = jnp.where(it < nbt - 1, gated(cgn_ref, hn_ref), 0.0)
    y = (w_ref[0:1, :] * u_ref[HALO_A - 1:HALO_A - 1 + tm, :]
         + w_ref[1:2, :] * u_ref[HALO_A:HALO_A + tm, :]
         + w_ref[2:3, :] * u_ref[HALO_A + 1:HALO_A + 1 + tm, :])
    o_ref[...] = (bg_ref[...].astype(F32) * y).astype(BF16)


def _conv_a(pf, w, *, seq, tm):
    t = pf.shape[0]
    nbt = seq // tm
    r = tm // HALO_A
    last = t // HALO_A - 1

    def cur(c):
        return pl.BlockSpec((tm, W_BR), lambda i: (i, c))

    def prev(c):
        return pl.BlockSpec((HALO_A, W_BR), lambda i: (jnp.maximum(i * r - 1, 0), c))

    def nxt(c):
        return pl.BlockSpec((HALO_A, W_BR), lambda i: (jnp.minimum((i + 1) * r, last), c))

    return pl.pallas_call(
        functools.partial(_conv_a_kernel, nbt=nbt),
        out_shape=jax.ShapeDtypeStruct((t, W_BR), BF16),
        grid=(t // tm,),
        in_specs=[cur(0), cur(1), cur(2), prev(1), prev(2), nxt(1), nxt(2),
                  pl.BlockSpec((CONV_A, W_BR), lambda i: (0, 0))],
        out_specs=pl.BlockSpec((tm, W_BR), lambda i: (i, 0)),
        scratch_shapes=[pltpu.VMEM((tm + 2 * HALO_A, W_BR), F32)],
        compiler_params=_cparams(("parallel",)),
        name="conv_a",
    )(pf, pf, pf, pf, pf, pf, pf, w)


HALO_C = 16
ROWS_C = 64
SUBLANES = 8


def _conv_c_kernel(v_ref, g_ref, vp_ref, gp_ref, vn_ref, gn_ref, w_ref, b_ref, lg_ref, lb_ref,
                   o_ref, u_ref, us_ref, *, nbt):
    it = pl.program_id(0) % nbt
    tm = v_ref.shape[0]

    def glu(val_ref, gate_ref):
        return val_ref[...].astype(F32) * _sigmoid(gate_ref[...].astype(F32))

    u_ref[HALO_C:HALO_C + tm, :] = glu(v_ref, g_ref)
    u_ref[0:HALO_C, :] = jnp.where(it > 0, glu(vp_ref, gp_ref), 0.0)
    u_ref[HALO_C + tm:, :] = jnp.where(it < nbt - 1, glu(vn_ref, gn_ref), 0.0)
    n_sh = us_ref.shape[1]
    for r in range(1, SUBLANES):
        us_ref[r - 1] = u_ref[r:r + n_sh, :]
    pad = CONV_C // 2

    def tap(row):
        a, r = divmod(row, SUBLANES)
        if r == 0:
            return u_ref[a * SUBLANES:a * SUBLANES + ROWS_C, :]
        return us_ref[r - 1, a * SUBLANES:a * SUBLANES + ROWS_C, :]

    for c in range(tm // ROWS_C):
        base = HALO_C + c * ROWS_C - pad
        acc = w_ref[0:1, :] * tap(base)
        for k in range(1, CONV_C):
            acc = acc + w_ref[k:k + 1, :] * tap(base + k)
        acc = acc + b_ref[...]
        mu = jnp.mean(acc, axis=-1, keepdims=True)
        xc = acc - mu
        var = jnp.mean(xc * xc, axis=-1, keepdims=True)
        y = xc * lax.rsqrt(var + EPS) * lg_ref[...] + lb_ref[...]
        o_ref[c * ROWS_C:(c + 1) * ROWS_C, :] = (y * _sigmoid(y)).astype(BF16)


def _conv_c(pf, w, b, lg, lb, *, seq, tm):
    t = pf.shape[0]
    nbt = seq // tm
    r = tm // HALO_C
    last = t // HALO_C - 1
    c0 = 3

    def cur(c):
        return pl.BlockSpec((tm, W_BR), lambda i: (i, c))

    def prev(c):
        return pl.BlockSpec((HALO_C, W_BR), lambda i: (jnp.maximum(i * r - 1, 0), c))

    def nxt(c):
        return pl.BlockSpec((HALO_C, W_BR), lambda i: (jnp.minimum((i + 1) * r, last), c))

    def vec():
        return pl.BlockSpec((1, W_BR), lambda i: (0, 0))

    return pl.pallas_call(
        functools.partial(_conv_c_kernel, nbt=nbt),
        out_shape=jax.ShapeDtypeStruct((t, W_BR), BF16),
        grid=(t // tm,),
        in_specs=[cur(c0), cur(c0 + 1), prev(c0), prev(c0 + 1), nxt(c0), nxt(c0 + 1),
                  pl.BlockSpec((CONV_C, W_BR), lambda i: (0, 0)), vec(), vec(), vec()],
        out_specs=pl.BlockSpec((tm, W_BR), lambda i: (i, 0)),
        scratch_shapes=[pltpu.VMEM((tm + 2 * HALO_C, W_BR), F32),
                        pltpu.VMEM((SUBLANES - 1, tm + 2 * HALO_C - SUBLANES, W_BR), F32)],
        compiler_params=_cparams(("parallel",)),
        name="conv_c",
    )(pf, pf, pf, pf, pf, pf, w, b, lg, lb)


def _dot_nt(a, b):
    return lax.dot_general(a, b, (((1,), (1,)), ((), ())), preferred_element_type=F32)


WIN_UNIT = 2 * WINDOW
WIN_AHEAD = 5


def _win_kernel(sink_ref, q_ref, kp_ref, kc_ref, kn_ref, kx_ref, vp_ref, vc_ref, vn_ref, vx_ref, o_ref,
                kk_ref, vext_ref, *, nq):
    hk = pl.program_id(1)
    i = pl.program_id(2)
    tq = q_ref.shape[0]
    n_loc = tq + 2 * WINDOW
    n_all = kk_ref.shape[0]
    u = WIN_UNIT
    kk_ref[0:WINDOW, :] = kp_ref[...]
    kk_ref[WINDOW:WINDOW + tq, :] = kc_ref[...]
    kk_ref[WINDOW + tq:n_loc, :] = kn_ref[...]
    kk_ref[n_loc:, :] = kx_ref[...]
    vext_ref[0:HEAD_DIM, 0:WINDOW] = vp_ref[...]
    vext_ref[0:HEAD_DIM, WINDOW:WINDOW + tq] = vc_ref[...]
    vext_ref[0:HEAD_DIM, WINDOW + tq:n_loc] = vn_ref[...]
    vext_ref[0:HEAD_DIM, n_loc:] = vx_ref[...]
    vext_ref[HEAD_DIM:, :] = jnp.ones((ONES_ROWS, n_all), BF16)

    key = lax.broadcasted_iota(jnp.int32, (u, u), 0)
    qry = lax.broadcasted_iota(jnp.int32, (u, u), 1)
    n_loc_blocks = n_loc // u
    n_chunks = tq // u

    def mask(c, kb):
        ok = jnp.abs((kb - c) * u - WINDOW + key - qry) <= WINDOW
        if kb == 0:
            ok = ok & (key >= jnp.where(i > 0, 0, WINDOW))
        if kb == n_loc_blocks - 1:
            ok = ok & (key < jnp.where(i < nq - 1, u, u - WINDOW))
        return ok

    units = []
    for g in range(GROUP):
        for c in range(n_chunks):
            units += [(g, c, kb) for kb in (c, c + 1)] + [(g, c, kb) for kb in range(n_loc_blocks, n_all // u)]

    def scores(unit):
        g, c, kb = unit
        return _dot_nt(kk_ref[kb * u:(kb + 1) * u, :], q_ref[c * u:(c + 1) * u, g * HEAD_DIM:(g + 1) * HEAD_DIM])

    m, acc = {}, {}
    for g in range(GROUP):
        sink = sink_ref[hk * GROUP + g] * LOG2E
        for c in range(n_chunks):
            m[g, c] = jnp.full((1, u), sink, F32)
            acc[g, c] = jnp.concatenate([jnp.zeros((HEAD_DIM, u), F32), jnp.ones((ONES_ROWS, u), F32)], axis=0)

    ahead = [scores(unit) for unit in units[:WIN_AHEAD]]
    for idx, (g, c, kb) in enumerate(units):
        s = ahead.pop(0)
        if idx + WIN_AHEAD < len(units):
            ahead.append(scores(units[idx + WIN_AHEAD]))
        if kb < n_loc_blocks:
            s = jnp.where(mask(c, kb), s, NEG_BIG)
        m_next = jnp.maximum(m[g, c], jnp.max(s, axis=0, keepdims=True))
        alpha = jnp.exp2(m[g, c] - m_next)
        p = jnp.exp2(s - m_next).astype(BF16)
        acc[g, c] = alpha * acc[g, c] + jnp.dot(vext_ref[:, kb * u:(kb + 1) * u], p, preferred_element_type=F32)
        m[g, c] = m_next

    outs = []
    for g in range(GROUP):
        o_t = jnp.concatenate([acc[g, c][0:HEAD_DIM, :] / acc[g, c][HEAD_DIM:HEAD_DIM + 1, :]
                               for c in range(n_chunks)], axis=1)
        outs.append(o_t.T.astype(BF16))
    o_ref[...] = jnp.concatenate(outs, axis=1)


def _win_attn(q, k, vt, k_c, vt_c, sink, *, batch, seq, ctx, tq):
    nq = seq // tq
    r = tq // WINDOW
    nblk = seq // WINDOW

    def prev_blk(b, i):
        return b * nblk + jnp.maximum(i * r - 1, 0)

    def next_blk(b, i):
        return b * nblk + jnp.minimum((i + 1) * r, nblk - 1)

    return pl.pallas_call(
        functools.partial(_win_kernel, nq=nq),
        out_shape=jax.ShapeDtypeStruct((batch * seq, Q_W), BF16),
        grid=(batch, N_KV, nq),
        in_specs=[pl.BlockSpec(memory_space=pltpu.SMEM),
                  pl.BlockSpec((tq, GROUP * HEAD_DIM), lambda b, h, i: (b * nq + i, h)),
                  pl.BlockSpec((None, WINDOW, HEAD_DIM), lambda b, h, i: (h, prev_blk(b, i), 0)),
                  pl.BlockSpec((None, tq, HEAD_DIM), lambda b, h, i: (h, b * nq + i, 0)),
                  pl.BlockSpec((None, WINDOW, HEAD_DIM), lambda b, h, i: (h, next_blk(b, i), 0)),
                  pl.BlockSpec((None, ctx, HEAD_DIM), lambda b, h, i: (h, b, 0)),
                  pl.BlockSpec((HEAD_DIM, WINDOW), lambda b, h, i: (h, prev_blk(b, i))),
                  pl.BlockSpec((HEAD_DIM, tq), lambda b, h, i: (h, b * nq + i)),
                  pl.BlockSpec((HEAD_DIM, WINDOW), lambda b, h, i: (h, next_blk(b, i))),
                  pl.BlockSpec((HEAD_DIM, ctx), lambda b, h, i: (h, b))],
        out_specs=pl.BlockSpec((tq, GROUP * HEAD_DIM), lambda b, h, i: (b * nq + i, h)),
        scratch_shapes=[pltpu.VMEM((tq + 2 * WINDOW + ctx, HEAD_DIM), BF16),
                        pltpu.VMEM((HEAD_DIM + ONES_ROWS, tq + 2 * WINDOW + ctx), BF16)],
        compiler_params=_cparams(("parallel", "parallel", "parallel")),
        name="win_attn",
    )(sink, q, k, k, k, k_c, vt, vt, vt, vt_c)


ONES_ROWS = 16
Q_CHUNK = 256
K_SUB = 256
QK_AHEAD = 5


def _flash_kernel(sink_ref, q_ref, k_ref, vt_ref, *rest, n_main, has_extra, use_sink):
    if has_extra:
        kx_ref, vtx_ref, o_ref, q2_ref, m_ref, acc_ref, vext_ref, vextx_ref = rest
    else:
        o_ref, q2_ref, m_ref, acc_ref, vext_ref = rest
    hk = pl.program_id(1)
    j = pl.program_id(3)
    tq = q_ref.shape[0]

    @pl.when(j == 0)
    def _():
        for g in range(GROUP):
            q2_ref[g * tq:(g + 1) * tq, :] = q_ref[:, g * HEAD_DIM:(g + 1) * HEAD_DIM]
            if use_sink:
                m_ref[:, g * tq:(g + 1) * tq] = jnp.full((1, tq), sink_ref[hk * GROUP + g] * LOG2E, F32)
        if not use_sink:
            m_ref[...] = jnp.full(m_ref.shape, NEG_BIG, F32)
        acc_ref[0:HEAD_DIM, :] = jnp.zeros((HEAD_DIM, GROUP * tq), F32)
        acc_ref[HEAD_DIM:, :] = jnp.full((ONES_ROWS, GROUP * tq), 1.0 if use_sink else 0.0, F32)
        vext_ref[HEAD_DIM:, :] = jnp.ones((ONES_ROWS, vext_ref.shape[1]), BF16)
        if has_extra:
            vextx_ref[HEAD_DIM:, :] = jnp.ones((ONES_ROWS, vextx_ref.shape[1]), BF16)

    n_chunks = GROUP * tq // Q_CHUNK

    def step(kk_ref, vv_ref, vext):
        tkk = kk_ref.shape[0]
        kb = min(tkk, K_SUB)
        units = [(b, c) for b in range(tkk // kb) for c in range(n_chunks)]
        vext[0:HEAD_DIM, :] = vv_ref[...]

        def scores(u):
            b, c = u
            return _dot_nt(kk_ref[b * kb:(b + 1) * kb, :], q2_ref[c * Q_CHUNK:(c + 1) * Q_CHUNK, :])

        m = [m_ref[:, c * Q_CHUNK:(c + 1) * Q_CHUNK] for c in range(n_chunks)]
        acc = [acc_ref[:, c * Q_CHUNK:(c + 1) * Q_CHUNK] for c in range(n_chunks)]
        ahead = [scores(u) for u in units[:QK_AHEAD]]
        for idx, (b, c) in enumerate(units):
            s = ahead.pop(0)
            if idx + QK_AHEAD < len(units):
                ahead.append(scores(units[idx + QK_AHEAD]))
            m_next = jnp.maximum(m[c], jnp.max(s, axis=0, keepdims=True))
            alpha = jnp.exp2(m[c] - m_next)
            p = jnp.exp2(s - m_next).astype(BF16)
            acc[c] = alpha * acc[c] + jnp.dot(vext[:, b * kb:(b + 1) * kb], p, preferred_element_type=F32)
            m[c] = m_next
        acc_ref[...] = jnp.concatenate(acc, axis=1)
        m_ref[...] = jnp.concatenate(m, axis=1)

    @pl.when(j < n_main)
    def _():
        step(k_ref, vt_ref, vext_ref)

    if has_extra:
        @pl.when(j == n_main)
        def _():
            step(kx_ref, vtx_ref, vextx_ref)

    @pl.when(j == n_main + (1 if has_extra else 0) - 1)
    def _():
        o = (acc_ref[0:HEAD_DIM, :] / acc_ref[HEAD_DIM:HEAD_DIM + 1, :]).T
        for g in range(GROUP):
            o_ref[:, g * HEAD_DIM:(g + 1) * HEAD_DIM] = o[g * tq:(g + 1) * tq, :].astype(BF16)


def _flash(q_arr, k_arr, vt_arr, kx_arr, vtx_arr, sink, *, batch, seq_q, seq_k, seq_x, tq, tk, branch, use_sink):
    nq = seq_q // tq
    n_main = seq_k // tk
    has_extra = kx_arr is not None
    n_steps = n_main + (1 if has_extra else 0)
    qc = branch * N_KV
    kr = branch * N_KV

    def jj(j):
        return jnp.minimum(j, n_main - 1)

    in_specs = [pl.BlockSpec(memory_space=pltpu.SMEM),
                pl.BlockSpec((tq, GROUP * HEAD_DIM), lambda b, h, i, j: (b * nq + i, qc + h)),
                pl.BlockSpec((None, tk, HEAD_DIM), lambda b, h, i, j: (kr + h, b * n_main + jj(j), 0)),
                pl.BlockSpec((HEAD_DIM, tk), lambda b, h, i, j: (kr + h, b * n_main + jj(j)))]
    args = [sink, q_arr, k_arr, vt_arr]
    rows = GROUP * tq
    scratch = [pltpu.VMEM((rows, HEAD_DIM), BF16),
               pltpu.VMEM((1, rows), F32),
               pltpu.VMEM((HEAD_DIM + ONES_ROWS, rows), F32),
               pltpu.VMEM((HEAD_DIM + ONES_ROWS, tk), BF16)]
    if has_extra:
        in_specs += [pl.BlockSpec((None, seq_x, HEAD_DIM), lambda b, h, i, j: (kr + h, b, 0)),
                     pl.BlockSpec((HEAD_DIM, seq_x), lambda b, h, i, j: (kr + h, b))]
        args += [kx_arr, vtx_arr]
        scratch += [pltpu.VMEM((HEAD_DIM + ONES_ROWS, seq_x), BF16)]
    return pl.pallas_call(
        functools.partial(_flash_kernel, n_main=n_main, has_extra=has_extra, use_sink=use_sink),
        out_shape=jax.ShapeDtypeStruct((batch * seq_q, Q_W), BF16),
        grid=(batch, N_KV, nq, n_steps),
        in_specs=in_specs,
        out_specs=pl.BlockSpec((tq, GROUP * HEAD_DIM), lambda b, h, i, j: (b * nq + i, h)),
        scratch_shapes=scratch,
        compiler_params=_cparams(("parallel", "parallel", "parallel", "arbitrary")),
        name="flash",
    )(*args)


def _merge_kernel(h_ref, ya_ref, yb_ref, yc_ref, yd_ref, wga_ref, wgb_ref, wgc_ref, wgd_ref,
                  wb_ref, bg_ref, o_ref):
    h = h_ref[...]
    half = o_ref.shape[1] // 2
    branches = ((ya_ref, wga_ref), (yb_ref, wgb_ref), (yc_ref, wgc_ref), (yd_ref, wgd_ref))
    for c in range(2):
        cols = slice(c * half, (c + 1) * half)
        m = None
        for k, (y_ref, wg_ref) in enumerate(branches):
            logits = jnp.dot(h, wg_ref[:, cols], preferred_element_type=F32) + bg_ref[k:k + 1, cols]
            t = _sigmoid(logits) * jnp.dot(y_ref[...], wb_ref[k, :, cols], preferred_element_type=F32)
            m = t if m is None else m + t
        o_ref[:, cols] = m.astype(BF16)


def _merge(h, ys, wg, wb, bg, *, layer, tm, tn):
    t = h.shape[0]
    nct = D_MODEL // tn

    def y():
        return pl.BlockSpec((tm, W_BR), lambda i, j: (i, 0))

    def g(k):
        return pl.BlockSpec((None, D_MODEL, tn), lambda i, j: (layer, 0, OFF_G // tn + k * nct + j))

    return pl.pallas_call(
        _merge_kernel,
        out_shape=jax.ShapeDtypeStruct((t, D_MODEL), BF16),
        grid=(t // tm, nct),
        in_specs=[pl.BlockSpec((tm, D_MODEL), lambda i, j: (i, 0)), y(), y(), y(), y(),
                  g(0), g(1), g(2), g(3),
                  pl.BlockSpec((None, N_BRANCH, W_BR, tn), lambda i, j: (layer, 0, 0, j)),
                  pl.BlockSpec((N_BRANCH, tn), lambda i, j: (0, j))],
        out_specs=pl.BlockSpec((tm, tn), lambda i, j: (i, j)),
        compiler_params=_cparams(("parallel", "arbitrary")),
        name="merge",
    )(h, *ys, wg, wg, wg, wg, wb, bg)


def _resid_kernel(a_ref, w_ref, x_ref, gate_ref, o_ref, *scratch, nk):
    part = jnp.dot(a_ref[...], w_ref[...], preferred_element_type=F32)
    if nk == 1:
        o_ref[...] = x_ref[...] + gate_ref[0] * part
        return
    (acc_ref,) = scratch
    k = pl.program_id(1)

    @pl.when(k == 0)
    def _():
        acc_ref[...] = part

    if nk > 2:
        @pl.when((k > 0) & (k < nk - 1))
        def _():
            acc_ref[...] += part

    @pl.when(k == nk - 1)
    def _():
        o_ref[...] = x_ref[...] + gate_ref[0] * (acc_ref[...] + part)


def _resid(a, w, x, gate, *, layer, seq, tm, tk):
    t, k = a.shape
    nbt = seq // tm
    nk = k // tk
    return pl.pallas_call(
        functools.partial(_resid_kernel, nk=nk),
        out_shape=jax.ShapeDtypeStruct((t, D_MODEL), F32),
        grid=(t // tm, nk),
        in_specs=[pl.BlockSpec((tm, tk), lambda i, j: (i, j)),
                  pl.BlockSpec((None, tk, D_MODEL), lambda i, j: (layer, j, 0)),
                  pl.BlockSpec((tm, D_MODEL), lambda i, j: (i, 0)),
                  pl.BlockSpec((1, 1, D_MODEL), lambda i, j: (i // nbt, 0, 0))],
        out_specs=pl.BlockSpec((tm, D_MODEL), lambda i, j: (i, 0)),
        scratch_shapes=[pltpu.VMEM((tm, D_MODEL), F32)] if nk > 1 else [],
        compiler_params=_cparams(("parallel", "arbitrary")),
        name="resid",
    )(a, w, x, gate)


def _resid_norm_kernel(a_ref, w_ref, x_ref, gate_ref, gn_ref, o_ref, acc_ref, *, nk):
    k = pl.program_id(1)
    part = jnp.dot(a_ref[...], w_ref[...], preferred_element_type=F32)

    @pl.when(k == 0)
    def _():
        acc_ref[...] = part

    @pl.when(k > 0)
    def _():
        acc_ref[...] += part

    @pl.when(k == nk - 1)
    def _():
        gate = gate_ref[0]
        gn = gn_ref[...]

        def body(i, carry):
            rows = pl.ds(pl.multiple_of(i * NORM_CHUNK, NORM_CHUNK), NORM_CHUNK)
            y = x_ref[rows, :] + gate * acc_ref[rows, :]
            o_ref[rows, :] = y * lax.rsqrt(jnp.mean(y * y, axis=-1, keepdims=True) + EPS) * gn
            return carry

        lax.fori_loop(0, o_ref.shape[0] // NORM_CHUNK, body, 0, unroll=NORM_UNROLL)


def _resid_norm(a, w, x, gate, gn, *, layer, seq, tm, tk):
    t, k = a.shape
    nbt = seq // tm
    nk = k // tk
    return pl.pallas_call(
        functools.partial(_resid_norm_kernel, nk=nk),
        out_shape=jax.ShapeDtypeStruct((t, D_MODEL), F32),
        grid=(t // tm, nk),
        in_specs=[pl.BlockSpec((tm, tk), lambda i, j: (i, j)),
                  pl.BlockSpec((None, tk, D_MODEL), lambda i, j: (layer, j, 0)),
                  pl.BlockSpec((tm, D_MODEL), lambda i, j: (i, 0)),
                  pl.BlockSpec((1, 1, D_MODEL), lambda i, j: (i // nbt, 0, 0)),
                  pl.BlockSpec((1, D_MODEL), lambda i, j: (0, 0))],
        out_specs=pl.BlockSpec((tm, D_MODEL), lambda i, j: (i, 0)),
        scratch_shapes=[pltpu.VMEM((tm, D_MODEL), F32)],
        compiler_params=_cparams(("parallel", "arbitrary")),
        name="resid_norm",
    )(a, w, x, gate, gn)


def _ffn_in_kernel(x_ref, g_ref, sc_ref, sh_ref, wa_ref, wb_ref, o_ref, h_ref):
    @pl.when(pl.program_id(1) == 0)
    def _():
        _norm_mod_rows(x_ref, g_ref, sc_ref, sh_ref, h_ref)

    h = h_ref[...]
    half = o_ref.shape[1] // 2
    parts = []
    for c in range(2):
        cols = slice(c * half, (c + 1) * half)
        parts.append((jnp.dot(h, wa_ref[:, cols], preferred_element_type=F32),
                      jnp.dot(h, wb_ref[:, cols], preferred_element_type=F32)))
    for c, (a, b) in enumerate(parts):
        o_ref[:, c * half:(c + 1) * half] = (a * _sigmoid(a) * b).astype(BF16)


def _ffn_in(x, g, sc, sh, w, *, layer, seq, tm, tn):
    t = x.shape[0]
    nbt = seq // tm
    nct = FFN_HIDDEN // tn
    return pl.pallas_call(
        _ffn_in_kernel,
        out_shape=jax.ShapeDtypeStruct((t, FFN_HIDDEN), BF16),
        grid=(t // tm, nct),
        in_specs=[pl.BlockSpec((tm, D_MODEL), lambda i, j: (i, 0)),
                  pl.BlockSpec((1, D_MODEL), lambda i, j: (0, 0)),
                  pl.BlockSpec((1, 1, D_MODEL), lambda i, j: (i // nbt, 0, 0)),
                  pl.BlockSpec((1, 1, D_MODEL), lambda i, j: (i // nbt, 0, 0)),
                  pl.BlockSpec((None, D_MODEL, tn), lambda i, j: (layer, 0, j)),
                  pl.BlockSpec((None, D_MODEL, tn), lambda i, j: (layer, 0, nct + j))],
        out_specs=pl.BlockSpec((tm, tn), lambda i, j: (i, j)),
        scratch_shapes=[pltpu.VMEM((tm, D_MODEL), BF16)],
        compiler_params=_cparams(("parallel", "arbitrary")),
        name="ffn_in",
    )(x, g, sc, sh, w, w)


def _rope_tables(seq):
    n_rows = seq // GRID_W
    inv = ROPE_THETA ** (-jnp.arange(0, ROT_AXIS, 2, dtype=F32) / ROT_AXIS)
    ar = jnp.arange(n_rows, dtype=F32)[:, None] * inv
    ac = jnp.arange(GRID_W, dtype=F32)[:, None] * inv
    shape = (n_rows, GRID_W, ROT_AXIS // 2)

    def grid(fr, fc):
        r = jnp.broadcast_to(fr[:, None, :], shape)
        c = jnp.broadcast_to(fc[None, :, :], shape)
        return r, c

    cr, cc = grid(jnp.cos(ar), jnp.cos(ac))
    sr, sc = grid(jnp.sin(ar), jnp.sin(ac))
    cos = jnp.concatenate([cr, cc, cr, cc], axis=-1).reshape(seq, HEAD_DIM)
    sin_signed = jnp.concatenate([-sr, -sc, sr, sc], axis=-1).reshape(seq, HEAD_DIM)
    return cos, sin_signed


def _pair_split(v):
    quarter = HEAD_DIM // 4
    q = v.reshape(v.shape[:-1] + (v.shape[-1] // HEAD_DIM, 4, quarter))
    q = jnp.concatenate([q[..., 0:1, :], q[..., 2:3, :], q[..., 1:2, :], q[..., 3:4, :]], axis=-2)
    return q.reshape(v.shape)


CAST_ROWS = 128
ROTARY_SPANS = ((OFF_BQ, OFF_BV), (OFF_DQ, OFF_DV))


def _cast_w_in_kernel(w_ref, o_ref):
    quarter = HEAD_DIM // 4
    lane = lax.broadcasted_iota(jnp.int32, (w_ref.shape[0], HEAD_DIM), 1)
    second = (lane >= quarter) & (lane < 2 * quarter)
    third = (lane >= 2 * quarter) & (lane < 3 * quarter)
    edges = sorted({0, w_ref.shape[1]} | {e for span in ROTARY_SPANS for e in span})
    for lo, hi in zip(edges[:-1], edges[1:]):
        if (lo, hi) in ROTARY_SPANS:
            for c in range(lo, hi, HEAD_DIM):
                xh = w_ref[:, c:c + HEAD_DIM]
                xh = jnp.where(second, pltpu.roll(xh, HEAD_DIM - quarter, 1),
                               jnp.where(third, pltpu.roll(xh, quarter, 1), xh))
                o_ref[:, c:c + HEAD_DIM] = xh.astype(BF16)
        else:
            o_ref[:, lo:hi] = w_ref[:, lo:hi].astype(BF16)


def _cast_w_in(w):
    depth, d, p = w.shape
    return pl.pallas_call(
        _cast_w_in_kernel,
        out_shape=jax.ShapeDtypeStruct(w.shape, BF16),
        grid=(depth, d // CAST_ROWS),
        in_specs=[pl.BlockSpec((None, CAST_ROWS, p), lambda l, i: (l, i, 0))],
        out_specs=pl.BlockSpec((None, CAST_ROWS, p), lambda l, i: (l, i, 0)),
        compiler_params=_cparams(("parallel", "parallel")),
        name="cast_w_in",
    )(w)


def _tile(n, pref):
    return pref if n % pref == 0 else n


class _Tiles(NamedTuple):
    proj: int
    rows: int
    conv_a: int
    conv_c: int
    win_q: int
    flash_q: int
    flash_k: int


def _tiles(seq):
    return _Tiles(proj=_tile(seq, 1024), rows=_tile(seq, 512), conv_a=_tile(seq, 512), conv_c=_tile(seq, 512),
                  win_q=_tile(seq, 512), flash_q=_tile(seq, 1024), flash_k=_tile(seq, 8192))


def _forward(x, c, ctx, c_ctx, w_ada, b_ada, norm_mix, norm_ffn, w_in, b_gate, conv_a_w, sink_b,
             qk_norm_q, qk_norm_k, conv_c_w, conv_c_b, ln_c_g, ln_c_b, w_branch, w_out,
             w_ffn_in, w_ffn_out, norm_final):
    batch, seq, _ = x.shape
    n_ctx = ctx.shape[1]
    depth = w_in.shape[0]
    xs = x.reshape(batch * seq, D_MODEL)
    xc = ctx.reshape(batch * n_ctx, D_MODEL)

    tiles = _tiles(seq)
    tm = tiles.proj

    rope_x = _rope_tables(seq)
    rope_c = (jnp.ones((n_ctx, HEAD_DIM), F32), jnp.zeros((n_ctx, HEAD_DIM), F32))
    cvec = jnp.zeros((ADA_ROWS, D_MODEL), F32).at[:batch].set(c).at[batch].set(c_ctx)

    def row(v):
        return v.reshape(1, -1)

    w_proj = w_gate = _cast_w_in(w_in)
    w_br = w_branch.astype(BF16)
    w_o = w_out.astype(BF16)
    w_f1 = w_ffn_in.astype(BF16)
    w_f2 = w_ffn_out.astype(BF16)

    def mixers(pf, q, k, vt, k_c, vt_c, l, *, s, tma, tmc, is_ctx):
        y_a = _conv_a(pf, conv_a_w[l], seq=s, tm=tma)
        y_c = _conv_c(pf, conv_c_w[l], row(conv_c_b[l]), row(ln_c_g[l]), row(ln_c_b[l]), seq=s, tm=tmc)
        if is_ctx:
            y_b = _flash(q, k, vt, None, None, sink_b[l], batch=batch, seq_q=s, seq_k=s, seq_x=0, tq=s, tk=s,
                         branch=0, use_sink=True)
            y_d = _flash(q, k, vt, None, None, sink_b[l], batch=batch, seq_q=s, seq_k=s, seq_x=0, tq=s, tk=s,
                         branch=1, use_sink=False)
        else:
            y_b = _win_attn(q, k, vt, k_c, vt_c, sink_b[l], batch=batch, seq=s, ctx=n_ctx, tq=tiles.win_q)
            y_d = _flash(q, k, vt, k_c, vt_c, sink_b[l], batch=batch, seq_q=s, seq_k=s, seq_x=n_ctx,
                         tq=tiles.flash_q, tk=tiles.flash_k, branch=1, use_sink=False)
        return [y_a, y_b, y_c, y_d]

    for l in range(depth):
        last = l == depth - 1
        mod = _ada(cvec, w_ada, row(b_ada[l]), layer=l)
        mx = mod[:batch].reshape(batch, 1, 6, D_MODEL)
        mcx = jnp.broadcast_to(mod[batch].reshape(1, 1, 6, D_MODEL), (batch, 1, 6, D_MODEL))
        sh_m, sc_m, g_m, sh_f, sc_f, g_f = (mx[:, :, k] for k in range(6))
        csh_m, csc_m, cg_m, csh_f, csc_f, cg_f = (mcx[:, :, k] for k in range(6))
        gq, gk = row(_pair_split(qk_norm_q[l])), row(_pair_split(qk_norm_k[l]))

        pf_c, q_c, h_c, k_c, vt_c = _inproj(xc, row(norm_mix[l]), csc_m, csh_m, w_proj, *rope_c, gq, gk,
                                            layer=l, seq=n_ctx, tm=n_ctx)
        pf, q, h, k, vt = _inproj(xs, row(norm_mix[l]), sc_m, sh_m, w_proj, *rope_x, gq, gk,
                                  layer=l, seq=seq, tm=tm)

        ys = mixers(pf, q, k, vt, k_c, vt_c, l, s=seq, tma=tiles.conv_a, tmc=tiles.conv_c, is_ctx=False)
        m = _merge(h, ys, w_gate, w_br, b_gate[l], layer=l, tm=tm, tn=512)
        xs = _resid(m, w_o, xs, g_m, layer=l, seq=seq, tm=tiles.rows, tk=D_MODEL)

        if not last:
            ys_c = mixers(pf_c, q_c, k_c, vt_c, None, None, l, s=n_ctx, tma=n_ctx, tmc=n_ctx, is_ctx=True)
            t_c = batch * n_ctx
            m_c = _merge(h_c, ys_c, w_gate, w_br, b_gate[l], layer=l, tm=t_c, tn=512)
            xc = _resid(m_c, w_o, xc, cg_m, layer=l, seq=t_c, tm=t_c, tk=D_MODEL)
            hid_c = _ffn_in(xc, row(norm_ffn[l]), csc_f, csh_f, w_f1, layer=l, seq=t_c, tm=t_c, tn=512)
            xc = _resid(hid_c, w_f2, xc, cg_f, layer=l, seq=t_c, tm=t_c, tk=FFN_HIDDEN // 2)

        hid = _ffn_in(xs, row(norm_ffn[l]), sc_f, sh_f, w_f1, layer=l, seq=seq, tm=tm, tn=512)
        if last:
            xs = _resid_norm(hid, w_f2, xs, g_f, row(norm_final), layer=l, seq=seq, tm=tiles.rows,
                             tk=FFN_HIDDEN // 2)
        else:
            xs = _resid(hid, w_f2, xs, g_f, layer=l, seq=seq, tm=tiles.rows, tk=FFN_HIDDEN // 2)

    return xs.reshape(batch, seq, D_MODEL)


def kernel(x, c, ctx, c_ctx, w_ada, b_ada, norm_mix, norm_ffn, w_in, b_gate, conv_a_w, sink_b, qk_norm_q,
           qk_norm_k, conv_c_w, conv_c_b, ln_c_g, ln_c_b, w_branch, w_out, w_ffn_in, w_ffn_out, norm_final):
    return _forward(x, c, ctx, c_ctx, w_ada, b_ada, norm_mix, norm_ffn, w_in, b_gate, conv_a_w, sink_b,
                    qk_norm_q, qk_norm_k, conv_c_w, conv_c_b, ln_c_g, ln_c_b, w_branch, w_out,
                    w_ffn_in, w_ffn_out, norm_final)
```
